```python
import jax
import jax.numpy as jnp
from jax import lax
import numpy as np

D_MODEL = 1024
BATCH = 8
SEQ = 2048
DEPTH = 2

DN_HEADS = 4
DN_DK = 128
DN_DV = 128
DN_CONV = 4
DN_CHUNK = 64
DN_WIDTH = DN_HEADS * DN_DV
POOL_WINDOWS = (2, 4, 8, 16)
POOL_GROUP = 128
POOL_WIDTH = POOL_GROUP * len(POOL_WINDOWS)
SA_HEADS = 4
SA_HEAD_DIM = 128
SA_WIDTH = SA_HEADS * SA_HEAD_DIM
IDX_HEADS = 4
IDX_DIM = 64
TOPK_MAX = 256
Q_BLOCK = 128
ROPE_THETA = 10000.0
NORM_EPS = 1e-6
N_BRANCH = 3
BRANCH_WIDTH = 512
SPLIT_SIZES = (DN_WIDTH, DN_WIDTH, DN_WIDTH, DN_WIDTH, DN_HEADS, DN_HEADS,
               POOL_WIDTH, POOL_WIDTH,
               SA_WIDTH, SA_WIDTH, SA_WIDTH, SA_WIDTH, IDX_HEADS * IDX_DIM, IDX_DIM, IDX_HEADS,
               N_BRANCH * D_MODEL)
IN_WIDTH = sum(SPLIT_SIZES)

kernel_name = 'hybrid_deltanet_pool_dsa'


def rms_norm(x, g):
    x32 = x.astype(jnp.float32)
    y = x32 * lax.rsqrt(jnp.mean(x32 * x32, axis=-1, keepdims=True) + NORM_EPS)
    return y.astype(x.dtype) * g


def l2_normalize(x):
    x32 = x.astype(jnp.float32)
    return x32 * lax.rsqrt(jnp.sum(x32 * x32, axis=-1, keepdims=True) + NORM_EPS)


def rope_tables(seq_len, dim):
    inv_freq = ROPE_THETA ** (-jnp.arange(0, dim, 2, dtype=jnp.float32) / dim)
    ang = jnp.arange(seq_len, dtype=jnp.float32)[:, None] * inv_freq[None, :]
    return jnp.cos(ang), jnp.sin(ang)


def apply_rope(x, cos, sin):
    x32 = x.astype(jnp.float32)
    x1, x2 = jnp.split(x32, 2, axis=-1)
    c = cos[None, :, None, :]
    s = sin[None, :, None, :]
    return jnp.concatenate([x1 * c - x2 * s, x2 * c + x1 * s], axis=-1).astype(x.dtype)


def split_columns(proj):
    parts = []
    start = 0
    for size in SPLIT_SIZES:
        parts.append(proj[..., start:start + size])
        start += size
    return parts


def causal_depthwise_conv(x, w):
    k = w.shape[0]
    return lax.conv_general_dilated(x, w[:, None, :], window_strides=(1,), padding=((k - 1, 0),),
                                    dimension_numbers=('NWC', 'WIO', 'NWC'),
                                    feature_group_count=x.shape[-1])


def chunk_gated_delta_rule(q, k, v, beta, g):
    b, t, h, dk = q.shape
    dv = v.shape[-1]
    n = t // DN_CHUNK
    c = DN_CHUNK
    to_chunks4 = lambda a: a.reshape(b, n, c, h, a.shape[-1]).transpose(1, 0, 3, 2, 4)
    to_chunks3 = lambda a: a.reshape(b, n, c, h).transpose(1, 0, 3, 2)
    qc = to_chunks4(q * (dk ** -0.5))
    kc = to_chunks4(k)
    vc = to_chunks4(v)
    bc = to_chunks3(beta)
    gc = jnp.cumsum(to_chunks3(g), axis=-1)
    tril = jnp.tril(jnp.ones((c, c), dtype=bool))
    strict = jnp.tril(jnp.ones((c, c), dtype=bool), -1)
    decay = jnp.exp(jnp.where(tril, gc[..., :, None] - gc[..., None, :], -jnp.inf))
    k_beta = kc * bc[..., None]
    v_beta = vc * bc[..., None]
    lmat = jnp.where(strict, jnp.einsum('nbhid,nbhjd->nbhij', k_beta, kc) * decay, 0.0)
    eye = jnp.eye(c, dtype=jnp.float32)
    tmat = lax.linalg.triangular_solve(eye + lmat, jnp.broadcast_to(eye, lmat.shape),
                                       left_side=True, lower=True, unit_diagonal=True)
    u = jnp.einsum('nbhij,nbhjd->nbhid', tmat, v_beta)
    w = jnp.einsum('nbhij,nbhjd->nbhid', tmat, k_beta * jnp.exp(gc)[..., None])
    attn_intra = jnp.where(tril, jnp.einsum('nbhid,nbhjd->nbhij', qc, kc) * decay, 0.0)

    def step(state, xs):
        q_i, k_i, u_i, w_i, g_i, a_i = xs
        v_new = u_i - jnp.einsum('bhck,bhkv->bhcv', w_i, state)
        o = (jnp.einsum('bhck,bhkv->bhcv', q_i * jnp.exp(g_i)[..., None], state)
             + jnp.einsum('bhij,bhjv->bhiv', a_i, v_new))
        g_last = g_i[..., -1:]
        state = (state * jnp.exp(g_last)[..., None]
                 + jnp.einsum('bhck,bhcv->bhkv', k_i * jnp.exp(g_last - g_i)[..., None], v_new))
        return state, o

    state0 = jnp.zeros((b, h, dk, dv), dtype=jnp.float32)
    _, o = lax.scan(step, state0, (qc, kc, u, w, gc, attn_intra))
    return o.transpose(1, 0, 3, 2, 4).reshape(b, t, h, dv)


def deltanet_branch(qa, ka, va, za, ba, aa, conv_w, a_log, dt_bias, o_norm):
    b, t, _ = qa.shape
    qkv = jax.nn.silu(causal_depthwise_conv(jnp.concatenate([qa, ka, va], axis=-1), conv_w))
    q, k, v = jnp.split(qkv, 3, axis=-1)
    q = l2_normalize(q.reshape(b, t, DN_HEADS, DN_DK))
    k = l2_normalize(k.reshape(b, t, DN_HEADS, DN_DK))
    v = v.reshape(b, t, DN_HEADS, DN_DV).astype(jnp.float32)
    beta = jax.nn.sigmoid(ba.astype(jnp.float32))
    g = -jnp.exp(a_log.astype(jnp.float32)) * jax.nn.softplus(aa.astype(jnp.float32) + dt_bias.astype(jnp.float32))
    o = chunk_gated_delta_rule(q, k, v, beta, g)
    o = rms_norm(o, o_norm) * jax.nn.silu(za.reshape(b, t, DN_HEADS, DN_DV))
    return o.reshape(b, t, DN_WIDTH)


def pooling_branch(u, z, pool_w, pool_scale):
    b, t, _ = u.shape
    u32 = u.astype(jnp.float32)
    cs = jnp.cumsum(u32, axis=1)
    pos_count = jnp.arange(1, t + 1, dtype=jnp.int32)
    groups = []
    for gi, win in enumerate(POOL_WINDOWS):
        sl = slice(gi * POOL_GROUP, (gi + 1) * POOL_GROUP)
        cs_g = cs[..., sl]
        lagged = jnp.pad(cs_g, ((0, 0), (win, 0), (0, 0)))[:, :t]
        count = jnp.minimum(pos_count, win).astype(jnp.float32)[None, :, None]
        groups.append((cs_g - lagged) / count - u32[..., sl])
    pooled = jnp.stack(groups, axis=2)
    mixed = jnp.einsum('btgc,gcd->btgd', pooled.astype(pool_w.dtype), pool_w).reshape(b, t, POOL_WIDTH)
    return mixed * pool_scale * jax.nn.silu(z)


def sparse_attention_branch(qc, kc, vc, zc, iq, ik, iw, q_norm, k_norm, cos_a, sin_a, cos_i, sin_i):
    b, t, _ = qc.shape
    q = apply_rope(rms_norm(qc.reshape(b, t, SA_HEADS, SA_HEAD_DIM), q_norm), cos_a, sin_a)
    k = apply_rope(rms_norm(kc.reshape(b, t, SA_HEADS, SA_HEAD_DIM), k_norm), cos_a, sin_a)
    v = vc.reshape(b, t, SA_HEADS, SA_HEAD_DIM)
    qi = apply_rope(iq.reshape(b, t, IDX_HEADS, IDX_DIM), cos_i, sin_i)
    ki = apply_rope(ik.reshape(b, t, 1, IDX_DIM), cos_i, sin_i)[:, :, 0].astype(jnp.float32)
    wi = iw.astype(jnp.float32) * (IDX_HEADS ** -0.5 * IDX_DIM ** -0.5)
    topk = min(TOPK_MAX, t // 4)
    nb = t // Q_BLOCK
    to_blocks = lambda a: a.reshape((b, nb, Q_BLOCK) + a.shape[2:]).swapaxes(0, 1)
    key_pos = jnp.arange(t, dtype=jnp.int32)
    gather_rows = jax.vmap(lambda src, idx: src[idx])

    def attend(xs):
        q_b, qi_b, wi_b, start = xs
        q_pos = start + jnp.arange(Q_BLOCK, dtype=jnp.int32)
        causal = key_pos[None, :] <= q_pos[:, None]
        dots = jnp.einsum('bqhd,bsd->bqhs', qi_b.astype(jnp.float32), ki)
        score = jnp.einsum('bqh,bqhs->bqs', wi_b, jax.nn.relu(dots))
        score = jnp.where(causal[None], score, -jnp.inf)
        _, idx = lax.top_k(score, topk)
        k_sel = gather_rows(k, idx)
        v_sel = gather_rows(v, idx)
        logits = jnp.einsum('bqhd,bqkhd->bhqk', q_b.astype(jnp.float32), k_sel.astype(jnp.float32)) * (SA_HEAD_DIM ** -0.5)
        valid = idx <= q_pos[None, :, None]
        logits = jnp.where(valid[:, None], logits, -jnp.inf)
        p = jax.nn.softmax(logits, axis=-1)
        return jnp.einsum('bhqk,bqkhd->bqhd', p.astype(v_sel.dtype), v_sel)

    starts = jnp.arange(nb, dtype=jnp.int32) * Q_BLOCK
    o = lax.map(attend, (to_blocks(q), to_blocks(qi), to_blocks(wi), starts))
    o = o.swapaxes(0, 1).reshape(b, t, SA_WIDTH)
    return o * jax.nn.silu(zc)


def hybrid_layer(x, norm_g, w_in, gate_b, conv_w, a_log, dt_bias, dn_onorm, pool_w, pool_scale,
                 q_norm, k_norm, w_branch, w_out, cos_a, sin_a, cos_i, sin_i):
    b, t, _ = x.shape
    h = rms_norm(x, norm_g)
    proj = jnp.einsum('btd,de->bte', h, w_in)
    (qa, ka, va, za, ba, aa, ub, zb, qc, kc, vc, zc, iq, ik, iw, gates) = split_columns(proj)
    ya = deltanet_branch(qa, ka, va, za, ba, aa, conv_w, a_log, dt_bias, dn_onorm)
    yb = pooling_branch(ub, zb, pool_w, pool_scale)
    yc = sparse_attention_branch(qc, kc, vc, zc, iq, ik, iw, q_norm, k_norm, cos_a, sin_a, cos_i, sin_i)
    branches = jnp.stack([ya, yb, yc], axis=2).astype(w_branch.dtype)
    projected = jnp.einsum('btnc,ncd->btnd', branches, w_branch)
    gate = jax.nn.sigmoid(gates.reshape(b, t, N_BRANCH, D_MODEL).astype(jnp.float32) + gate_b.astype(jnp.float32))
    merged = jnp.sum(gate * projected, axis=2).astype(w_out.dtype)
    return x + jnp.einsum('btd,de->bte', merged, w_out).astype(x.dtype)


def setup_inputs(seed: int = 0) -> dict:
    key = jax.random.key(seed)
    ks = jax.random.split(key, 16)
    f32 = jnp.float32
    x = jax.random.normal(ks[0], (BATCH, SEQ, D_MODEL), f32)
    norm_g = 1.0 + 0.02 * jax.random.normal(ks[1], (DEPTH, D_MODEL), f32)
    w_in = jax.random.normal(ks[2], (DEPTH, D_MODEL, IN_WIDTH), f32) * D_MODEL ** -0.5
    gate_b = 0.02 * jax.random.normal(ks[3], (DEPTH, N_BRANCH, D_MODEL), f32)
    conv_w = jax.random.normal(ks[4], (DEPTH, DN_CONV, 3 * DN_WIDTH), f32) * DN_CONV ** -0.5
    a_log = jnp.log(jax.random.uniform(ks[5], (DEPTH, DN_HEADS), f32, minval=1.0, maxval=16.0))
    dt = jnp.exp(jax.random.uniform(ks[6], (DEPTH, DN_HEADS), f32, minval=float(np.log(1e-3)), maxval=float(np.log(1e-1))))
    dt_bias = dt + jnp.log(-jnp.expm1(-dt))
    dn_onorm = 1.0 + 0.02 * jax.random.normal(ks[7], (DEPTH, DN_DV), f32)
    pool_w = jax.random.normal(ks[8], (DEPTH, len(POOL_WINDOWS), POOL_GROUP, POOL_GROUP), f32) * POOL_GROUP ** -0.5
    pool_scale = 1.0 + 0.1 * jax.random.normal(ks[9], (DEPTH, POOL_WIDTH), f32)
    q_norm = 1.0 + 0.02 * jax.random.normal(ks[10], (DEPTH, SA_HEAD_DIM), f32)
    k_norm = 1.0 + 0.02 * jax.random.normal(ks[11], (DEPTH, SA_HEAD_DIM), f32)
    w_branch = jax.random.normal(ks[12], (DEPTH, N_BRANCH, BRANCH_WIDTH, D_MODEL), f32) * BRANCH_WIDTH ** -0.5
    w_out = jax.random.normal(ks[13], (DEPTH, D_MODEL, D_MODEL), f32) * D_MODEL ** -0.5
    return {'x': x, 'norm_g': norm_g, 'w_in': w_in, 'gate_b': gate_b, 'conv_w': conv_w,
            'a_log': a_log, 'dt_bias': dt_bias, 'dn_onorm': dn_onorm, 'pool_w': pool_w,
            'pool_scale': pool_scale, 'q_norm': q_norm, 'k_norm': k_norm,
            'w_branch': w_branch, 'w_out': w_out}


def reference(x, norm_g, w_in, gate_b, conv_w, a_log, dt_bias, dn_onorm, pool_w, pool_scale,
              q_norm, k_norm, w_branch, w_out):
    t = x.shape[1]
    cos_a, sin_a = rope_tables(t, SA_HEAD_DIM)
    cos_i, sin_i = rope_tables(t, IDX_DIM)
    for layer in range(DEPTH):
        x = hybrid_layer(x, norm_g[layer], w_in[layer], gate_b[layer], conv_w[layer], a_log[layer],
                         dt_bias[layer], dn_onorm[layer], pool_w[layer], pool_scale[layer],
                         q_norm[layer], k_norm[layer], w_branch[layer], w_out[layer],
                         cos_a, sin_a, cos_i, sin_i)
    return x
```

```python
import functools

import numpy as np
import jax
import jax.numpy as jnp
from jax import lax
from jax.experimental import pallas as pl
from jax.experimental.pallas import tpu as pltpu

F32 = jnp.float32
BF16 = jnp.bfloat16
HIGHEST = lax.Precision.HIGHEST

D_MODEL = 1024
HEADS = 4
HEAD_DIM = 128
BRANCH_W = HEADS * HEAD_DIM
DN_CONV = 4
POOL_WINDOWS = (2, 4, 8, 16)
POOL_GROUP = 128
IDX_DIM = 64
TOPK_MAX = 256
ROPE_THETA = 10000.0
NORM_EPS = 1e-6
N_BRANCH = 3

COL_QA, COL_KA, COL_VA, COL_ZA = 0, 512, 1024, 1536
COL_UB, COL_ZB = 2048, 2560
COL_QC, COL_KC, COL_VC, COL_ZC = 3072, 3584, 4096, 4608
COL_GATES = 5120
BIG_W = 8192
SMALL_W = 384
SM_BETA, SM_DECAY, SM_IDXW = 64, 68, 72

V7X_VMEM_LIMIT = 56 * 1024 * 1024

PROJ_TM = 512
PROJ_TN = 512
MERGE_TM = 512
PREP_TT = 512
POOL_TT = 256
POOL_HALO = 16
DN_CHUNK = 128
DN_HALO = 16
ATT_QB = 256
ATT_KC = 256
NUM_BISECT = 36
MASK_NEG = -1e30


def _nt_dot(a, b):
    return lax.dot_general(a, b, (((1,), (1,)), ((), ())), preferred_element_type=F32)


def _dot(a, b, precision=None):
    return jnp.dot(a, b, preferred_element_type=F32, precision=precision)


def _silu(x):
    return x * jax.nn.sigmoid(x)


def _cparams(n_axes):
    return pltpu.CompilerParams(dimension_semantics=("arbitrary",) * n_axes,
                                vmem_limit_bytes=V7X_VMEM_LIMIT)


def _inproj_kernel(x_ref, g_ref, w_ref, big_ref, small_ref):
    x = x_ref[...]
    ms = jnp.mean(x * x, axis=-1, keepdims=True)
    h = ((x * lax.rsqrt(ms + NORM_EPS)) * g_ref[...]).astype(BF16)
    for c in range(0, BIG_W, PROJ_TN):
        big_ref[:, c:c + PROJ_TN] = _dot(h, w_ref[:, c:c + PROJ_TN]).astype(big_ref.dtype)
    small_ref[...] = _dot(h, w_ref[:, BIG_W:])


def _in_projection(x2d, g_row, w_p):
    m = x2d.shape[0]
    return pl.pallas_call(
        _inproj_kernel,
        grid=(m // PROJ_TM,),
        in_specs=[
            pl.BlockSpec((PROJ_TM, D_MODEL), lambda i: (i, 0)),
            pl.BlockSpec((1, D_MODEL), lambda i: (0, 0)),
            pl.BlockSpec((D_MODEL, BIG_W + SMALL_W), lambda i: (0, 0), pipeline_mode=pl.Buffered(1)),
        ],
        out_specs=[
            pl.BlockSpec((PROJ_TM, BIG_W), lambda i: (i, 0)),
            pl.BlockSpec((PROJ_TM, SMALL_W), lambda i: (i, 0)),
        ],
        out_shape=[
            jax.ShapeDtypeStruct((m, BIG_W), BF16),
            jax.ShapeDtypeStruct((m, SMALL_W), F32),
        ],
        compiler_params=_cparams(1),
        name="in_projection",
    )(x2d, g_row, w_p)


def _pool_kernel(u_ref, z_ref, pw_ref, ps_ref, y_ref):
    t_len = u_ref.shape[0]
    for t in range(t_len // POOL_TT):
        r0 = t * POOL_TT
        cur = u_ref[r0:r0 + POOL_TT, :].astype(F32)
        if t == 0:
            prev = jnp.zeros((POOL_HALO, cur.shape[1]), F32)
        else:
            prev = u_ref[r0 - POOL_HALO:r0, :].astype(F32)
        win_rows = jnp.concatenate([prev, cur], axis=0)
        pos = r0 + lax.broadcasted_iota(jnp.int32, (POOL_TT, 1), 0)
        for gi, win in enumerate(POOL_WINDOWS):
            sl = slice(gi * POOL_GROUP, (gi + 1) * POOL_GROUP)
            s = win_rows[:, sl]
            shift = 1
            while shift < win:
                s = s + pltpu.roll(s, shift, axis=0)
                shift *= 2
            count = jnp.minimum(pos + 1, win).astype(F32)
            pooled = s[POOL_HALO:, :] / count - cur[:, sl]
            mixed = _dot(pooled.astype(BF16), pw_ref[gi])
            zg = z_ref[r0:r0 + POOL_TT, sl].astype(F32)
            y_ref[r0:r0 + POOL_TT, sl] = (mixed * ps_ref[:, sl] * _silu(zg)).astype(y_ref.dtype)


def _pool_branch(big3d, pool_w, pool_scale_row):
    b, t, _ = big3d.shape
    width = POOL_GROUP * len(POOL_WINDOWS)
    return pl.pallas_call(
        _pool_kernel,
        grid=(b,),
        in_specs=[
            pl.BlockSpec((None, t, width), lambda i: (i, 0, COL_UB // width)),
            pl.BlockSpec((None, t, width), lambda i: (i, 0, COL_ZB // width)),
            pl.BlockSpec((len(POOL_WINDOWS), POOL_GROUP, POOL_GROUP), lambda i: (0, 0, 0)),
            pl.BlockSpec((1, width), lambda i: (0, 0)),
        ],
        out_specs=pl.BlockSpec((None, t, width), lambda i: (i, 0, 0)),
        out_shape=jax.ShapeDtypeStruct((b, t, width), BF16),
        compiler_params=_cparams(1),
        name="pool_branch",
    )(big3d, big3d, pool_w, pool_scale_row)


def _softplus(x):
    return jnp.maximum(x, 0.0) + jnp.log1p(jnp.exp(-jnp.abs(x)))


def _dn_kernel(q_ref, k_ref, v_ref, z_ref, cq_ref, ck_ref, cv_ref, sm_ref, alog_ref, dtb_ref, on_ref,
               y_ref, u_s, w_s, qg_s, a_s, kdt_s, el_s):
    head = pl.program_id(1)
    t_len = q_ref.shape[0]
    n_chunks = t_len // DN_CHUNK
    c = DN_CHUNK
    row = lax.broadcasted_iota(jnp.int32, (c, c), 0)
    col = lax.broadcasted_iota(jnp.int32, (c, c), 1)
    tril = row >= col
    strict = row > col
    tril_f = tril.astype(F32)
    eye_f = (row == col).astype(F32)
    pick_beta = (row == SM_BETA + head).astype(F32)
    pick_decay = (row == SM_DECAY + head).astype(F32)

    def conv_silu(ref, cw_ref, ci):
        start = pl.multiple_of(ci * c, c)
        cur = ref[pl.ds(start, c), :].astype(F32)
        pstart = pl.multiple_of(jnp.maximum(start - DN_HALO, 0), DN_HALO)
        prev = ref[pl.ds(pstart, DN_HALO), :].astype(F32)
        prev = jnp.where(ci > 0, prev, 0.0)
        rows = jnp.concatenate([prev, cur], axis=0)
        cw = cw_ref[...]
        acc = rows * cw[DN_CONV - 1:DN_CONV, :]
        for s in range(1, DN_CONV):
            acc = acc + pltpu.roll(rows, s, axis=0) * cw[DN_CONV - 1 - s:DN_CONV - s, :]
        y = acc[DN_HALO:, :]
        return _silu(y)

    def prepare(ci, carry):
        start = pl.multiple_of(ci * c, c)
        xq = conv_silu(q_ref, cq_ref, ci)
        xk = conv_silu(k_ref, ck_ref, ci)
        xv = conv_silu(v_ref, cv_ref, ci)
        qn = xq * lax.rsqrt(jnp.sum(xq * xq, axis=-1, keepdims=True) + NORM_EPS) * (HEAD_DIM ** -0.5)
        kn = xk * lax.rsqrt(jnp.sum(xk * xk, axis=-1, keepdims=True) + NORM_EPS)
        sm = sm_ref[pl.ds(start, c), :]
        beta_all = jax.nn.sigmoid(sm)
        g_all = -jnp.exp(alog_ref[...]) * _softplus(sm + dtb_ref[...])
        beta_b = _dot(beta_all, pick_beta, HIGHEST)
        g_b = _dot(g_all, pick_decay, HIGHEST)
        gc = _dot(tril_f, g_b, HIGHEST)
        gr = gc.T
        decay = jnp.exp(jnp.where(tril, gc - gr, -jnp.inf))
        kb = kn * beta_b
        vb = xv * beta_b
        kn16 = kn.astype(BF16)
        lmat = jnp.where(strict, _nt_dot(kb.astype(BF16), kn16) * decay, 0.0)
        tmat = eye_f - lmat
        power = _dot(lmat, lmat, HIGHEST)
        span = 2
        while span < c:
            tmat = tmat + _dot(tmat, power, HIGHEST)
            span *= 2
            if span < c:
                power = _dot(power, power, HIGHEST)
        t16 = tmat.astype(BF16)
        egc = jnp.exp(gc)
        u_s[pl.ds(start, c), :] = _dot(t16, vb.astype(BF16))
        w_s[pl.ds(start, c), :] = _dot(t16, (kb * egc).astype(BF16)).astype(BF16)
        qg_s[pl.ds(start, c), :] = (qn * egc).astype(BF16)
        a_s[pl.ds(start, c), :] = jnp.where(tril, _nt_dot(qn.astype(BF16), kn16) * decay, 0.0).astype(BF16)
        g_last = gc[c - 1:c, :]
        kd = kn * jnp.exp(g_last - gc)
        kdt_s[pl.ds(start, c), :] = kd.T.astype(BF16)
        el_s[pl.ds(ci, 1), :] = jnp.exp(g_last)
        return carry

    lax.fori_loop(0, n_chunks, prepare, 0)

    def scan(ci, state):
        start = pl.multiple_of(ci * c, c)
        s16 = state.astype(BF16)
        v_new = u_s[pl.ds(start, c), :] - _dot(w_s[pl.ds(start, c), :], s16)
        v16 = v_new.astype(BF16)
        o = _dot(qg_s[pl.ds(start, c), :], s16) + _dot(a_s[pl.ds(start, c), :], v16)
        on = o * lax.rsqrt(jnp.mean(o * o, axis=-1, keepdims=True) + NORM_EPS) * on_ref[...]
        zg = z_ref[pl.ds(start, c), :].astype(F32)
        y_ref[pl.ds(start, c), :] = (on * _silu(zg)).astype(y_ref.dtype)
        return state * el_s[pl.ds(ci, 1), :] + _dot(kdt_s[pl.ds(start, c), :], v16)

    lax.fori_loop(0, n_chunks, scan, jnp.zeros((HEAD_DIM, HEAD_DIM), F32))


def _deltanet_branch(big3d, small3d, conv_q, conv_k, conv_v, alog_row, dtb_row, onorm_row):
    b, t, _ = big3d.shape
    hd = HEAD_DIM
    seq_spec = lambda col: pl.BlockSpec((None, t, hd), lambda i, h, col=col: (i, 0, col // hd + h))
    conv_spec = pl.BlockSpec((DN_CONV, hd), lambda i, h: (0, h))
    row_spec = pl.BlockSpec((1, hd), lambda i, h: (0, 0))
    return pl.pallas_call(
        _dn_kernel,
        grid=(b, HEADS),
        in_specs=[
            seq_spec(COL_QA), seq_spec(COL_KA), seq_spec(COL_VA), seq_spec(COL_ZA),
            conv_spec, conv_spec, conv_spec,
            pl.BlockSpec((None, t, hd), lambda i, h: (i, 0, SMALL_W // hd - 1)),
            row_spec, row_spec, row_spec,
        ],
        out_specs=pl.BlockSpec((None, t, hd), lambda i, h: (i, 0, h)),
        out_shape=jax.ShapeDtypeStruct((b, t, BRANCH_W), BF16),
        scratch_shapes=[
            pltpu.VMEM((t, hd), F32),
            pltpu.VMEM((t, hd), BF16),
            pltpu.VMEM((t, hd), BF16),
            pltpu.VMEM((t, hd), BF16),
            pltpu.VMEM((t, hd), BF16),
            pltpu.VMEM((t // DN_CHUNK, hd), F32),
        ],
        compiler_params=_cparams(2),
        name="deltanet_branch",
    )(big3d, big3d, big3d, big3d, conv_q, conv_k, conv_v, small3d, alog_row, dtb_row, onorm_row)


def _attn_prep_kernel(q_ref, k_ref, v_ref, sm_ref, qn_ref, kn_ref, cos_ref, sin_ref,
                      icos_ref, isin_up_ref, isin_dn_ref,
                      qo_ref, ko_ref, vt_ref, qio_ref, kia_ref, kib_ref, wt_ref):
    cos = cos_ref[...]
    sin = sin_ref[...]

    def norm_rope(ref, gain):
        for h in range(HEADS):
            sl = slice(h * HEAD_DIM, (h + 1) * HEAD_DIM)
            x = ref[:, sl].astype(F32)
            xn = (x * lax.rsqrt(jnp.mean(x * x, axis=-1, keepdims=True) + NORM_EPS)) * gain
            yield sl, xn * cos + pltpu.roll(xn, HEAD_DIM // 2, axis=1) * sin

    for sl, val in norm_rope(q_ref, qn_ref[...]):
        qo_ref[:, sl] = val.astype(qo_ref.dtype)
    for sl, val in norm_rope(k_ref, kn_ref[...]):
        ko_ref[:, sl] = val.astype(ko_ref.dtype)
    vt_ref[...] = v_ref[...].astype(F32).T.astype(vt_ref.dtype)

    sm = sm_ref[...]
    iq_w = HEADS * IDX_DIM
    half = IDX_DIM // 2
    iq = sm[:, :iq_w]
    iq_r = (iq * icos_ref[...] + pltpu.roll(iq, iq_w - half, axis=1) * isin_up_ref[...]
            + pltpu.roll(iq, half, axis=1) * isin_dn_ref[...])
    qio_ref[...] = iq_r.astype(qio_ref.dtype)
    last = sm[:, iq_w:]
    lw = last.shape[1]
    ik_r = (last * icos_ref[:, :lw] + pltpu.roll(last, lw - half, axis=1) * isin_up_ref[:, :lw]
            + pltpu.roll(last, half, axis=1) * isin_dn_ref[:, :lw])
    lane = lax.broadcasted_iota(jnp.int32, ik_r.shape, 1)
    ik_r = jnp.where(lane < IDX_DIM, ik_r, 0.0)
    kia_ref[...] = ik_r.astype(kia_ref.dtype)
    kib_ref[...] = pltpu.roll(ik_r, IDX_DIM, axis=1).astype(kib_ref.dtype)
    wt = last.T
    wt_ref[...] = wt[SM_IDXW:SM_IDXW + 8, :] * (HEADS ** -0.5 * IDX_DIM ** -0.5)


def _attn_prep(big3d, small3d, qn_row, kn_row, tabs):
    b, t, _ = big3d.shape
    w = BRANCH_W
    tt = PREP_TT
    iq_w = HEADS * IDX_DIM
    seq = lambda col: pl.BlockSpec((None, tt, w), lambda i, j, col=col: (i, j, col // w))
    tab = lambda width: pl.BlockSpec((tt, width), lambda i, j: (j, 0))
    row = pl.BlockSpec((1, HEAD_DIM), lambda i, j: (0, 0))
    return pl.pallas_call(
        _attn_prep_kernel,
        grid=(b, t // tt),
        in_specs=[
            seq(COL_QC), seq(COL_KC), seq(COL_VC),
            pl.BlockSpec((None, tt, SMALL_W), lambda i, j: (i, j, 0)),
            row, row, tab(HEAD_DIM), tab(HEAD_DIM), tab(iq_w), tab(iq_w), tab(iq_w),
        ],
        out_specs=[
            pl.BlockSpec((None, tt, w), lambda i, j: (i, j, 0)),
            pl.BlockSpec((None, tt, w), lambda i, j: (i, j, 0)),
            pl.BlockSpec((None, w, tt), lambda i, j: (i, 0, j)),
            pl.BlockSpec((None, tt, iq_w), lambda i, j: (i, j, 0)),
            pl.BlockSpec((None, tt, 2 * IDX_DIM), lambda i, j: (i, j, 0)),
            pl.BlockSpec((None, tt, 2 * IDX_DIM), lambda i, j: (i, j, 0)),
            pl.BlockSpec((None, 8, tt), lambda i, j: (i, 0, j)),
        ],
        out_shape=[
            jax.ShapeDtypeStruct((b, t, w), BF16),
            jax.ShapeDtypeStruct((b, t, w), BF16),
            jax.ShapeDtypeStruct((b, w, t), BF16),
            jax.ShapeDtypeStruct((b, t, iq_w), BF16),
            jax.ShapeDtypeStruct((b, t, 2 * IDX_DIM), BF16),
            jax.ShapeDtypeStruct((b, t, 2 * IDX_DIM), BF16),
            jax.ShapeDtypeStruct((b, 8, t), F32),
        ],
        compiler_params=_cparams(2),
        name="attn_prep",
    )(big3d, big3d, big3d, small3d, qn_row, kn_row, *tabs)


def _dsa_kernel(q_ref, k_ref, vt_ref, qi_ref, kia_ref, kib_ref, wt_ref, z_ref, y_ref, s_ref, *, topk):
    qb, kc = ATT_QB, ATT_KC
    blk_i = pl.program_id(1)
    n_kc = blk_i + 1
    q_pos = blk_i * qb + lax.broadcasted_iota(jnp.int32, (1, qb), 1)
    qi = qi_ref[...]
    wt = wt_ref[...]

    def col_sum(x):
        return x.reshape(kc // 8, 8, qb).sum(axis=0)

    def score_body(ci, carry):
        vmax, vmin = carry
        off = pl.multiple_of(ci * kc, kc)
        ka = kia_ref[pl.ds(off, kc), :]
        kb = kib_ref[pl.ds(off, kc), :]
        s = jnp.zeros((kc, qb), F32)
        for h in range(HEADS):
            kk = ka if h % 2 == 0 else kb
            qq = qi[:, (h // 2) * 2 * IDX_DIM:(h // 2 + 1) * 2 * IDX_DIM]
            s = s + jnp.maximum(_nt_dot(kk, qq), 0.0) * wt[h:h + 1, :]
        s = jnp.where(s == 0.0, 0.0, s)
        key_pos = off + lax.broadcasted_iota(jnp.int32, (kc, 1), 0)
        causal = key_pos <= q_pos
        s_ref[pl.ds(off, kc), :] = jnp.where(causal, s, -jnp.inf)
        vmax = jnp.maximum(vmax, jnp.max(jnp.where(causal, s, -jnp.inf), axis=0, keepdims=True))
        vmin = jnp.minimum(vmin, jnp.min(jnp.where(causal, s, jnp.inf), axis=0, keepdims=True))
        return vmax, vmin

    vmax, vmin = lax.fori_loop(0, n_kc, score_body,
                               (jnp.full((1, qb), -jnp.inf, F32), jnp.full((1, qb), jnp.inf, F32)))

    def count_ge(thr):
        def body(ci, acc):
            off = pl.multiple_of(ci * kc, kc)
            blk = s_ref[pl.ds(off, kc), :]
            return acc + col_sum(jnp.where(blk >= thr, 1.0, 0.0))
        acc = lax.fori_loop(0, n_kc, body, jnp.zeros((8, qb), F32))
        return jnp.sum(acc, axis=0, keepdims=True)

    k_sel = jnp.minimum(q_pos + 1, topk).astype(F32)

    def bisect(_, carry):
        lo, hi = carry
        mid = 0.5 * lo + 0.5 * hi
        ge = count_ge(mid) >= k_sel
        return jnp.where(ge, mid, lo), jnp.where(ge, hi, mid)

    lo, hi = lax.fori_loop(0, NUM_BISECT, bisect, (vmin, vmax + (jnp.abs(vmax) + 1.0)))
    need = k_sel - count_ge(hi)

    tri = (lax.broadcasted_iota(jnp.int32, (kc, kc), 1)
           < lax.broadcasted_iota(jnp.int32, (kc, kc), 0)).astype(BF16)

    def mask_body(ci, seen):
        off = pl.multiple_of(ci * kc, kc)
        blk = s_ref[pl.ds(off, kc), :]
        tie = jnp.where((blk >= lo) & (blk < hi), 1.0, 0.0)
        rank = _dot(tri, tie.astype(BF16)) + seen
        sel = (blk >= hi) | ((tie > 0.5) & (rank < need))
        s_ref[pl.ds(off, kc), :] = jnp.where(sel, 1.0, 0.0)
        return seen + jnp.sum(col_sum(tie), axis=0, keepdims=True)

    lax.fori_loop(0, n_kc, mask_body, jnp.zeros((1, qb), F32))

    for h in range(HEADS):
        sl = slice(h * HEAD_DIM, (h + 1) * HEAD_DIM)
        qh = q_ref[:, sl]

        def att_body(ci, carry, sl=sl, qh=qh):
            m, l, acc = carry
            off = pl.multiple_of(ci * kc, kc)
            logits = _nt_dot(k_ref[pl.ds(off, kc), sl], qh) * (HEAD_DIM ** -0.5)
            sel = s_ref[pl.ds(off, kc), :] > 0.5
            lm = jnp.where(sel, logits, MASK_NEG)
            m_new = jnp.maximum(m, jnp.max(lm, axis=0, keepdims=True))
            p = jnp.where(sel, jnp.exp(lm - m_new), 0.0)
            alpha = jnp.exp(m - m_new)
            l_new = alpha * l + jnp.sum(p, axis=0, keepdims=True)
            acc_new = alpha * acc + _dot(vt_ref[sl, pl.ds(off, kc)], p.astype(BF16))
            return m_new, l_new, acc_new

        m, l, acc = lax.fori_loop(
            0, n_kc, att_body,
            (jnp.full((1, qb), MASK_NEG, F32), jnp.zeros((1, qb), F32), jnp.zeros((HEAD_DIM, qb), F32)))
        o = (acc / l).T
        y_ref[:, sl] = (o * _silu(z_ref[:, sl].astype(F32))).astype(y_ref.dtype)


def _dsa_branch(big3d, q_r, k_r, v_t, qi_r, ki_a, ki_b, w_t):
    b, t, _ = big3d.shape
    w = BRANCH_W
    qb = ATT_QB
    topk = min(TOPK_MAX, t // 4)
    return pl.pallas_call(
        functools.partial(_dsa_kernel, topk=topk),
        grid=(b, t // qb),
        in_specs=[
            pl.BlockSpec((None, qb, w), lambda i, j: (i, j, 0)),
            pl.BlockSpec((None, t, w), lambda i, j: (i, 0, 0)),
            pl.BlockSpec((None, w, t), lambda i, j: (i, 0, 0)),
            pl.BlockSpec((None, qb, HEADS * IDX_DIM), lambda i, j: (i, j, 0)),
            pl.BlockSpec((None, t, 2 * IDX_DIM), lambda i, j: (i, 0, 0)),
            pl.BlockSpec((None, t, 2 * IDX_DIM), lambda i, j: (i, 0, 0)),
            pl.BlockSpec((None, 8, qb), lambda i, j: (i, 0, j)),
            pl.BlockSpec((None, qb, w), lambda i, j: (i, j, COL_ZC // w)),
        ],
        out_specs=pl.BlockSpec((None, qb, w), lambda i, j: (i, j, 0)),
        out_shape=jax.ShapeDtypeStruct((b, t, w), BF16),
        scratch_shapes=[pltpu.VMEM((t, qb), F32)],
        compiler_params=_cparams(2),
        name="sparse_attention",
    )(q_r, k_r, v_t, qi_r, ki_a, ki_b, w_t, big3d)


def _merge_kernel(ya_ref, yb_ref, yc_ref, ga_ref, gb_ref, gc_ref, bias_ref, wb_ref, wo_ref, x_ref, o_ref):
    merged = None
    for n, (y_ref, g_ref) in enumerate(((ya_ref, ga_ref), (yb_ref, gb_ref), (yc_ref, gc_ref))):
        proj = _dot(y_ref[...], wb_ref[n])
        gate = jax.nn.sigmoid(g_ref[...].astype(F32) + bias_ref[n:n + 1, :])
        term = gate * proj
        merged = term if merged is None else merged + term
    o_ref[...] = x_ref[...] + _dot(merged.astype(BF16), wo_ref[...])


def _merge(ya, yb, yc, big2d, gate_b, w_branch, w_out, x2d):
    m = x2d.shape[0]
    tm = MERGE_TM
    yspec = pl.BlockSpec((tm, BRANCH_W), lambda i: (i, 0))
    gspec = lambda n: pl.BlockSpec((tm, D_MODEL), lambda i, n=n: (i, COL_GATES // D_MODEL + n))
    return pl.pallas_call(
        _merge_kernel,
        grid=(m // tm,),
        in_specs=[
            yspec, yspec, yspec, gspec(0), gspec(1), gspec(2),
            pl.BlockSpec((N_BRANCH, D_MODEL), lambda i: (0, 0)),
            pl.BlockSpec((N_BRANCH, BRANCH_W, D_MODEL), lambda i: (0, 0, 0)),
            pl.BlockSpec((D_MODEL, D_MODEL), lambda i: (0, 0)),
            pl.BlockSpec((tm, D_MODEL), lambda i: (i, 0)),
        ],
        out_specs=pl.BlockSpec((tm, D_MODEL), lambda i: (i, 0)),
        out_shape=jax.ShapeDtypeStruct((m, D_MODEL), F32),
        compiler_params=_cparams(1),
        name="merge",
    )(ya, yb, yc, big2d, big2d, big2d, gate_b, w_branch, w_out, x2d)


def _permute_w_in(w_in):
    qkvz_a = w_in[..., 0:2048]
    beta = w_in[..., 2048:2052]
    decay = w_in[..., 2052:2056]
    pool = w_in[..., 2056:3080]
    qkvz_c = w_in[..., 3080:5128]
    idx_q = w_in[..., 5128:5384]
    idx_k = w_in[..., 5384:5448]
    idx_w = w_in[..., 5448:5452]
    gates = w_in[..., 5452:8524]
    pad = jnp.zeros(w_in.shape[:-1] + (SMALL_W - 256 - IDX_DIM - 12,), w_in.dtype)
    return jnp.concatenate([qkvz_a, pool, qkvz_c, gates, idx_q, idx_k, beta, decay, idx_w, pad],
                           axis=-1).astype(BF16)


def _rope_tables(t):
    def base(dim):
        inv_freq = ROPE_THETA ** (-jnp.arange(0, dim, 2, dtype=F32) / dim)
        ang = jnp.arange(t, dtype=F32)[:, None] * inv_freq[None, :]
        return jnp.cos(ang), jnp.sin(ang)

    cos_a, sin_a = base(HEAD_DIM)
    cos = jnp.concatenate([cos_a, cos_a], axis=-1)
    sin = jnp.concatenate([-sin_a, sin_a], axis=-1)
    cos_i, sin_i = base(IDX_DIM)
    zero = jnp.zeros_like(sin_i)
    icos = jnp.tile(jnp.concatenate([cos_i, cos_i], axis=-1), (1, HEADS))
    isin_up = jnp.tile(jnp.concatenate([-sin_i, zero], axis=-1), (1, HEADS))
    isin_dn = jnp.tile(jnp.concatenate([zero, sin_i], axis=-1), (1, HEADS))
    return cos, sin, icos, isin_up, isin_dn


def _lane_row(vals, offset):
    d, n = vals.shape
    return jnp.zeros((d, 1, HEAD_DIM), F32).at[:, 0, offset:offset + n].set(vals.astype(F32))


def kernel(x, norm_g, w_in, gate_b, conv_w, a_log, dt_bias, dn_onorm, pool_w, pool_scale, q_norm, k_norm,
           w_branch, w_out):
    b, t, d = x.shape
    depth = norm_g.shape[0]
    w_p = _permute_w_in(w_in)
    wb16 = w_branch.astype(BF16)
    wo16 = w_out.astype(BF16)
    pw16 = pool_w.astype(BF16)
    alog_rows = _lane_row(a_log, SM_DECAY)
    dtb_rows = _lane_row(dt_bias, SM_DECAY)
    tabs = _rope_tables(t)
    x2d = x.reshape(b * t, d)
    for layer in range(depth):
        big2d, small2d = _in_projection(x2d, norm_g[layer][None, :], w_p[layer])
        big3d = big2d.reshape(b, t, BIG_W)
        small3d = small2d.reshape(b, t, SMALL_W)
        cw = conv_w[layer]
        ya = _deltanet_branch(big3d, small3d, cw[:, 0:BRANCH_W], cw[:, BRANCH_W:2 * BRANCH_W],
                              cw[:, 2 * BRANCH_W:], alog_rows[layer], dtb_rows[layer],
                              dn_onorm[layer][None, :])
        yb = _pool_branch(big3d, pw16[layer], pool_scale[layer][None, :])
        prep = _attn_prep(big3d, small3d, q_norm[layer][None, :], k_norm[layer][None, :], tabs)
        yc = _dsa_branch(big3d, *prep)
        x2d = _merge(ya.reshape(b * t, BRANCH_W), yb.reshape(b * t, BRANCH_W), yc.reshape(b * t, BRANCH_W),
                     big2d, gate_b[layer], wb16[layer], wo16[layer], x2d)
    return x2d.reshape(b, t, d)
```

```python
import functools

import numpy as np
import jax
import jax.numpy as jnp
from jax import lax
from jax.experimental import pallas as pl
from jax.experimental.pallas import tpu as pltpu

F32 = jnp.float32
BF16 = jnp.bfloat16
HIGHEST = lax.Precision.HIGHEST

D_MODEL = 1024
HEADS = 4
HEAD_DIM = 128
BRANCH_W = HEADS * HEAD_DIM
DN_CONV = 4
POOL_WINDOWS = (2, 4, 8, 16)
POOL_GROUP = 128
IDX_DIM = 64
TOPK_MAX = 256
ROPE_THETA = 10000.0
NORM_EPS = 1e-6
N_BRANCH = 3

COL_QA, COL_KA, COL_VA, COL_ZA = 0, 512, 1024, 1536
COL_UB, COL_ZB = 2048, 2560
COL_QC, COL_KC, COL_VC, COL_ZC = 3072, 3584, 4096, 4608
COL_GATES = 5120
BIG_W = 8192
SMALL_W = 384
SM_BETA, SM_DECAY, SM_IDXW = 64, 68, 72

V7X_VMEM_LIMIT = 56 * 1024 * 1024

PROJ_TM = 512
PROJ_TN = 512
MERGE_TM = 512
PREP_TT = 512
POOL_TT = 256
POOL_HALO = 16
DN_CHUNK = 128
DN_HALO = 16
ATT_QB = 256
ATT_KC = 256
NUM_BISECT = 36
BISECT_UNROLL = 4
MASK_NEG = -1e30


def _nt_dot(a, b):
    return lax.dot_general(a, b, (((1,), (1,)), ((), ())), preferred_element_type=F32)


def _dot(a, b, precision=None):
    return jnp.dot(a, b, preferred_element_type=F32, precision=precision)


def _silu(x):
    return x * jax.nn.sigmoid(x)


def _cparams(n_axes):
    return pltpu.CompilerParams(dimension_semantics=("arbitrary",) * n_axes,
                                vmem_limit_bytes=V7X_VMEM_LIMIT)


def _inproj_kernel(x_ref, g_ref, w_ref, big_ref, small_ref):
    x = x_ref[...]
    ms = jnp.mean(x * x, axis=-1, keepdims=True)
    h = ((x * lax.rsqrt(ms + NORM_EPS)) * g_ref[...]).astype(BF16)
    for c in range(0, BIG_W, PROJ_TN):
        big_ref[:, c:c + PROJ_TN] = _dot(h, w_ref[:, c:c + PROJ_TN]).astype(big_ref.dtype)
    small_ref[...] = _dot(h, w_ref[:, BIG_W:])


def _in_projection(x2d, g_row, w_p):
    m = x2d.shape[0]
    return pl.pallas_call(
        _inproj_kernel,
        grid=(m // PROJ_TM,),
        in_specs=[
            pl.BlockSpec((PROJ_TM, D_MODEL), lambda i: (i, 0)),
            pl.BlockSpec((1, D_MODEL), lambda i: (0, 0)),
            pl.BlockSpec((D_MODEL, BIG_W + SMALL_W), lambda i: (0, 0), pipeline_mode=pl.Buffered(1)),
        ],
        out_specs=[
            pl.BlockSpec((PROJ_TM, BIG_W), lambda i: (i, 0)),
            pl.BlockSpec((PROJ_TM, SMALL_W), lambda i: (i, 0)),
        ],
        out_shape=[
            jax.ShapeDtypeStruct((m, BIG_W), BF16),
            jax.ShapeDtypeStruct((m, SMALL_W), F32),
        ],
        compiler_params=_cparams(1),
        name="in_projection",
    )(x2d, g_row, w_p)


def _pool_kernel(u_ref, z_ref, pw_ref, ps_ref, y_ref):
    t_len = u_ref.shape[0]
    for t in range(t_len // POOL_TT):
        r0 = t * POOL_TT
        cur = u_ref[r0:r0 + POOL_TT, :].astype(F32)
        if t == 0:
            prev = jnp.zeros((POOL_HALO, cur.shape[1]), F32)
        else:
            prev = u_ref[r0 - POOL_HALO:r0, :].astype(F32)
        win_rows = jnp.concatenate([prev, cur], axis=0)
        pos = r0 + lax.broadcasted_iota(jnp.int32, (POOL_TT, 1), 0)
        for gi, win in enumerate(POOL_WINDOWS):
            sl = slice(gi * POOL_GROUP, (gi + 1) * POOL_GROUP)
            s = win_rows[:, sl]
            shift = 1
            while shift < win:
                s = s + pltpu.roll(s, shift, axis=0)
                shift *= 2
            count = jnp.minimum(pos + 1, win).astype(F32)
            pooled = s[POOL_HALO:, :] / count - cur[:, sl]
            mixed = _dot(pooled.astype(BF16), pw_ref[gi])
            zg = z_ref[r0:r0 + POOL_TT, sl].astype(F32)
            y_ref[r0:r0 + POOL_TT, sl] = (mixed * ps_ref[:, sl] * _silu(zg)).astype(y_ref.dtype)


def _pool_branch(big3d, pool_w, pool_scale_row):
    b, t, _ = big3d.shape
    width = POOL_GROUP * len(POOL_WINDOWS)
    return pl.pallas_call(
        _pool_kernel,
        grid=(b,),
        in_specs=[
            pl.BlockSpec((None, t, width), lambda i: (i, 0, COL_UB // width)),
            pl.BlockSpec((None, t, width), lambda i: (i, 0, COL_ZB // width)),
            pl.BlockSpec((len(POOL_WINDOWS), POOL_GROUP, POOL_GROUP), lambda i: (0, 0, 0)),
            pl.BlockSpec((1, width), lambda i: (0, 0)),
        ],
        out_specs=pl.BlockSpec((None, t, width), lambda i: (i, 0, 0)),
        out_shape=jax.ShapeDtypeStruct((b, t, width), BF16),
        compiler_params=_cparams(1),
        name="pool_branch",
    )(big3d, big3d, pool_w, pool_scale_row)


def _softplus(x):
    return jnp.maximum(x, 0.0) + jnp.log1p(jnp.exp(-jnp.abs(x)))


def _split_bf16(a):
    hi = a.astype(BF16)
    lo = (a - hi.astype(F32)).astype(BF16)
    return hi, lo


def _dot_split(a, b):
    a_hi, a_lo = a
    b_hi, b_lo = b
    return (_dot(jnp.concatenate([a_hi, a_lo], axis=1), jnp.concatenate([b_hi, b_hi], axis=0))
            + _dot(a_hi, b_lo))


def _dn_kernel(q_ref, k_ref, v_ref, z_ref, cq_ref, ck_ref, cv_ref, sm_ref, alog_ref, dtb_ref, on_ref,
               y_ref, u_s, w_s, qg_s, a_s, kdt_s, el_s, st_s):
    t_len = q_ref.shape[0]
    n_chunks = t_len // DN_CHUNK
    c = DN_CHUNK
    hd = HEAD_DIM
    row = lax.broadcasted_iota(jnp.int32, (c, c), 0)
    col = lax.broadcasted_iota(jnp.int32, (c, c), 1)
    tril = row >= col
    strict = row > col
    tril16 = tril.astype(BF16)
    eye_f = (row == col).astype(F32)

    def conv_silu(ref, cw_ref, ci):
        start = pl.multiple_of(ci * c, c)
        cur = ref[pl.ds(start, c), :].astype(F32)
        pstart = pl.multiple_of(jnp.maximum(start - DN_HALO, 0), DN_HALO)
        prev = ref[pl.ds(pstart, DN_HALO), :].astype(F32)
        prev = jnp.where(ci > 0, prev, 0.0)
        rows = jnp.concatenate([prev, cur], axis=0)
        cw = cw_ref[...]
        acc = rows * cw[DN_CONV - 1:DN_CONV, :]
        for s in range(1, DN_CONV):
            acc = acc + pltpu.roll(rows, s, axis=0) * cw[DN_CONV - 1 - s:DN_CONV - s, :]
        return _silu(acc[DN_HALO:, :])

    def prepare(ci, carry):
        start = pl.multiple_of(ci * c, c)
        xq_all = conv_silu(q_ref, cq_ref, ci)
        xk_all = conv_silu(k_ref, ck_ref, ci)
        xv_all = conv_silu(v_ref, cv_ref, ci)
        sm = sm_ref[pl.ds(start, c), :]
        beta_all = jax.nn.sigmoid(sm)
        g_all = -jnp.exp(alog_ref[...]) * _softplus(sm + dtb_ref[...])
        g_hi = g_all.astype(BF16)
        g_r = g_all - g_hi.astype(F32)
        g_mid = g_r.astype(BF16)
        g_lo = (g_r - g_mid.astype(F32)).astype(BF16)
        gc_all = _dot(tril16, g_hi) + (_dot(tril16, g_mid) + _dot(tril16, g_lo))
        hs = range(HEADS)
        sls = [slice(h * hd, (h + 1) * hd) for h in hs]
        qn = [xq_all[:, sl] * (lax.rsqrt(jnp.sum(xq_all[:, sl] * xq_all[:, sl], axis=-1, keepdims=True) + NORM_EPS)
                               * (hd ** -0.5)) for sl in sls]
        kn = [xk_all[:, sl] * lax.rsqrt(jnp.sum(xk_all[:, sl] * xk_all[:, sl], axis=-1, keepdims=True) + NORM_EPS)
              for sl in sls]
        beta_b = [jnp.broadcast_to(beta_all[:, SM_BETA + h:SM_BETA + h + 1], (c, hd)) for h in hs]
        gc = [jnp.broadcast_to(gc_all[:, SM_DECAY + h:SM_DECAY + h + 1], (c, hd)) for h in hs]
        decay = [jnp.exp(jnp.where(tril, gc[h] - gc[h].T, -jnp.inf)) for h in hs]
        kb = [kn[h] * beta_b[h] for h in hs]
        kn16 = [kn[h].astype(BF16) for h in hs]
        kq = [_nt_dot(jnp.concatenate([kb[h].astype(BF16), qn[h].astype(BF16)], axis=0), kn16[h]) for h in hs]
        lmat = [jnp.where(strict, kq[h][:c, :] * decay[h], 0.0) for h in hs]
        tmat = [eye_f - lmat[h] for h in hs]
        parts = [_split_bf16(lmat[h]) for h in hs]
        power = [_dot_split(parts[h], parts[h]) for h in hs]
        span = 2
        while span < c:
            parts = [_split_bf16(power[h]) for h in hs]
            t_parts = [_split_bf16(tmat[h]) for h in hs]
            span *= 2
            if span < c:
                both = [_dot_split((jnp.concatenate([t_parts[h][0], parts[h][0]], axis=0),
                                    jnp.concatenate([t_parts[h][1], parts[h][1]], axis=0)), parts[h]) for h in hs]
                tmat = [tmat[h] + both[h][:c, :] for h in hs]
                power = [both[h][c:, :] for h in hs]
            else:
                tmat = [tmat[h] + _dot_split(t_parts[h], parts[h]) for h in hs]
        for h in hs:
            sl = sls[h]
            t16 = tmat[h].astype(BF16)
            egc = jnp.exp(gc[h])
            uw = _dot(t16, jnp.concatenate([(xv_all[:, sl] * beta_b[h]).astype(BF16), (kb[h] * egc).astype(BF16)],
                                           axis=1))
            u_s[pl.ds(start, c), sl] = uw[:, :hd]
            w_s[pl.ds(start, c), sl] = uw[:, hd:].astype(BF16)
            qg_s[pl.ds(start, c), sl] = (qn[h] * egc).astype(BF16)
            a_s[pl.ds(start, c), sl] = jnp.where(tril, kq[h][c:, :] * decay[h], 0.0).astype(BF16)
            g_last = gc[h][c - 1:c, :]
            kd = kn[h] * jnp.exp(g_last - gc[h])
            kdt_s[pl.ds(start, c), sl] = kd.T.astype(BF16)
            el_s[pl.ds(pl.multiple_of(ci * 8, 8), 8), sl] = jnp.broadcast_to(jnp.exp(g_last), (8, hd))
        return carry

    lax.fori_loop(0, n_chunks, prepare, 0)

    st_s[...] = jnp.zeros(st_s.shape, F32)

    def scan(ci, carry):
        start = pl.multiple_of(ci * c, c)
        hs = range(HEADS)
        sls = [slice(h * hd, (h + 1) * hd) for h in hs]
        state = [st_s[h] for h in hs]
        s16 = [state[h].astype(BF16) for h in hs]
        v_new = [u_s[pl.ds(start, c), sls[h]] - _dot(w_s[pl.ds(start, c), sls[h]], s16[h]) for h in hs]
        v16 = [v_new[h].astype(BF16) for h in hs]
        for h in hs:
            e_last = el_s[pl.ds(pl.multiple_of(ci * 8, 8), 8), sls[h]][0:1, :]
            st_s[h] = state[h] * e_last + _dot(kdt_s[pl.ds(start, c), sls[h]], v16[h])
        for h in hs:
            sl = sls[h]
            o = _dot(qg_s[pl.ds(start, c), sl], s16[h]) + _dot(a_s[pl.ds(start, c), sl], v16[h])
            on = o * lax.rsqrt(jnp.mean(o * o, axis=-1, keepdims=True) + NORM_EPS) * on_ref[...]
            zg = z_ref[pl.ds(start, c), sl].astype(F32)
            y_ref[pl.ds(start, c), sl] = (on * _silu(zg)).astype(y_ref.dtype)
        return carry

    lax.fori_loop(0, n_chunks, scan, 0)


def _deltanet_branch(big3d, small3d, conv_q, conv_k, conv_v, alog_row, dtb_row, onorm_row):
    b, t, _ = big3d.shape
    hd = HEAD_DIM
    w = BRANCH_W
    seq_spec = lambda col: pl.BlockSpec((None, t, w), lambda i, col=col: (i, 0, col // w))
    conv_spec = pl.BlockSpec((DN_CONV, w), lambda i: (0, 0))
    row_spec = pl.BlockSpec((1, hd), lambda i: (0, 0))
    return pl.pallas_call(
        _dn_kernel,
        grid=(b,),
        in_specs=[
            seq_spec(COL_QA), seq_spec(COL_KA), seq_spec(COL_VA), seq_spec(COL_ZA),
            conv_spec, conv_spec, conv_spec,
            pl.BlockSpec((None, t, hd), lambda i: (i, 0, SMALL_W // hd - 1)),
            row_spec, row_spec, row_spec,
        ],
        out_specs=pl.BlockSpec((None, t, w), lambda i: (i, 0, 0)),
        out_shape=jax.ShapeDtypeStruct((b, t, w), BF16),
        scratch_shapes=[
            pltpu.VMEM((t, w), F32),
            pltpu.VMEM((t, w), BF16),
            pltpu.VMEM((t, w), BF16),
            pltpu.VMEM((t, w), BF16),
            pltpu.VMEM((t, w), BF16),
            pltpu.VMEM((8 * t // DN_CHUNK, w), F32),
            pltpu.VMEM((HEADS, hd, hd), F32),
        ],
        compiler_params=_cparams(1),
        name="deltanet_branch",
    )(big3d, big3d, big3d, big3d, conv_q, conv_k, conv_v, small3d, alog_row, dtb_row, onorm_row)


def _attn_prep_kernel(q_ref, k_ref, v_ref, sm_ref, qn_ref, kn_ref, cos_ref, sin_ref,
                      icos_ref, isin_up_ref, isin_dn_ref,
                      qo_ref, ko_ref, vt_ref, qio_ref, kia_ref, kib_ref, wt_ref):
    cos = cos_ref[...]
    sin = sin_ref[...]

    def norm_rope(ref, gain):
        for h in range(HEADS):
            sl = slice(h * HEAD_DIM, (h + 1) * HEAD_DIM)
            x = ref[:, sl].astype(F32)
            xn = (x * lax.rsqrt(jnp.mean(x * x, axis=-1, keepdims=True) + NORM_EPS)) * gain
            yield sl, xn * cos + pltpu.roll(xn, HEAD_DIM // 2, axis=1) * sin

    for sl, val in norm_rope(q_ref, qn_ref[...]):
        qo_ref[:, sl] = val.astype(qo_ref.dtype)
    for sl, val in norm_rope(k_ref, kn_ref[...]):
        ko_ref[:, sl] = val.astype(ko_ref.dtype)
    vt_ref[...] = v_ref[...].astype(F32).T.astype(vt_ref.dtype)

    sm = sm_ref[...]
    iq_w = HEADS * IDX_DIM
    half = IDX_DIM // 2
    iq = sm[:, :iq_w]
    iq_r = (iq * icos_ref[...] + pltpu.roll(iq, iq_w - half, axis=1) * isin_up_ref[...]
            + pltpu.roll(iq, half, axis=1) * isin_dn_ref[...])
    qio_ref[...] = iq_r.astype(qio_ref.dtype)
    last = sm[:, iq_w:]
    lw = last.shape[1]
    ik_r = (last * icos_ref[:, :lw] + pltpu.roll(last, lw - half, axis=1) * isin_up_ref[:, :lw]
            + pltpu.roll(last, half, axis=1) * isin_dn_ref[:, :lw])
    lane = lax.broadcasted_iota(jnp.int32, ik_r.shape, 1)
    ik_r = jnp.where(lane < IDX_DIM, ik_r, 0.0)
    kia_ref[...] = ik_r.astype(kia_ref.dtype)
    kib_ref[...] = pltpu.roll(ik_r, IDX_DIM, axis=1).astype(kib_ref.dtype)
    wt = last.T
    wt_ref[...] = wt[SM_IDXW:SM_IDXW + 8, :] * (HEADS ** -0.5 * IDX_DIM ** -0.5)


def _attn_prep(big3d, small3d, qn_row, kn_row, tabs):
    b, t, _ = big3d.shape
    w = BRANCH_W
    tt = PREP_TT
    iq_w = HEADS * IDX_DIM
    seq = lambda col: pl.BlockSpec((None, tt, w), lambda i, j, col=col: (i, j, col // w))
    tab = lambda width: pl.BlockSpec((tt, width), lambda i, j: (j, 0))
    row = pl.BlockSpec((1, HEAD_DIM), lambda i, j: (0, 0))
    return pl.pallas_call(
        _attn_prep_kernel,
        grid=(b, t // tt),
        in_specs=[
            seq(COL_QC), seq(COL_KC), seq(COL_VC),
            pl.BlockSpec((None, tt, SMALL_W), lambda i, j: (i, j, 0)),
            row, row, tab(HEAD_DIM), tab(HEAD_DIM), tab(iq_w), tab(iq_w), tab(iq_w),
        ],
        out_specs=[
            pl.BlockSpec((None, tt, w), lambda i, j: (i, j, 0)),
            pl.BlockSpec((None, tt, w), lambda i, j: (i, j, 0)),
            pl.BlockSpec((None, w, tt), lambda i, j: (i, 0, j)),
            pl.BlockSpec((None, tt, iq_w), lambda i, j: (i, j, 0)),
            pl.BlockSpec((None, tt, 2 * IDX_DIM), lambda i, j: (i, j, 0)),
            pl.BlockSpec((None, tt, 2 * IDX_DIM), lambda i, j: (i, j, 0)),
            pl.BlockSpec((None, 8, tt), lambda i, j: (i, 0, j)),
        ],
        out_shape=[
            jax.ShapeDtypeStruct((b, t, w), BF16),
            jax.ShapeDtypeStruct((b, t, w), BF16),
            jax.ShapeDtypeStruct((b, w, t), BF16),
            jax.ShapeDtypeStruct((b, t, iq_w), BF16),
            jax.ShapeDtypeStruct((b, t, 2 * IDX_DIM), BF16),
            jax.ShapeDtypeStruct((b, t, 2 * IDX_DIM), BF16),
            jax.ShapeDtypeStruct((b, 8, t), F32),
        ],
        compiler_params=_cparams(2),
        name="attn_prep",
    )(big3d, big3d, big3d, small3d, qn_row, kn_row, *tabs)


def _dsa_kernel(q_ref, k_ref, vt_ref, qi_ref, kia_ref, kib_ref, wt_ref, z_ref, y_ref, s_ref, acc_ref, *, topk):
    qb, kc = ATT_QB, ATT_KC
    blk_i = pl.program_id(1)
    n_kc = blk_i + 1
    q_pos = blk_i * qb + lax.broadcasted_iota(jnp.int32, (1, qb), 1)
    qi = qi_ref[...]
    wt = wt_ref[...]
    inf = jnp.inf

    def col_sum(x):
        return x.reshape(kc // 8, 8, qb).sum(axis=0)

    def total(x):
        return jnp.sum(x, axis=0, keepdims=True)

    def score_body(ci, carry):
        vmax, vmin, min_pos, n_pos, n_nonneg = carry
        off = pl.multiple_of(ci * kc, kc)
        ka = kia_ref[pl.ds(off, kc), :]
        kb = kib_ref[pl.ds(off, kc), :]
        s = jnp.zeros((kc, qb), F32)
        for h in range(HEADS):
            kk = ka if h % 2 == 0 else kb
            qq = qi[:, (h // 2) * 2 * IDX_DIM:(h // 2 + 1) * 2 * IDX_DIM]
            s = s + jnp.maximum(_nt_dot(kk, qq), 0.0) * wt[h:h + 1, :]
        s = jnp.where(s == 0.0, 0.0, s)
        key_pos = off + lax.broadcasted_iota(jnp.int32, (kc, 1), 0)
        causal = key_pos <= q_pos
        sc = jnp.where(causal, s, -inf)
        s_ref[pl.ds(off, kc), :] = sc
        vmax = jnp.maximum(vmax, jnp.max(sc, axis=0, keepdims=True))
        vmin = jnp.minimum(vmin, jnp.min(jnp.where(causal, s, inf), axis=0, keepdims=True))
        min_pos = jnp.minimum(min_pos, jnp.min(jnp.where(sc > 0.0, sc, inf), axis=0, keepdims=True))
        n_pos = n_pos + col_sum(jnp.where(sc > 0.0, 1.0, 0.0))
        n_nonneg = n_nonneg + col_sum(jnp.where(sc >= 0.0, 1.0, 0.0))
        return vmax, vmin, min_pos, n_pos, n_nonneg

    row_inf = jnp.full((1, qb), inf, F32)
    zeros8 = jnp.zeros((8, qb), F32)
    vmax, vmin, min_pos, n_pos, n_nonneg = lax.fori_loop(
        0, n_kc, score_body, (-row_inf, row_inf, row_inf, zeros8, zeros8))
    n_pos = total(n_pos)
    n_nonneg = total(n_nonneg)

    def count_ge(thr):
        def body(ci, acc):
            off = pl.multiple_of(ci * kc, kc)
            blk = s_ref[pl.ds(off, kc), :]
            return acc + col_sum(jnp.where(blk >= thr, 1.0, 0.0))
        return total(lax.fori_loop(0, n_kc, body, zeros8))

    k_sel = jnp.minimum(q_pos + 1, topk).astype(F32)
    n_causal = (q_pos + 1).astype(F32)
    at_zero = (n_pos < k_sel) & (k_sel <= n_nonneg)
    above = k_sel <= n_pos
    lo = jnp.where(at_zero, 0.0, jnp.where(above, min_pos, vmin))
    hi = jnp.where(at_zero, min_pos, jnp.where(above, vmax + (jnp.abs(vmax) + 1.0), 0.0))
    c_lo = jnp.where(at_zero, n_nonneg, jnp.where(above, n_pos, n_causal))
    c_hi = jnp.where(at_zero, n_pos, jnp.where(above, 0.0, n_nonneg))
    done = jnp.where(at_zero | (c_lo == k_sel), 1.0, 0.0)

    def n_open(d):
        return jnp.sum(1.0 - d)

    def bisect_cond(carry):
        it, n_left = carry[0], carry[1]
        return (it < NUM_BISECT) & (n_left > 0.0)

    def bisect_body(carry):
        it, _, lo, hi, c_lo, c_hi, done = carry
        for _ in range(BISECT_UNROLL):
            mid = 0.5 * lo + 0.5 * hi
            cnt = count_ge(mid)
            live = done < 0.5
            up = (cnt >= k_sel) & live
            dn = (cnt < k_sel) & live
            lo = jnp.where(up, mid, lo)
            c_lo = jnp.where(up, cnt, c_lo)
            hi = jnp.where(dn, mid, hi)
            c_hi = jnp.where(dn, cnt, c_hi)
            done = jnp.where(c_lo == k_sel, 1.0, done)
        return it + BISECT_UNROLL, n_open(done), lo, hi, c_lo, c_hi, done

    _, _, lo, hi, c_lo, c_hi, done = lax.while_loop(
        bisect_cond, bisect_body, (jnp.int32(0), n_open(done), lo, hi, c_lo, c_hi, done))
    need = k_sel - c_hi
    n_tied = jnp.sum(jnp.where(c_lo - c_hi > need, 1.0, 0.0))

    def tie_mask():
        tri = (lax.broadcasted_iota(jnp.int32, (kc, kc), 1)
               < lax.broadcasted_iota(jnp.int32, (kc, kc), 0)).astype(BF16)

        def mask_body(ci, seen):
            off = pl.multiple_of(ci * kc, kc)
            blk = s_ref[pl.ds(off, kc), :]
            tie = jnp.where((blk >= lo) & (blk < hi), 1.0, 0.0)
            rank = _dot(tri, tie.astype(BF16)) + seen
            sel = (blk >= hi) | ((tie > 0.5) & (rank < need))
            s_ref[pl.ds(off, kc), :] = jnp.where(sel, 1.0, 0.0)
            return seen + total(col_sum(tie))

        lax.fori_loop(0, n_kc, mask_body, jnp.zeros((1, qb), F32))

    pl.when(n_tied > 0.0)(tie_mask)
    sel_thr = jnp.where(n_tied > 0.0, 0.5, lo)

    hs = range(HEADS)
    sls = [slice(h * HEAD_DIM, (h + 1) * HEAD_DIM) for h in hs]
    qh = [q_ref[:, sl] for sl in sls]
    acc_ref[...] = jnp.zeros(acc_ref.shape, F32)

    def att_body(ci, carry):
        ms, ls = carry
        off = pl.multiple_of(ci * kc, kc)
        sel = s_ref[pl.ds(off, kc), :] >= sel_thr
        logits = [_nt_dot(k_ref[pl.ds(off, kc), sls[h]], qh[h]) * (HEAD_DIM ** -0.5) for h in hs]
        lm = [jnp.where(sel, logits[h], MASK_NEG) for h in hs]
        m_new = [jnp.maximum(ms[h], jnp.max(lm[h], axis=0, keepdims=True)) for h in hs]
        p = [jnp.where(sel, jnp.exp(lm[h] - m_new[h]), 0.0) for h in hs]
        alpha = [jnp.exp(ms[h] - m_new[h]) for h in hs]
        l_new = [alpha[h] * ls[h] + jnp.sum(p[h], axis=0, keepdims=True) for h in hs]
        pv = [_dot(vt_ref[sls[h], pl.ds(off, kc)], p[h].astype(BF16)) for h in hs]
        for h in hs:
            acc_ref[sls[h], :] = alpha[h] * acc_ref[sls[h], :] + pv[h]
        return tuple(m_new), tuple(l_new)

    row_neg = jnp.full((1, qb), MASK_NEG, F32)
    row_zero = jnp.zeros((1, qb), F32)
    _, ls = lax.fori_loop(0, n_kc, att_body, ((row_neg,) * HEADS, (row_zero,) * HEADS))
    for h in hs:
        o = (acc_ref[sls[h], :] / ls[h]).T
        y_ref[:, sls[h]] = (o * _silu(z_ref[:, sls[h]].astype(F32))).astype(y_ref.dtype)


def _dsa_branch(big3d, q_r, k_r, v_t, qi_r, ki_a, ki_b, w_t):
    b, t, _ = big3d.shape
    w = BRANCH_W
    qb = ATT_QB
    topk = min(TOPK_MAX, t // 4)
    return pl.pallas_call(
        functools.partial(_dsa_kernel, topk=topk),
        grid=(b, t // qb),
        in_specs=[
            pl.BlockSpec((None, qb, w), lambda i, j: (i, j, 0)),
            pl.BlockSpec((None, t, w), lambda i, j: (i, 0, 0)),
            pl.BlockSpec((None, w, t), lambda i, j: (i, 0, 0)),
            pl.BlockSpec((None, qb, HEADS * IDX_DIM), lambda i, j: (i, j, 0)),
            pl.BlockSpec((None, t, 2 * IDX_DIM), lambda i, j: (i, 0, 0)),
            pl.BlockSpec((None, t, 2 * IDX_DIM), lambda i, j: (i, 0, 0)),
            pl.BlockSpec((None, 8, qb), lambda i, j: (i, 0, j)),
            pl.BlockSpec((None, qb, w), lambda i, j: (i, j, COL_ZC // w)),
        ],
        out_specs=pl.BlockSpec((None, qb, w), lambda i, j: (i, j, 0)),
        out_shape=jax.ShapeDtypeStruct((b, t, w), BF16),
        scratch_shapes=[pltpu.VMEM((t, qb), F32),
                        pltpu.VMEM((w, qb), F32)],
        compiler_params=_cparams(2),
        name="sparse_attention",
    )(q_r, k_r, v_t, qi_r, ki_a, ki_b, w_t, big3d)


def _merge_kernel(ya_ref, yb_ref, yc_ref, ga_ref, gb_ref, gc_ref, bias_ref, wb_ref, wo_ref, x_ref, o_ref):
    merged = None
    for n, (y_ref, g_ref) in enumerate(((ya_ref, ga_ref), (yb_ref, gb_ref), (yc_ref, gc_ref))):
        proj = _dot(y_ref[...], wb_ref[n])
        gate = jax.nn.sigmoid(g_ref[...].astype(F32) + bias_ref[n:n + 1, :])
        term = gate * proj
        merged = term if merged is None else merged + term
    o_ref[...] = x_ref[...] + _dot(merged.astype(BF16), wo_ref[...])


def _merge(ya, yb, yc, big2d, gate_b, w_branch, w_out, x2d):
    m = x2d.shape[0]
    tm = MERGE_TM
    yspec = pl.BlockSpec((tm, BRANCH_W), lambda i: (i, 0))
    gspec = lambda n: pl.BlockSpec((tm, D_MODEL), lambda i, n=n: (i, COL_GATES // D_MODEL + n))
    return pl.pallas_call(
        _merge_kernel,
        grid=(m // tm,),
        in_specs=[
            yspec, yspec, yspec, gspec(0), gspec(1), gspec(2),
            pl.BlockSpec((N_BRANCH, D_MODEL), lambda i: (0, 0)),
            pl.BlockSpec((N_BRANCH, BRANCH_W, D_MODEL), lambda i: (0, 0, 0)),
            pl.BlockSpec((D_MODEL, D_MODEL), lambda i: (0, 0)),
            pl.BlockSpec((tm, D_MODEL), lambda i: (i, 0)),
        ],
        out_specs=pl.BlockSpec((tm, D_MODEL), lambda i: (i, 0)),
        out_shape=jax.ShapeDtypeStruct((m, D_MODEL), F32),
        compiler_params=_cparams(1),
        name="merge",
    )(ya, yb, yc, big2d, big2d, big2d, gate_b, w_branch, w_out, x2d)


def _permute_w_in(w_in):
    qkvz_a = w_in[..., 0:2048]
    beta = w_in[..., 2048:2052]
    decay = w_in[..., 2052:2056]
    pool = w_in[..., 2056:3080]
    qkvz_c = w_in[..., 3080:5128]
    idx_q = w_in[..., 5128:5384]
    idx_k = w_in[..., 5384:5448]
    idx_w = w_in[..., 5448:5452]
    gates = w_in[..., 5452:8524]
    pad = jnp.zeros(w_in.shape[:-1] + (SMALL_W - 256 - IDX_DIM - 12,), w_in.dtype)
    return jnp.concatenate([qkvz_a, pool, qkvz_c, gates, idx_q, idx_k, beta, decay, idx_w, pad],
                           axis=-1).astype(BF16)


def _rope_tables(t):
    def base(dim):
        inv_freq = ROPE_THETA ** (-jnp.arange(0, dim, 2, dtype=F32) / dim)
        ang = jnp.arange(t, dtype=F32)[:, None] * inv_freq[None, :]
        return jnp.cos(ang), jnp.sin(ang)

    cos_a, sin_a = base(HEAD_DIM)
    cos = jnp.concatenate([cos_a, cos_a], axis=-1)
    sin = jnp.concatenate([-sin_a, sin_a], axis=-1)
    cos_i, sin_i = base(IDX_DIM)
    zero = jnp.zeros_like(sin_i)
    icos = jnp.tile(jnp.concatenate([cos_i, cos_i], axis=-1), (1, HEADS))
    isin_up = jnp.tile(jnp.concatenate([-sin_i, zero], axis=-1), (1, HEADS))
    isin_dn = jnp.tile(jnp.concatenate([zero, sin_i], axis=-1), (1, HEADS))
    return cos, sin, icos, isin_up, isin_dn


def _lane_row(vals, offset):
    d, n = vals.shape
    return jnp.zeros((d, 1, HEAD_DIM), F32).at[:, 0, offset:offset + n].set(vals.astype(F32))


def kernel(x, norm_g, w_in, gate_b, conv_w, a_log, dt_bias, dn_onorm, pool_w, pool_scale, q_norm, k_norm,
           w_branch, w_out):
    b, t, d = x.shape
    depth = norm_g.shape[0]
    w_p = _permute_w_in(w_in)
    wb16 = w_branch.astype(BF16)
    wo16 = w_out.astype(BF16)
    pw16 = pool_w.astype(BF16)
    alog_rows = _lane_row(a_log, SM_DECAY)
    dtb_rows = _lane_row(dt_bias, SM_DECAY)
    tabs = _rope_tables(t)
    x2d = x.reshape(b * t, d)
    for layer in range(depth):
        big2d, small2d = _in_projection(x2d, norm_g[layer][None, :], w_p[layer])
        big3d = big2d.reshape(b, t, BIG_W)
        small3d = small2d.reshape(b, t, SMALL_W)
        cw = conv_w[layer]
        ya = _deltanet_branch(big3d, small3d, cw[:, 0:BRANCH_W], cw[:, BRANCH_W:2 * BRANCH_W],
                              cw[:, 2 * BRANCH_W:], alog_rows[layer], dtb_rows[layer],
                              dn_onorm[layer][None, :])
        yb = _pool_branch(big3d, pw16[layer], pool_scale[layer][None, :])
        prep = _attn_prep(big3d, small3d, q_norm[layer][None, :], k_norm[layer][None, :], tabs)
        yc = _dsa_branch(big3d, *prep)
        x2d = _merge(ya.reshape(b * t, BRANCH_W), yb.reshape(b * t, BRANCH_W), yc.reshape(b * t, BRANCH_W),
                     big2d, gate_b[layer], wb16[layer], wo16[layer], x2d)
    return x2d.reshape(b, t, d)
```

```python
import functools

import numpy as np
import jax
import jax.numpy as jnp
from jax import lax
from jax.experimental import pallas as pl
from jax.experimental.pallas import tpu as pltpu

F32 = jnp.float32
BF16 = jnp.bfloat16
HIGHEST = lax.Precision.HIGHEST

D_MODEL = 1024
HEADS = 4
HEAD_DIM = 128
BRANCH_W = HEADS * HEAD_DIM
DN_CONV = 4
POOL_WINDOWS = (2, 4, 8, 16)
POOL_GROUP = 128
IDX_DIM = 64
TOPK_MAX = 256
ROPE_THETA = 10000.0
NORM_EPS = 1e-6
N_BRANCH = 3

COL_QA, COL_KA, COL_VA, COL_ZA = 0, 512, 1024, 1536
COL_UB, COL_ZB = 2048, 2560
COL_QC, COL_KC, COL_VC, COL_ZC = 3072, 3584, 4096, 4608
COL_GATES = 5120
BIG_W = 8192
SMALL_W = 384
SM_BETA, SM_DECAY, SM_IDXW = 64, 68, 72

V7X_VMEM_LIMIT = 56 * 1024 * 1024

PROJ_TM = 512
PROJ_TN = 512
MERGE_TM = 512
PREP_TT = 512
POOL_TT = 256
POOL_HALO = 16
DN_CHUNK = 128
DN_HALO = 16
DN_BASE = 8
ATT_QB = 256
ATT_KC = 256
NUM_BISECT = 36
BISECT_UNROLL = 4
MASK_NEG = -1e30
LOG2_E = 1.4426950408889634


def _nt_dot(a, b):
    return lax.dot_general(a, b, (((1,), (1,)), ((), ())), preferred_element_type=F32)


def _dot(a, b, precision=None):
    return jnp.dot(a, b, preferred_element_type=F32, precision=precision)


def _silu(x):
    return x * jax.nn.sigmoid(x)


def _cparams(n_axes):
    return pltpu.CompilerParams(dimension_semantics=("arbitrary",) * n_axes,
                                vmem_limit_bytes=V7X_VMEM_LIMIT)


def _inproj_kernel(x_ref, g_ref, w_ref, big_ref, small_ref):
    x = x_ref[...]
    ms = jnp.mean(x * x, axis=-1, keepdims=True)
    h = ((x * lax.rsqrt(ms + NORM_EPS)) * g_ref[...]).astype(BF16)
    for c in range(0, BIG_W, PROJ_TN):
        big_ref[:, c:c + PROJ_TN] = _dot(h, w_ref[:, c:c + PROJ_TN]).astype(big_ref.dtype)
    small_ref[...] = _dot(h, w_ref[:, BIG_W:])


def _in_projection(x2d, g_row, w_p):
    m = x2d.shape[0]
    return pl.pallas_call(
        _inproj_kernel,
        grid=(m // PROJ_TM,),
        in_specs=[
            pl.BlockSpec((PROJ_TM, D_MODEL), lambda i: (i, 0)),
            pl.BlockSpec((1, D_MODEL), lambda i: (0, 0)),
            pl.BlockSpec((D_MODEL, BIG_W + SMALL_W), lambda i: (0, 0), pipeline_mode=pl.Buffered(1)),
        ],
        out_specs=[
            pl.BlockSpec((PROJ_TM, BIG_W), lambda i: (i, 0)),
            pl.BlockSpec((PROJ_TM, SMALL_W), lambda i: (i, 0)),
        ],
        out_shape=[
            jax.ShapeDtypeStruct((m, BIG_W), BF16),
            jax.ShapeDtypeStruct((m, SMALL_W), F32),
        ],
        compiler_params=_cparams(1),
        name="in_projection",
    )(x2d, g_row, w_p)


def _pool_kernel(u_ref, z_ref, pw_ref, ps_ref, y_ref):
    t_len = u_ref.shape[0]
    for t in range(t_len // POOL_TT):
        r0 = t * POOL_TT
        cur = u_ref[r0:r0 + POOL_TT, :].astype(F32)
        if t == 0:
            prev = jnp.zeros((POOL_HALO, cur.shape[1]), F32)
        else:
            prev = u_ref[r0 - POOL_HALO:r0, :].astype(F32)
        win_rows = jnp.concatenate([prev, cur], axis=0)
        pos = r0 + lax.broadcasted_iota(jnp.int32, (POOL_TT, 1), 0)
        for gi, win in enumerate(POOL_WINDOWS):
            sl = slice(gi * POOL_GROUP, (gi + 1) * POOL_GROUP)
            s = win_rows[:, sl]
            shift = 1
            while shift < win:
                s = s + pltpu.roll(s, shift, axis=0)
                shift *= 2
            count = jnp.minimum(pos + 1, win).astype(F32)
            pooled = s[POOL_HALO:, :] / count - cur[:, sl]
            mixed = _dot(pooled.astype(BF16), pw_ref[gi])
            zg = z_ref[r0:r0 + POOL_TT, sl].astype(F32)
            y_ref[r0:r0 + POOL_TT, sl] = (mixed * ps_ref[:, sl] * _silu(zg)).astype(y_ref.dtype)


def _pool_branch(big3d, pool_w, pool_scale_row):
    b, t, _ = big3d.shape
    width = POOL_GROUP * len(POOL_WINDOWS)
    return pl.pallas_call(
        _pool_kernel,
        grid=(b,),
        in_specs=[
            pl.BlockSpec((None, t, width), lambda i: (i, 0, COL_UB // width)),
            pl.BlockSpec((None, t, width), lambda i: (i, 0, COL_ZB // width)),
            pl.BlockSpec((len(POOL_WINDOWS), POOL_GROUP, POOL_GROUP), lambda i: (0, 0, 0)),
            pl.BlockSpec((1, width), lambda i: (0, 0)),
        ],
        out_specs=pl.BlockSpec((None, t, width), lambda i: (i, 0, 0)),
        out_shape=jax.ShapeDtypeStruct((b, t, width), BF16),
        compiler_params=_cparams(1),
        name="pool_branch",
    )(big3d, big3d, pool_w, pool_scale_row)


def _softplus(x):
    return jnp.maximum(x, 0.0) + jnp.log1p(jnp.exp(-jnp.abs(x)))


def _split_bf16(a):
    hi = a.astype(BF16)
    lo = (a - hi.astype(F32)).astype(BF16)
    return hi, lo


def _dot_split(a, b):
    a_hi, a_lo = a
    b_hi, b_lo = b
    return _dot(jnp.concatenate([a_hi, a_lo, a_hi], axis=1), jnp.concatenate([b_hi, b_hi, b_lo], axis=0))


def _dn_kernel(q_ref, k_ref, v_ref, z_ref, cq_ref, ck_ref, cv_ref, sm_ref, alog_ref, dtb_ref, on_ref,
               y_ref, u_s, w_s, qg_s, a_s, kdt_s, el_s, st_s):
    t_len = q_ref.shape[0]
    n_chunks = t_len // DN_CHUNK
    c = DN_CHUNK
    hd = HEAD_DIM
    row = lax.broadcasted_iota(jnp.int32, (c, c), 0)
    col = lax.broadcasted_iota(jnp.int32, (c, c), 1)
    tril = row >= col
    strict = row > col
    tril16 = tril.astype(BF16)
    eye_f = (row == col).astype(F32)
    base_blk = (row // DN_BASE) == (col // DN_BASE)
    pair_blks = []
    size = DN_BASE
    while size < c:
        pair_blks.append(((row // (2 * size)) == (col // (2 * size))) & ((row // size) != (col // size)))
        size *= 2

    def conv_silu(ref, cw_ref, ci):
        start = pl.multiple_of(ci * c, c)
        cur = ref[pl.ds(start, c), :].astype(F32)
        pstart = pl.multiple_of(jnp.maximum(start - DN_HALO, 0), DN_HALO)
        prev = ref[pl.ds(pstart, DN_HALO), :].astype(F32)
        prev = jnp.where(ci > 0, prev, 0.0)
        rows = jnp.concatenate([prev, cur], axis=0)
        cw = cw_ref[...]
        acc = rows * cw[DN_CONV - 1:DN_CONV, :]
        for s in range(1, DN_CONV):
            acc = acc + pltpu.roll(rows, s, axis=0) * cw[DN_CONV - 1 - s:DN_CONV - s, :]
        return _silu(acc[DN_HALO:, :])

    def prepare(ci, carry):
        start = pl.multiple_of(ci * c, c)
        xq_all = conv_silu(q_ref, cq_ref, ci)
        xk_all = conv_silu(k_ref, ck_ref, ci)
        xv_all = conv_silu(v_ref, cv_ref, ci)
        sm = sm_ref[pl.ds(start, c), :]
        beta_all = jax.nn.sigmoid(sm)
        g_all = -jnp.exp(alog_ref[...]) * _softplus(sm + dtb_ref[...])
        g_hi = g_all.astype(BF16)
        g_r = g_all - g_hi.astype(F32)
        g_mid = g_r.astype(BF16)
        g_lo = (g_r - g_mid.astype(F32)).astype(BF16)
        gc_all = _dot(tril16, g_hi) + (_dot(tril16, g_mid) + _dot(tril16, g_lo))
        hs = range(HEADS)
        sls = [slice(h * hd, (h + 1) * hd) for h in hs]
        qn = [xq_all[:, sl] * (lax.rsqrt(jnp.sum(xq_all[:, sl] * xq_all[:, sl], axis=-1, keepdims=True) + NORM_EPS)
                               * (hd ** -0.5)) for sl in sls]
        kn = [xk_all[:, sl] * lax.rsqrt(jnp.sum(xk_all[:, sl] * xk_all[:, sl], axis=-1, keepdims=True) + NORM_EPS)
              for sl in sls]
        beta_b = [jnp.broadcast_to(beta_all[:, SM_BETA + h:SM_BETA + h + 1], (c, hd)) for h in hs]
        gc = [jnp.broadcast_to(gc_all[:, SM_DECAY + h:SM_DECAY + h + 1], (c, hd)) for h in hs]
        decay = [jnp.exp(jnp.where(tril, gc[h] - gc[h].T, -jnp.inf)) for h in hs]
        kb = [kn[h] * beta_b[h] for h in hs]
        kn16 = [kn[h].astype(BF16) for h in hs]
        kq = [_nt_dot(jnp.concatenate([kb[h].astype(BF16), qn[h].astype(BF16)], axis=0), kn16[h]) for h in hs]
        lmat = [jnp.where(strict, kq[h][:c, :] * decay[h], 0.0) for h in hs]
        l_hi = [lmat[h].astype(BF16) for h in hs]
        l_lo = [(lmat[h] - l_hi[h].astype(F32)).astype(BF16) for h in hs]
        zero16 = jnp.zeros((c, c), BF16)
        d_parts = [(jnp.where(base_blk, l_hi[h], zero16), jnp.where(base_blk, l_lo[h], zero16)) for h in hs]
        tmat = [eye_f - jnp.where(base_blk, lmat[h], 0.0) for h in hs]
        power = [_dot_split(d_parts[h], d_parts[h]) for h in hs]
        span = 2
        while span < DN_BASE:
            parts = [_split_bf16(power[h]) for h in hs]
            t_parts = [_split_bf16(tmat[h]) for h in hs]
            span *= 2
            if span < DN_BASE:
                both = [_dot_split((jnp.concatenate([t_parts[h][0], parts[h][0]], axis=0),
                                    jnp.concatenate([t_parts[h][1], parts[h][1]], axis=0)), parts[h]) for h in hs]
                tmat = [tmat[h] + both[h][:c, :] for h in hs]
                power = [both[h][c:, :] for h in hs]
            else:
                tmat = [tmat[h] + _dot_split(t_parts[h], parts[h]) for h in hs]
        for pair_blk in pair_blks:
            t_parts = [_split_bf16(tmat[h]) for h in hs]
            off = [(jnp.where(pair_blk, l_hi[h], zero16), jnp.where(pair_blk, l_lo[h], zero16)) for h in hs]
            cx = [_dot_split(off[h], t_parts[h]) for h in hs]
            tmat = [tmat[h] - _dot_split(t_parts[h], _split_bf16(cx[h])) for h in hs]
        for h in hs:
            sl = sls[h]
            t16 = tmat[h].astype(BF16)
            egc = jnp.exp(gc[h])
            uw = _dot(t16, jnp.concatenate([(xv_all[:, sl] * beta_b[h]).astype(BF16), (kb[h] * egc).astype(BF16)],
                                           axis=1))
            u_s[pl.ds(start, c), sl] = uw[:, :hd]
            w_s[pl.ds(start, c), sl] = uw[:, hd:].astype(BF16)
            qg_s[pl.ds(start, c), sl] = (qn[h] * egc).astype(BF16)
            a_s[pl.ds(start, c), sl] = jnp.where(tril, kq[h][c:, :] * decay[h], 0.0).astype(BF16)
            g_last = gc[h][c - 1:c, :]
            kd = kn[h] * jnp.exp(g_last - gc[h])
            kdt_s[pl.ds(start, c), sl] = kd.T.astype(BF16)
            el_s[pl.ds(pl.multiple_of(ci * 8, 8), 8), sl] = jnp.broadcast_to(jnp.exp(g_last), (8, hd))
        return carry

    lax.fori_loop(0, n_chunks, prepare, 0)

    st_s[...] = jnp.zeros(st_s.shape, F32)

    def scan(ci, carry):
        start = pl.multiple_of(ci * c, c)
        hs = range(HEADS)
        sls = [slice(h * hd, (h + 1) * hd) for h in hs]
        state = [st_s[h] for h in hs]
        s16 = [state[h].astype(BF16) for h in hs]
        v_new = [u_s[pl.ds(start, c), sls[h]] - _dot(w_s[pl.ds(start, c), sls[h]], s16[h]) for h in hs]
        v16 = [v_new[h].astype(BF16) for h in hs]
        for h in hs:
            e_last = el_s[pl.ds(pl.multiple_of(ci * 8, 8), 8), sls[h]][0:1, :]
            st_s[h] = state[h] * e_last + _dot(kdt_s[pl.ds(start, c), sls[h]], v16[h])
        for h in hs:
            sl = sls[h]
            o = _dot(qg_s[pl.ds(start, c), sl], s16[h]) + _dot(a_s[pl.ds(start, c), sl], v16[h])
            on = o * lax.rsqrt(jnp.mean(o * o, axis=-1, keepdims=True) + NORM_EPS) * on_ref[...]
            zg = z_ref[pl.ds(start, c), sl].astype(F32)
            y_ref[pl.ds(start, c), sl] = (on * _silu(zg)).astype(y_ref.dtype)
        return carry

    lax.fori_loop(0, n_chunks, scan, 0)


def _deltanet_branch(big3d, small3d, conv_q, conv_k, conv_v, alog_row, dtb_row, onorm_row):
    b, t, _ = big3d.shape
    hd = HEAD_DIM
    w = BRANCH_W
    seq_spec = lambda col: pl.BlockSpec((None, t, w), lambda i, col=col: (i, 0, col // w))
    conv_spec = pl.BlockSpec((DN_CONV, w), lambda i: (0, 0))
    row_spec = pl.BlockSpec((1, hd), lambda i: (0, 0))
    return pl.pallas_call(
        _dn_kernel,
        grid=(b,),
        in_specs=[
            seq_spec(COL_QA), seq_spec(COL_KA), seq_spec(COL_VA), seq_spec(COL_ZA),
            conv_spec, conv_spec, conv_spec,
            pl.BlockSpec((None, t, hd), lambda i: (i, 0, SMALL_W // hd - 1)),
            row_spec, row_spec, row_spec,
        ],
        out_specs=pl.BlockSpec((None, t, w), lambda i: (i, 0, 0)),
        out_shape=jax.ShapeDtypeStruct((b, t, w), BF16),
        scratch_shapes=[
            pltpu.VMEM((t, w), F32),
            pltpu.VMEM((t, w), BF16),
            pltpu.VMEM((t, w), BF16),
            pltpu.VMEM((t, w), BF16),
            pltpu.VMEM((t, w), BF16),
            pltpu.VMEM((8 * t // DN_CHUNK, w), F32),
            pltpu.VMEM((HEADS, hd, hd), F32),
        ],
        compiler_params=_cparams(1),
        name="deltanet_branch",
    )(big3d, big3d, big3d, big3d, conv_q, conv_k, conv_v, small3d, alog_row, dtb_row, onorm_row)


def _attn_prep_kernel(q_ref, k_ref, v_ref, sm_ref, qn_ref, kn_ref, cos_ref, sin_ref,
                      icos_ref, isin_up_ref, isin_dn_ref,
                      qo_ref, ko_ref, vt_ref, qio_ref, kia_ref, kib_ref, wt_ref):
    cos = cos_ref[...]
    sin = sin_ref[...]

    def norm_rope(ref, gain):
        for h in range(HEADS):
            sl = slice(h * HEAD_DIM, (h + 1) * HEAD_DIM)
            x = ref[:, sl].astype(F32)
            xn = (x * lax.rsqrt(jnp.mean(x * x, axis=-1, keepdims=True) + NORM_EPS)) * gain
            yield sl, xn * cos + pltpu.roll(xn, HEAD_DIM // 2, axis=1) * sin

    for sl, val in norm_rope(q_ref, qn_ref[...]):
        qo_ref[:, sl] = (val * (HEAD_DIM ** -0.5 * LOG2_E)).astype(qo_ref.dtype)
    for sl, val in norm_rope(k_ref, kn_ref[...]):
        ko_ref[:, sl] = val.astype(ko_ref.dtype)
    vt_ref[...] = v_ref[...].astype(F32).T.astype(vt_ref.dtype)

    sm = sm_ref[...]
    iq_w = HEADS * IDX_DIM
    half = IDX_DIM // 2
    iq = sm[:, :iq_w]
    iq_r = (iq * icos_ref[...] + pltpu.roll(iq, iq_w - half, axis=1) * isin_up_ref[...]
            + pltpu.roll(iq, half, axis=1) * isin_dn_ref[...])
    qio_ref[...] = iq_r.astype(qio_ref.dtype)
    last = sm[:, iq_w:]
    lw = last.shape[1]
    ik_r = (last * icos_ref[:, :lw] + pltpu.roll(last, lw - half, axis=1) * isin_up_ref[:, :lw]
            + pltpu.roll(last, half, axis=1) * isin_dn_ref[:, :lw])
    lane = lax.broadcasted_iota(jnp.int32, ik_r.shape, 1)
    ik_r = jnp.where(lane < IDX_DIM, ik_r, 0.0)
    kia_ref[...] = ik_r.astype(kia_ref.dtype)
    kib_ref[...] = pltpu.roll(ik_r, IDX_DIM, axis=1).astype(kib_ref.dtype)
    wt = last.T
    wt_ref[...] = wt[SM_IDXW:SM_IDXW + 8, :] * (HEADS ** -0.5 * IDX_DIM ** -0.5)


def _attn_prep(big3d, small3d, qn_row, kn_row, tabs):
    b, t, _ = big3d.shape
    w = BRANCH_W
    tt = PREP_TT
    iq_w = HEADS * IDX_DIM
    seq = lambda col: pl.BlockSpec((None, tt, w), lambda i, j, col=col: (i, j, col // w))
    tab = lambda width: pl.BlockSpec((tt, width), lambda i, j: (j, 0))
    row = pl.BlockSpec((1, HEAD_DIM), lambda i, j: (0, 0))
    return pl.pallas_call(
        _attn_prep_kernel,
        grid=(b, t // tt),
        in_specs=[
            seq(COL_QC), seq(COL_KC), seq(COL_VC),
            pl.BlockSpec((None, tt, SMALL_W), lambda i, j: (i, j, 0)),
            row, row, tab(HEAD_DIM), tab(HEAD_DIM), tab(iq_w), tab(iq_w), tab(iq_w),
        ],
        out_specs=[
            pl.BlockSpec((None, tt, w), lambda i, j: (i, j, 0)),
            pl.BlockSpec((None, tt, w), lambda i, j: (i, j, 0)),
            pl.BlockSpec((None, w, tt), lambda i, j: (i, 0, j)),
            pl.BlockSpec((None, tt, iq_w), lambda i, j: (i, j, 0)),
            pl.BlockSpec((None, tt, 2 * IDX_DIM), lambda i, j: (i, j, 0)),
            pl.BlockSpec((None, tt, 2 * IDX_DIM), lambda i, j: (i, j, 0)),
            pl.BlockSpec((None, 8, tt), lambda i, j: (i, 0, j)),
        ],
        out_shape=[
            jax.ShapeDtypeStruct((b, t, w), BF16),
            jax.ShapeDtypeStruct((b, t, w), BF16),
            jax.ShapeDtypeStruct((b, w, t), BF16),
            jax.ShapeDtypeStruct((b, t, iq_w), BF16),
            jax.ShapeDtypeStruct((b, t, 2 * IDX_DIM), BF16),
            jax.ShapeDtypeStruct((b, t, 2 * IDX_DIM), BF16),
            jax.ShapeDtypeStruct((b, 8, t), F32),
        ],
        compiler_params=_cparams(2),
        name="attn_prep",
    )(big3d, big3d, big3d, small3d, qn_row, kn_row, *tabs)


def _dsa_kernel(q_ref, k_ref, vt_ref, qi_ref, kia_ref, kib_ref, wt_ref, z_ref, y_ref, s_ref, acc_ref, *, topk):
    qb, kc = ATT_QB, ATT_KC
    blk_i = pl.program_id(1)
    n_kc = blk_i + 1
    q_pos = blk_i * qb + lax.broadcasted_iota(jnp.int32, (1, qb), 1)
    qi = qi_ref[...]
    wt = wt_ref[...]
    inf = jnp.inf

    def col_sum(x):
        return x.reshape(kc // 8, 8, qb).sum(axis=0)

    def total(x):
        return jnp.sum(x, axis=0, keepdims=True)

    def score_chunk(ci, carry, diagonal):
        vmax, vmin, min_pos, n_pos, n_nonneg = carry
        off = pl.multiple_of(ci * kc, kc)
        ka = kia_ref[pl.ds(off, kc), :]
        kb = kib_ref[pl.ds(off, kc), :]
        s = jnp.zeros((kc, qb), F32)
        for h in range(HEADS):
            kk = ka if h % 2 == 0 else kb
            qq = qi[:, (h // 2) * 2 * IDX_DIM:(h // 2 + 1) * 2 * IDX_DIM]
            s = s + jnp.maximum(_nt_dot(kk, qq), 0.0) * wt[h:h + 1, :]
        s = jnp.where(s == 0.0, 0.0, s)
        if diagonal:
            key_pos = off + lax.broadcasted_iota(jnp.int32, (kc, 1), 0)
            causal = key_pos <= q_pos
            sc = jnp.where(causal, s, -inf)
            s_hi = jnp.where(causal, s, inf)
        else:
            sc = s_hi = s
        s_ref[pl.ds(off, kc), :] = sc
        vmax = jnp.maximum(vmax, jnp.max(sc, axis=0, keepdims=True))
        vmin = jnp.minimum(vmin, jnp.min(s_hi, axis=0, keepdims=True))
        min_pos = jnp.minimum(min_pos, jnp.min(jnp.where(sc > 0.0, sc, inf), axis=0, keepdims=True))
        n_pos = n_pos + col_sum(jnp.where(sc > 0.0, 1.0, 0.0))
        n_nonneg = n_nonneg + col_sum(jnp.where(sc >= 0.0, 1.0, 0.0))
        return vmax, vmin, min_pos, n_pos, n_nonneg

    row_inf = jnp.full((1, qb), inf, F32)
    zeros8 = jnp.zeros((8, qb), F32)
    stats = lax.fori_loop(0, blk_i, functools.partial(score_chunk, diagonal=False),
                          (-row_inf, row_inf, row_inf, zeros8, zeros8))
    vmax, vmin, min_pos, n_pos, n_nonneg = score_chunk(blk_i, stats, diagonal=True)
    n_pos = total(n_pos)
    n_nonneg = total(n_nonneg)

    def count_ge(thr):
        def body(ci, acc):
            off = pl.multiple_of(ci * kc, kc)
            blk = s_ref[pl.ds(off, kc), :]
            return acc + col_sum(jnp.where(blk >= thr, 1.0, 0.0))
        return total(lax.fori_loop(0, n_kc, body, zeros8))

    k_sel = jnp.minimum(q_pos + 1, topk).astype(F32)
    n_causal = (q_pos + 1).astype(F32)
    at_zero = (n_pos < k_sel) & (k_sel <= n_nonneg)
    above = k_sel <= n_pos
    lo = jnp.where(at_zero, 0.0, jnp.where(above, min_pos, vmin))
    hi = jnp.where(at_zero, min_pos, jnp.where(above, vmax + (jnp.abs(vmax) + 1.0), 0.0))
    c_lo = jnp.where(at_zero, n_nonneg, jnp.where(above, n_pos, n_causal))
    c_hi = jnp.where(at_zero, n_pos, jnp.where(above, 0.0, n_nonneg))
    done = jnp.where(at_zero | (c_lo == k_sel), 1.0, 0.0)

    def n_open(d):
        return jnp.sum(1.0 - d)

    def bisect_cond(carry):
        it, n_left = carry[0], carry[1]
        return (it < NUM_BISECT) & (n_left > 0.0)

    def bisect_body(carry):
        it, _, lo, hi, c_lo, c_hi, done = carry
        for _ in range(BISECT_UNROLL):
            mid = 0.5 * lo + 0.5 * hi
            cnt = count_ge(mid)
            live = done < 0.5
            up = (cnt >= k_sel) & live
            dn = (cnt < k_sel) & live
            lo = jnp.where(up, mid, lo)
            c_lo = jnp.where(up, cnt, c_lo)
            hi = jnp.where(dn, mid, hi)
            c_hi = jnp.where(dn, cnt, c_hi)
            done = jnp.where(c_lo == k_sel, 1.0, done)
        return it + BISECT_UNROLL, n_open(done), lo, hi, c_lo, c_hi, done

    _, _, lo, hi, c_lo, c_hi, done = lax.while_loop(
        bisect_cond, bisect_body, (jnp.int32(0), n_open(done), lo, hi, c_lo, c_hi, done))
    need = k_sel - c_hi
    n_tied = jnp.sum(jnp.where(c_lo - c_hi > need, 1.0, 0.0))

    def tie_mask():
        tri = (lax.broadcasted_iota(jnp.int32, (kc, kc), 1)
               < lax.broadcasted_iota(jnp.int32, (kc, kc), 0)).astype(BF16)

        def mask_body(ci, seen):
            off = pl.multiple_of(ci * kc, kc)
            blk = s_ref[pl.ds(off, kc), :]
            tie = jnp.where((blk >= lo) & (blk < hi), 1.0, 0.0)
            rank = _dot(tri, tie.astype(BF16)) + seen
            sel = (blk >= hi) | ((tie > 0.5) & (rank < need))
            s_ref[pl.ds(off, kc), :] = jnp.where(sel, 1.0, 0.0)
            return seen + total(col_sum(tie))

        lax.fori_loop(0, n_kc, mask_body, jnp.zeros((1, qb), F32))

    pl.when(n_tied > 0.0)(tie_mask)
    sel_thr = jnp.where(n_tied > 0.0, 0.5, lo)

    hs = range(HEADS)
    sls = [slice(h * HEAD_DIM, (h + 1) * HEAD_DIM) for h in hs]
    qh = [q_ref[:, sl] for sl in sls]
    acc_ref[...] = jnp.zeros(acc_ref.shape, F32)

    def att_body(ci, carry):
        ms, ls = carry
        off = pl.multiple_of(ci * kc, kc)
        sel = s_ref[pl.ds(off, kc), :] >= sel_thr
        lm = [jnp.where(sel, _nt_dot(k_ref[pl.ds(off, kc), sls[h]], qh[h]), MASK_NEG) for h in hs]
        m_new = [jnp.maximum(ms[h], jnp.max(lm[h], axis=0, keepdims=True)) for h in hs]
        p = [jnp.exp2(lm[h] - m_new[h]) for h in hs]
        alpha = [jnp.exp2(ms[h] - m_new[h]) for h in hs]
        l_new = [alpha[h] * ls[h] + jnp.sum(p[h], axis=0, keepdims=True) for h in hs]
        pv = [_dot(vt_ref[sls[h], pl.ds(off, kc)], p[h].astype(BF16)) for h in hs]
        for h in hs:
            acc_ref[sls[h], :] = alpha[h] * acc_ref[sls[h], :] + pv[h]
        return tuple(m_new), tuple(l_new)

    row_neg = jnp.full((1, qb), MASK_NEG, F32)
    row_zero = jnp.zeros((1, qb), F32)
    _, ls = lax.fori_loop(0, n_kc, att_body, ((row_neg,) * HEADS, (row_zero,) * HEADS))
    for h in hs:
        o = (acc_ref[sls[h], :] / ls[h]).T
        y_ref[:, sls[h]] = (o * _silu(z_ref[:, sls[h]].astype(F32))).astype(y_ref.dtype)


def _dsa_branch(big3d, q_r, k_r, v_t, qi_r, ki_a, ki_b, w_t):
    b, t, _ = big3d.shape
    w = BRANCH_W
    qb = ATT_QB
    topk = min(TOPK_MAX, t // 4)
    return pl.pallas_call(
        functools.partial(_dsa_kernel, topk=topk),
        grid=(b, t // qb),
        in_specs=[
            pl.BlockSpec((None, qb, w), lambda i, j: (i, j, 0)),
            pl.BlockSpec((None, t, w), lambda i, j: (i, 0, 0)),
            pl.BlockSpec((None, w, t), lambda i, j: (i, 0, 0)),
            pl.BlockSpec((None, qb, HEADS * IDX_DIM), lambda i, j: (i, j, 0)),
            pl.BlockSpec((None, t, 2 * IDX_DIM), lambda i, j: (i, 0, 0)),
            pl.BlockSpec((None, t, 2 * IDX_DIM), lambda i, j: (i, 0, 0)),
            pl.BlockSpec((None, 8, qb), lambda i, j: (i, 0, j)),
            pl.BlockSpec((None, qb, w), lambda i, j: (i, j, COL_ZC // w)),
        ],
        out_specs=pl.BlockSpec((None, qb, w), lambda i, j: (i, j, 0)),
        out_shape=jax.ShapeDtypeStruct((b, t, w), BF16),
        scratch_shapes=[pltpu.VMEM((t, qb), F32),
                        pltpu.VMEM((w, qb), F32)],
        compiler_params=_cparams(2),
        name="sparse_attention",
    )(q_r, k_r, v_t, qi_r, ki_a, ki_b, w_t, big3d)


def _merge_kernel(ya_ref, yb_ref, yc_ref, ga_ref, gb_ref, gc_ref, bias_ref, wb_ref, wo_ref, x_ref, o_ref):
    merged = None
    for n, (y_ref, g_ref) in enumerate(((ya_ref, ga_ref), (yb_ref, gb_ref), (yc_ref, gc_ref))):
        proj = _dot(y_ref[...], wb_ref[n])
        gate = jax.nn.sigmoid(g_ref[...].astype(F32) + bias_ref[n:n + 1, :])
        term = gate * proj
        merged = term if merged is None else merged + term
    o_ref[...] = x_ref[...] + _dot(merged.astype(BF16), wo_ref[...])


def _merge(ya, yb, yc, big2d, gate_b, w_branch, w_out, x2d):
    m = x2d.shape[0]
    tm = MERGE_TM
    yspec = pl.BlockSpec((tm, BRANCH_W), lambda i: (i, 0))
    gspec = lambda n: pl.BlockSpec((tm, D_MODEL), lambda i, n=n: (i, COL_GATES // D_MODEL + n))
    return pl.pallas_call(
        _merge_kernel,
        grid=(m // tm,),
        in_specs=[
            yspec, yspec, yspec, gspec(0), gspec(1), gspec(2),
            pl.BlockSpec((N_BRANCH, D_MODEL), lambda i: (0, 0)),
            pl.BlockSpec((N_BRANCH, BRANCH_W, D_MODEL), lambda i: (0, 0, 0)),
            pl.BlockSpec((D_MODEL, D_MODEL), lambda i: (0, 0)),
            pl.BlockSpec((tm, D_MODEL), lambda i: (i, 0)),
        ],
        out_specs=pl.BlockSpec((tm, D_MODEL), lambda i: (i, 0)),
        out_shape=jax.ShapeDtypeStruct((m, D_MODEL), F32),
        compiler_params=_cparams(1),
        name="merge",
    )(ya, yb, yc, big2d, big2d, big2d, gate_b, w_branch, w_out, x2d)


def _permute_w_in(w_in):
    w_in = w_in.astype(BF16)
    qkvz_a = w_in[..., 0:2048]
    beta = w_in[..., 2048:2052]
    decay = w_in[..., 2052:2056]
    pool = w_in[..., 2056:3080]
    qkvz_c = w_in[..., 3080:5128]
    idx_q = w_in[..., 5128:5384]
    idx_k = w_in[..., 5384:5448]
    idx_w = w_in[..., 5448:5452]
    gates = w_in[..., 5452:8524]
    pad = jnp.zeros(w_in.shape[:-1] + (SMALL_W - 256 - IDX_DIM - 12,), w_in.dtype)
    return jnp.concatenate([qkvz_a, pool, qkvz_c, gates, idx_q, idx_k, beta, decay, idx_w, pad], axis=-1)


def _rope_tables(t):
    def base(dim):
        inv_freq = ROPE_THETA ** (-jnp.arange(0, dim, 2, dtype=F32) / dim)
        ang = jnp.arange(t, dtype=F32)[:, None] * inv_freq[None, :]
        return jnp.cos(ang), jnp.sin(ang)

    cos_a, sin_a = base(HEAD_DIM)
    cos = jnp.concatenate([cos_a, cos_a], axis=-1)
    sin = jnp.concatenate([-sin_a, sin_a], axis=-1)
    cos_i, sin_i = base(IDX_DIM)
    zero = jnp.zeros_like(sin_i)
    icos = jnp.tile(jnp.concatenate([cos_i, cos_i], axis=-1), (1, HEADS))
    isin_up = jnp.tile(jnp.concatenate([-sin_i, zero], axis=-1), (1, HEADS))
    isin_dn = jnp.tile(jnp.concatenate([zero, sin_i], axis=-1), (1, HEADS))
    return cos, sin, icos, isin_up, isin_dn


def _lane_row(vals, offset):
    d, n = vals.shape
    return jnp.zeros((d, 1, HEAD_DIM), F32).at[:, 0, offset:offset + n].set(vals.astype(F32))


def kernel(x, norm_g, w_in, gate_b, conv_w, a_log, dt_bias, dn_onorm, pool_w, pool_scale, q_norm, k_norm,
           w_branch, w_out):
    b, t, d = x.shape
    depth = norm_g.shape[0]
    w_p = _permute_w_in(w_in)
    wb16 = w_branch.astype(BF16)
    wo16 = w_out.astype(BF16)
    pw16 = pool_w.astype(BF16)
    alog_rows = _lane_row(a_log, SM_DECAY)
    dtb_rows = _lane_row(dt_bias, SM_DECAY)
    tabs = _rope_tables(t)
    x2d = x.reshape(b * t, d)
    for layer in range(depth):
        big2d, small2d = _in_projection(x2d, norm_g[layer][None, :], w_p[layer])
        big3d = big2d.reshape(b, t, BIG_W)
        small3d = small2d.reshape(b, t, SMALL_W)
        cw = conv_w[layer]
        ya = _deltanet_branch(big3d, small3d, cw[:, 0:BRANCH_W], cw[:, BRANCH_W:2 * BRANCH_W],
                              cw[:, 2 * BRANCH_W:], alog_rows[layer], dtb_rows[layer],
                              dn_onorm[layer][None, :])
        yb = _pool_branch(big3d, pw16[layer], pool_scale[layer][None, :])
        prep = _attn_prep(big3d, small3d, q_norm[layer][None, :], k_norm[layer][None, :], tabs)
        yc = _dsa_branch(big3d, *prep)
        x2d = _merge(ya.reshape(b * t, BRANCH_W), yb.reshape(b * t, BRANCH_W), yc.reshape(b * t, BRANCH_W),
                     big2d, gate_b[layer], wb16[layer], wo16[layer], x2d)
    return x2d.reshape(b, t, d)
```

```python
import functools

import numpy as np
import jax
import jax.numpy as jnp
from jax import lax
from jax.experimental import pallas as pl
from jax.experimental.pallas import tpu as pltpu

F32 = jnp.float32
BF16 = jnp.bfloat16
HIGHEST = lax.Precision.HIGHEST

D_MODEL = 1024
HEADS = 4
HEAD_DIM = 128
BRANCH_W = HEADS * HEAD_DIM
DN_CONV = 4
POOL_WINDOWS = (2, 4, 8, 16)
POOL_GROUP = 128
IDX_DIM = 64
TOPK_MAX = 256
ROPE_THETA = 10000.0
NORM_EPS = 1e-6
N_BRANCH = 3

COL_QA, COL_KA, COL_VA, COL_ZA = 0, 512, 1024, 1536
COL_UB, COL_ZB = 2048, 2560
COL_QC, COL_KC, COL_VC, COL_ZC = 3072, 3584, 4096, 4608
COL_GATES = 5120
BIG_W = 8192
SMALL_W = 384
SM_BETA, SM_DECAY, SM_IDXW = 64, 68, 72

V7X_VMEM_LIMIT = 56 * 1024 * 1024

PROJ_TM = 512
PROJ_TN = 512
MERGE_TM = 1024
PREP_TT = 512
POOL_TT = 256
POOL_HALO = 16
DN_CHUNK = 128
DN_HALO = 16
DN_BASE = 8
ATT_QB = 256
ATT_KC = 256
NUM_BISECT = 36
BISECT_UNROLL = 4
COUNT_LANES = 4
MASK_NEG = -1e30
LOG2_E = 1.4426950408889634


def _nt_dot(a, b):
    return lax.dot_general(a, b, (((1,), (1,)), ((), ())), preferred_element_type=F32)


def _dot(a, b, precision=None):
    return jnp.dot(a, b, preferred_element_type=F32, precision=precision)


def _silu(x):
    return x * jax.nn.sigmoid(x)


def _cparams(n_axes):
    return pltpu.CompilerParams(dimension_semantics=("arbitrary",) * n_axes,
                                vmem_limit_bytes=V7X_VMEM_LIMIT)


def _inproj_kernel(x_ref, g_ref, w_ref, big_ref, small_ref):
    x = x_ref[...]
    ms = jnp.mean(x * x, axis=-1, keepdims=True)
    h = ((x * lax.rsqrt(ms + NORM_EPS)) * g_ref[...]).astype(BF16)
    for c in range(0, BIG_W, PROJ_TN):
        big_ref[:, c:c + PROJ_TN] = _dot(h, w_ref[:, c:c + PROJ_TN]).astype(big_ref.dtype)
    small_ref[...] = _dot(h, w_ref[:, BIG_W:])


def _in_projection(x2d, g_row, w_p, layer):
    m = x2d.shape[0]
    return pl.pallas_call(
        _inproj_kernel,
        grid=(m // PROJ_TM,),
        in_specs=[
            pl.BlockSpec((PROJ_TM, D_MODEL), lambda i: (i, 0)),
            pl.BlockSpec((1, D_MODEL), lambda i: (0, 0)),
            pl.BlockSpec((None, D_MODEL, BIG_W + SMALL_W), lambda i: (layer, 0, 0), pipeline_mode=pl.Buffered(1)),
        ],
        out_specs=[
            pl.BlockSpec((PROJ_TM, BIG_W), lambda i: (i, 0)),
            pl.BlockSpec((PROJ_TM, SMALL_W), lambda i: (i, 0)),
        ],
        out_shape=[
            jax.ShapeDtypeStruct((m, BIG_W), BF16),
            jax.ShapeDtypeStruct((m, SMALL_W), F32),
        ],
        compiler_params=_cparams(1),
        name="in_projection",
    )(x2d, g_row, w_p)


def _pool_kernel(u_ref, z_ref, pw_ref, ps_ref, y_ref):
    t_len = u_ref.shape[0]
    for t in range(t_len // POOL_TT):
        r0 = t * POOL_TT
        cur = u_ref[r0:r0 + POOL_TT, :].astype(F32)
        if t == 0:
            prev = jnp.zeros((POOL_HALO, cur.shape[1]), F32)
        else:
            prev = u_ref[r0 - POOL_HALO:r0, :].astype(F32)
        win_rows = jnp.concatenate([prev, cur], axis=0)
        pos = r0 + lax.broadcasted_iota(jnp.int32, (POOL_TT, 1), 0)
        for gi, win in enumerate(POOL_WINDOWS):
            sl = slice(gi * POOL_GROUP, (gi + 1) * POOL_GROUP)
            s = win_rows[:, sl]
            shift = 1
            while shift < win:
                s = s + pltpu.roll(s, shift, axis=0)
                shift *= 2
            count = jnp.minimum(pos + 1, win).astype(F32)
            pooled = s[POOL_HALO:, :] / count - cur[:, sl]
            mixed = _dot(pooled.astype(BF16), pw_ref[gi])
            zg = z_ref[r0:r0 + POOL_TT, sl].astype(F32)
            y_ref[r0:r0 + POOL_TT, sl] = (mixed * ps_ref[:, sl] * _silu(zg)).astype(y_ref.dtype)


def _pool_branch(big3d, pool_w, pool_scale_row):
    b, t, _ = big3d.shape
    width = POOL_GROUP * len(POOL_WINDOWS)
    return pl.pallas_call(
        _pool_kernel,
        grid=(b,),
        in_specs=[
            pl.BlockSpec((None, t, width), lambda i: (i, 0, COL_UB // width)),
            pl.BlockSpec((None, t, width), lambda i: (i, 0, COL_ZB // width)),
            pl.BlockSpec((len(POOL_WINDOWS), POOL_GROUP, POOL_GROUP), lambda i: (0, 0, 0)),
            pl.BlockSpec((1, width), lambda i: (0, 0)),
        ],
        out_specs=pl.BlockSpec((None, t, width), lambda i: (i, 0, 0)),
        out_shape=jax.ShapeDtypeStruct((b, t, width), BF16),
        compiler_params=_cparams(1),
        name="pool_branch",
    )(big3d, big3d, pool_w, pool_scale_row)


def _softplus(x):
    return jnp.maximum(x, 0.0) + jnp.log1p(jnp.exp(-jnp.abs(x)))


def _split_bf16(a):
    hi = a.astype(BF16)
    lo = (a - hi.astype(F32)).astype(BF16)
    return hi, lo


def _dot_split(a, b):
    a_hi, a_lo = a
    b_hi, b_lo = b
    return _dot(jnp.concatenate([a_hi, a_lo, a_hi], axis=1), jnp.concatenate([b_hi, b_hi, b_lo], axis=0))


def _dn_kernel(q_ref, k_ref, v_ref, z_ref, cq_ref, ck_ref, cv_ref, sm_ref, alog_ref, dtb_ref, on_ref,
               y_ref, u_s, w_s, qg_s, a_s, kdt_s, el_s, st_s):
    t_len = q_ref.shape[0]
    n_chunks = t_len // DN_CHUNK
    c = DN_CHUNK
    hd = HEAD_DIM
    row = lax.broadcasted_iota(jnp.int32, (c, c), 0)
    col = lax.broadcasted_iota(jnp.int32, (c, c), 1)
    tril = row >= col
    strict = row > col
    tril16 = tril.astype(BF16)
    eye_f = (row == col).astype(F32)
    base_blk = (row // DN_BASE) == (col // DN_BASE)
    pair_blks = []
    size = DN_BASE
    while size < c:
        pair_blks.append(((row // (2 * size)) == (col // (2 * size))) & ((row // size) != (col // size)))
        size *= 2

    def conv_silu(ref, cw_ref, ci):
        start = pl.multiple_of(ci * c, c)
        cur = ref[pl.ds(start, c), :].astype(F32)
        pstart = pl.multiple_of(jnp.maximum(start - DN_HALO, 0), DN_HALO)
        prev = ref[pl.ds(pstart, DN_HALO), :].astype(F32)
        prev = jnp.where(ci > 0, prev, 0.0)
        rows = jnp.concatenate([prev, cur], axis=0)
        cw = cw_ref[...]
        acc = rows * cw[DN_CONV - 1:DN_CONV, :]
        for s in range(1, DN_CONV):
            acc = acc + pltpu.roll(rows, s, axis=0) * cw[DN_CONV - 1 - s:DN_CONV - s, :]
        return _silu(acc[DN_HALO:, :])

    def prepare(ci, carry):
        start = pl.multiple_of(ci * c, c)
        xq_all = conv_silu(q_ref, cq_ref, ci)
        xk_all = conv_silu(k_ref, ck_ref, ci)
        xv_all = conv_silu(v_ref, cv_ref, ci)
        sm = sm_ref[pl.ds(start, c), :]
        beta_all = jax.nn.sigmoid(sm)
        g_all = -jnp.exp(alog_ref[...]) * _softplus(sm + dtb_ref[...])
        g_hi = g_all.astype(BF16)
        g_r = g_all - g_hi.astype(F32)
        g_mid = g_r.astype(BF16)
        g_lo = (g_r - g_mid.astype(F32)).astype(BF16)
        gc_all = _dot(tril16, g_hi) + (_dot(tril16, g_mid) + _dot(tril16, g_lo))
        hs = range(HEADS)
        sls = [slice(h * hd, (h + 1) * hd) for h in hs]
        qn = [xq_all[:, sl] * (lax.rsqrt(jnp.sum(xq_all[:, sl] * xq_all[:, sl], axis=-1, keepdims=True) + NORM_EPS)
                               * (hd ** -0.5)) for sl in sls]
        kn = [xk_all[:, sl] * lax.rsqrt(jnp.sum(xk_all[:, sl] * xk_all[:, sl], axis=-1, keepdims=True) + NORM_EPS)
              for sl in sls]
        beta_b = [jnp.broadcast_to(beta_all[:, SM_BETA + h:SM_BETA + h + 1], (c, hd)) for h in hs]
        gc = [jnp.broadcast_to(gc_all[:, SM_DECAY + h:SM_DECAY + h + 1], (c, hd)) for h in hs]
        decay = [jnp.exp(jnp.where(tril, gc[h] - gc[h].T, -jnp.inf)) for h in hs]
        kb = [kn[h] * beta_b[h] for h in hs]
        kn16 = [kn[h].astype(BF16) for h in hs]
        kq = [_nt_dot(jnp.concatenate([kb[h].astype(BF16), qn[h].astype(BF16)], axis=0), kn16[h]) for h in hs]
        lmat = [jnp.where(strict, kq[h][:c, :] * decay[h], 0.0) for h in hs]
        l_hi = [lmat[h].astype(BF16) for h in hs]
        l_lo = [(lmat[h] - l_hi[h].astype(F32)).astype(BF16) for h in hs]
        zero16 = jnp.zeros((c, c), BF16)
        d_parts = [(jnp.where(base_blk, l_hi[h], zero16), jnp.where(base_blk, l_lo[h], zero16)) for h in hs]
        tmat = [eye_f - jnp.where(base_blk, lmat[h], 0.0) for h in hs]
        power = [_dot_split(d_parts[h], d_parts[h]) for h in hs]
        span = 2
        while span < DN_BASE:
            parts = [_split_bf16(power[h]) for h in hs]
            t_parts = [_split_bf16(tmat[h]) for h in hs]
            span *= 2
            if span < DN_BASE:
                both = [_dot_split((jnp.concatenate([t_parts[h][0], parts[h][0]], axis=0),
                                    jnp.concatenate([t_parts[h][1], parts[h][1]], axis=0)), parts[h]) for h in hs]
                tmat = [tmat[h] + both[h][:c, :] for h in hs]
                power = [both[h][c:, :] for h in hs]
            else:
                tmat = [tmat[h] + _dot_split(t_parts[h], parts[h]) for h in hs]
        for pair_blk in pair_blks:
            t_parts = [_split_bf16(tmat[h]) for h in hs]
            off = [(jnp.where(pair_blk, l_hi[h], zero16), jnp.where(pair_blk, l_lo[h], zero16)) for h in hs]
            cx = [_dot_split(off[h], t_parts[h]) for h in hs]
            tmat = [tmat[h] - _dot_split(t_parts[h], _split_bf16(cx[h])) for h in hs]
        for h in hs:
            sl = sls[h]
            t16 = tmat[h].astype(BF16)
            egc = jnp.exp(gc[h])
            uw = _dot(t16, jnp.concatenate([(xv_all[:, sl] * beta_b[h]).astype(BF16), (kb[h] * egc).astype(BF16)],
                                           axis=1))
            u_s[pl.ds(start, c), sl] = uw[:, :hd]
            w_s[pl.ds(start, c), sl] = uw[:, hd:].astype(BF16)
            qg_s[pl.ds(start, c), sl] = (qn[h] * egc).astype(BF16)
            a_s[pl.ds(start, c), sl] = jnp.where(tril, kq[h][c:, :] * decay[h], 0.0).astype(BF16)
            g_last = gc[h][c - 1:c, :]
            kd = kn[h] * jnp.exp(g_last - gc[h])
            kdt_s[pl.ds(start, c), sl] = kd.T.astype(BF16)
            el_s[pl.ds(pl.multiple_of(ci * 8, 8), 8), sl] = jnp.broadcast_to(jnp.exp(g_last), (8, hd))
        return carry

    lax.fori_loop(0, n_chunks, prepare, 0)

    st_s[...] = jnp.zeros(st_s.shape, F32)

    def scan(ci, carry):
        start = pl.multiple_of(ci * c, c)
        hs = range(HEADS)
        sls = [slice(h * hd, (h + 1) * hd) for h in hs]
        state = [st_s[h] for h in hs]
        s16 = [state[h].astype(BF16) for h in hs]
        v_new = [u_s[pl.ds(start, c), sls[h]] - _dot(w_s[pl.ds(start, c), sls[h]], s16[h]) for h in hs]
        v16 = [v_new[h].astype(BF16) for h in hs]
        for h in hs:
            e_last = el_s[pl.ds(pl.multiple_of(ci * 8, 8), 8), sls[h]][0:1, :]
            st_s[h] = state[h] * e_last + _dot(kdt_s[pl.ds(start, c), sls[h]], v16[h])
        for h in hs:
            sl = sls[h]
            o = _dot(qg_s[pl.ds(start, c), sl], s16[h]) + _dot(a_s[pl.ds(start, c), sl], v16[h])
            on = o * lax.rsqrt(jnp.mean(o * o, axis=-1, keepdims=True) + NORM_EPS) * on_ref[...]
            zg = z_ref[pl.ds(start, c), sl].astype(F32)
            y_ref[pl.ds(start, c), sl] = (on * _silu(zg)).astype(y_ref.dtype)
        return carry

    lax.fori_loop(0, n_chunks, scan, 0)


def _deltanet_branch(big3d, small3d, conv_q, conv_k, conv_v, alog_row, dtb_row, onorm_row):
    b, t, _ = big3d.shape
    hd = HEAD_DIM
    w = BRANCH_W
    seq_spec = lambda col: pl.BlockSpec((None, t, w), lambda i, col=col: (i, 0, col // w))
    conv_spec = pl.BlockSpec((DN_CONV, w), lambda i: (0, 0))
    row_spec = pl.BlockSpec((1, hd), lambda i: (0, 0))
    return pl.pallas_call(
        _dn_kernel,
        grid=(b,),
        in_specs=[
            seq_spec(COL_QA), seq_spec(COL_KA), seq_spec(COL_VA), seq_spec(COL_ZA),
            conv_spec, conv_spec, conv_spec,
            pl.BlockSpec((None, t, hd), lambda i: (i, 0, SMALL_W // hd - 1)),
            row_spec, row_spec, row_spec,
        ],
        out_specs=pl.BlockSpec((None, t, w), lambda i: (i, 0, 0)),
        out_shape=jax.ShapeDtypeStruct((b, t, w), BF16),
        scratch_shapes=[
            pltpu.VMEM((t, w), F32),
            pltpu.VMEM((t, w), BF16),
            pltpu.VMEM((t, w), BF16),
            pltpu.VMEM((t, w), BF16),
            pltpu.VMEM((t, w), BF16),
            pltpu.VMEM((8 * t // DN_CHUNK, w), F32),
            pltpu.VMEM((HEADS, hd, hd), F32),
        ],
        compiler_params=_cparams(1),
        name="deltanet_branch",
    )(big3d, big3d, big3d, big3d, conv_q, conv_k, conv_v, small3d, alog_row, dtb_row, onorm_row)


def _attn_prep_kernel(q_ref, k_ref, v_ref, sm_ref, qc_ref, qs_ref, kc_ref, ks_ref,
                      icos_ref, isin_up_ref, isin_dn_ref,
                      qo_ref, ko_ref, vt_ref, qio_ref, kia_ref, kib_ref, wt_ref):
    hd = HEAD_DIM
    lane_r = lax.broadcasted_iota(jnp.int32, (hd, hd), 0)
    lane_c = lax.broadcasted_iota(jnp.int32, (hd, hd), 1)
    swap_halves = (lane_r == (lane_c + hd // 2) % hd).astype(BF16)

    def norm_rope(ref, cos_g, sin_g, out_ref, scale):
        for h in range(HEADS):
            sl = slice(h * hd, (h + 1) * hd)
            x16 = ref[:, sl]
            x = x16.astype(F32)
            inv = lax.rsqrt(jnp.mean(x * x, axis=-1, keepdims=True) + NORM_EPS) * scale
            out_ref[:, sl] = ((x * cos_g + _dot(x16, swap_halves) * sin_g) * inv).astype(out_ref.dtype)

    norm_rope(q_ref, qc_ref[...], qs_ref[...], qo_ref, hd ** -0.5 * LOG2_E)
    norm_rope(k_ref, kc_ref[...], ks_ref[...], ko_ref, 1.0)
    w = v_ref.shape[1]
    eye = (lax.broadcasted_iota(jnp.int32, (w, w), 0) == lax.broadcasted_iota(jnp.int32, (w, w), 1)).astype(BF16)
    vt_ref[...] = _nt_dot(eye, v_ref[...]).astype(vt_ref.dtype)

    sm = sm_ref[...]
    iq_w = HEADS * IDX_DIM
    half = IDX_DIM // 2
    iq = sm[:, :iq_w]
    iq_r = (iq * icos_ref[...] + pltpu.roll(iq, iq_w - half, axis=1) * isin_up_ref[...]
            + pltpu.roll(iq, half, axis=1) * isin_dn_ref[...])
    qio_ref[...] = iq_r.astype(qio_ref.dtype)
    last = sm[:, iq_w:]
    lw = last.shape[1]
    ik_r = (last * icos_ref[:, :lw] + pltpu.roll(last, lw - half, axis=1) * isin_up_ref[:, :lw]
            + pltpu.roll(last, half, axis=1) * isin_dn_ref[:, :lw])
    lane = lax.broadcasted_iota(jnp.int32, ik_r.shape, 1)
    ik_r = jnp.where(lane < IDX_DIM, ik_r, 0.0)
    kia_ref[...] = ik_r.astype(kia_ref.dtype)
    kib_ref[...] = pltpu.roll(ik_r, IDX_DIM, axis=1).astype(kib_ref.dtype)
    wt = last.T
    wt_ref[...] = wt[SM_IDXW:SM_IDXW + 8, :] * (HEADS ** -0.5 * IDX_DIM ** -0.5)


def _attn_prep(big3d, small3d, tabs):
    b, t, _ = big3d.shape
    w = BRANCH_W
    tt = PREP_TT
    iq_w = HEADS * IDX_DIM
    seq = lambda col: pl.BlockSpec((None, tt, w), lambda i, j, col=col: (i, j, col // w))
    tab = lambda width: pl.BlockSpec((tt, width), lambda i, j: (j, 0))
    return pl.pallas_call(
        _attn_prep_kernel,
        grid=(b, t // tt),
        in_specs=[
            seq(COL_QC), seq(COL_KC), seq(COL_VC),
            pl.BlockSpec((None, tt, SMALL_W), lambda i, j: (i, j, 0)),
            tab(HEAD_DIM), tab(HEAD_DIM), tab(HEAD_DIM), tab(HEAD_DIM), tab(iq_w), tab(iq_w), tab(iq_w),
        ],
        out_specs=[
            pl.BlockSpec((None, tt, w), lambda i, j: (i, j, 0)),
            pl.BlockSpec((None, tt, w), lambda i, j: (i, j, 0)),
            pl.BlockSpec((None, w, tt), lambda i, j: (i, 0, j)),
            pl.BlockSpec((None, tt, iq_w), lambda i, j: (i, j, 0)),
            pl.BlockSpec((None, tt, 2 * IDX_DIM), lambda i, j: (i, j, 0)),
            pl.BlockSpec((None, tt, 2 * IDX_DIM), lambda i, j: (i, j, 0)),
            pl.BlockSpec((None, 8, tt), lambda i, j: (i, 0, j)),
        ],
        out_shape=[
            jax.ShapeDtypeStruct((b, t, w), BF16),
            jax.ShapeDtypeStruct((b, t, w), BF16),
            jax.ShapeDtypeStruct((b, w, t), BF16),
            jax.ShapeDtypeStruct((b, t, iq_w), BF16),
            jax.ShapeDtypeStruct((b, t, 2 * IDX_DIM), BF16),
            jax.ShapeDtypeStruct((b, t, 2 * IDX_DIM), BF16),
            jax.ShapeDtypeStruct((b, 8, t), F32),
        ],
        compiler_params=_cparams(2),
        name="attn_prep",
    )(big3d, big3d, big3d, small3d, *tabs)


def _dsa_kernel(q_ref, k_ref, vt_ref, qi_ref, kia_ref, kib_ref, wt_ref, z_ref, y_ref, s_ref, acc_ref, *, topk):
    qb, kc = ATT_QB, ATT_KC
    blk_i = pl.program_id(1)
    n_kc = blk_i + 1
    q_pos = blk_i * qb + lax.broadcasted_iota(jnp.int32, (1, qb), 1)
    qi = qi_ref[...]
    wt = wt_ref[...]
    inf = jnp.inf

    def col_sum(x):
        return x.reshape(kc // 8, 8, qb).sum(axis=0)

    def total(x):
        return jnp.sum(x, axis=0, keepdims=True)

    def score_chunk(ci, carry, diagonal):
        vmax, vmin, min_pos, n_pos, n_nonneg = carry
        off = pl.multiple_of(ci * kc, kc)
        ka = kia_ref[pl.ds(off, kc), :]
        kb = kib_ref[pl.ds(off, kc), :]
        s = jnp.zeros((kc, qb), F32)
        for h in range(HEADS):
            kk = ka if h % 2 == 0 else kb
            qq = qi[:, (h // 2) * 2 * IDX_DIM:(h // 2 + 1) * 2 * IDX_DIM]
            s = s + jnp.maximum(_nt_dot(kk, qq), 0.0) * wt[h:h + 1, :]
        s = jnp.where(s == 0.0, 0.0, s)
        if diagonal:
            key_pos = off + lax.broadcasted_iota(jnp.int32, (kc, 1), 0)
            causal = key_pos <= q_pos
            sc = jnp.where(causal, s, -inf)
            s_hi = jnp.where(causal, s, inf)
        else:
            sc = s_hi = s
        s_ref[pl.ds(off, kc), :] = sc
        vmax = jnp.maximum(vmax, jnp.max(sc, axis=0, keepdims=True))
        vmin = jnp.minimum(vmin, jnp.min(s_hi, axis=0, keepdims=True))
        min_pos = jnp.minimum(min_pos, jnp.min(jnp.where(sc > 0.0, sc, inf), axis=0, keepdims=True))
        n_pos = n_pos + col_sum(jnp.where(sc > 0.0, 1.0, 0.0))
        n_nonneg = n_nonneg + col_sum(jnp.where(sc >= 0.0, 1.0, 0.0))
        return vmax, vmin, min_pos, n_pos, n_nonneg

    row_inf = jnp.full((1, qb), inf, F32)
    zeros8 = jnp.zeros((8, qb), F32)
    stats = lax.fori_loop(0, blk_i, functools.partial(score_chunk, diagonal=False),
                          (-row_inf, row_inf, row_inf, zeros8, zeros8))
    vmax, vmin, min_pos, n_pos, n_nonneg = score_chunk(blk_i, stats, diagonal=True)
    n_pos = total(n_pos)
    n_nonneg = total(n_nonneg)

    def count_ge(thr):
        def body(ci, acc):
            off = pl.multiple_of(ci * kc, kc)
            hit = jnp.where(s_ref[pl.ds(off, kc), :] >= thr, 1.0, 0.0)
            return acc + hit.reshape(kc // (8 * COUNT_LANES), COUNT_LANES * 8, qb).sum(axis=0)
        return total(lax.fori_loop(0, n_kc, body, jnp.zeros((COUNT_LANES * 8, qb), F32)))

    k_sel = jnp.minimum(q_pos + 1, topk).astype(F32)
    n_causal = (q_pos + 1).astype(F32)
    at_zero = (n_pos < k_sel) & (k_sel <= n_nonneg)
    above = k_sel <= n_pos
    lo = jnp.where(at_zero, 0.0, jnp.where(above, min_pos, vmin))
    hi = jnp.where(at_zero, min_pos, jnp.where(above, vmax + (jnp.abs(vmax) + 1.0), 0.0))
    c_lo = jnp.where(at_zero, n_nonneg, jnp.where(above, n_pos, n_causal))
    c_hi = jnp.where(at_zero, n_pos, jnp.where(above, 0.0, n_nonneg))
    done = jnp.where(at_zero | (c_lo == k_sel), 1.0, 0.0)

    def n_open(d):
        return jnp.sum(1.0 - d)

    def bisect_cond(carry):
        it, n_left = carry[0], carry[1]
        return (it < NUM_BISECT) & (n_left > 0.0)

    def bisect_body(carry):
        it, _, lo, hi, c_lo, c_hi, done = carry
        for _ in range(BISECT_UNROLL):
            mid = 0.5 * lo + 0.5 * hi
            cnt = count_ge(mid)
            live = done < 0.5
            up = (cnt >= k_sel) & live
            dn = (cnt < k_sel) & live
            lo = jnp.where(up, mid, lo)
            c_lo = jnp.where(up, cnt, c_lo)
            hi = jnp.where(dn, mid, hi)
            c_hi = jnp.where(dn, cnt, c_hi)
            done = jnp.where(c_lo == k_sel, 1.0, done)
        return it + BISECT_UNROLL, n_open(done), lo, hi, c_lo, c_hi, done

    _, _, lo, hi, c_lo, c_hi, done = lax.while_loop(
        bisect_cond, bisect_body, (jnp.int32(0), n_open(done), lo, hi, c_lo, c_hi, done))
    need = k_sel - c_hi
    n_tied = jnp.sum(jnp.where(c_lo - c_hi > need, 1.0, 0.0))

    def tie_mask():
        tri = (lax.broadcasted_iota(jnp.int32, (kc, kc), 1)
               < lax.broadcasted_iota(jnp.int32, (kc, kc), 0)).astype(BF16)

        def mask_body(ci, seen):
            off = pl.multiple_of(ci * kc, kc)
            blk = s_ref[pl.ds(off, kc), :]
            tie = jnp.where((blk >= lo) & (blk < hi), 1.0, 0.0)
            rank = _dot(tri, tie.astype(BF16)) + seen
            sel = (blk >= hi) | ((tie > 0.5) & (rank < need))
            s_ref[pl.ds(off, kc), :] = jnp.where(sel, 1.0, 0.0)
            return seen + total(col_sum(tie))

        lax.fori_loop(0, n_kc, mask_body, jnp.zeros((1, qb), F32))

    pl.when(n_tied > 0.0)(tie_mask)
    sel_thr = jnp.where(n_tied > 0.0, 0.5, lo)

    hs = range(HEADS)
    sls = [slice(h * HEAD_DIM, (h + 1) * HEAD_DIM) for h in hs]
    qh = [q_ref[:, sl] for sl in sls]
    acc_ref[...] = jnp.zeros(acc_ref.shape, F32)

    def att_body(ci, carry):
        ms, ls = carry
        off = pl.multiple_of(ci * kc, kc)
        sel = s_ref[pl.ds(off, kc), :] >= sel_thr
        lm = [jnp.where(sel, _nt_dot(k_ref[pl.ds(off, kc), sls[h]], qh[h]), MASK_NEG) for h in hs]
        m_new = [jnp.maximum(ms[h], jnp.max(lm[h], axis=0, keepdims=True)) for h in hs]
        p = [jnp.exp2(lm[h] - m_new[h]) for h in hs]
        alpha = [jnp.exp2(ms[h] - m_new[h]) for h in hs]
        l_new = [alpha[h] * ls[h] + jnp.sum(p[h], axis=0, keepdims=True) for h in hs]
        pv = [_dot(vt_ref[sls[h], pl.ds(off, kc)], p[h].astype(BF16)) for h in hs]
        for h in hs:
            acc_ref[sls[h], :] = alpha[h] * acc_ref[sls[h], :] + pv[h]
        return tuple(m_new), tuple(l_new)

    row_neg = jnp.full((1, qb), MASK_NEG, F32)
    row_zero = jnp.zeros((1, qb), F32)
    _, ls = lax.fori_loop(0, n_kc, att_body, ((row_neg,) * HEADS, (row_zero,) * HEADS))
    for h in hs:
        o = (acc_ref[sls[h], :] / ls[h]).T
        y_ref[:, sls[h]] = (o * _silu(z_ref[:, sls[h]].astype(F32))).astype(y_ref.dtype)


def _dsa_branch(big3d, q_r, k_r, v_t, qi_r, ki_a, ki_b, w_t):
    b, t, _ = big3d.shape
    w = BRANCH_W
    qb = ATT_QB
    topk = min(TOPK_MAX, t // 4)
    return pl.pallas_call(
        functools.partial(_dsa_kernel, topk=topk),
        grid=(b, t // qb),
        in_specs=[
            pl.BlockSpec((None, qb, w), lambda i, j: (i, j, 0)),
            pl.BlockSpec((None, t, w), lambda i, j: (i, 0, 0)),
            pl.BlockSpec((None, w, t), lambda i, j: (i, 0, 0)),
            pl.BlockSpec((None, qb, HEADS * IDX_DIM), lambda i, j: (i, j, 0)),
            pl.BlockSpec((None, t, 2 * IDX_DIM), lambda i, j: (i, 0, 0)),
            pl.BlockSpec((None, t, 2 * IDX_DIM), lambda i, j: (i, 0, 0)),
            pl.BlockSpec((None, 8, qb), lambda i, j: (i, 0, j)),
            pl.BlockSpec((None, qb, w), lambda i, j: (i, j, COL_ZC // w)),
        ],
        out_specs=pl.BlockSpec((None, qb, w), lambda i, j: (i, j, 0)),
        out_shape=jax.ShapeDtypeStruct((b, t, w), BF16),
        scratch_shapes=[pltpu.VMEM((t, qb), F32),
                        pltpu.VMEM((w, qb), F32)],
        compiler_params=_cparams(2),
        name="sparse_attention",
    )(q_r, k_r, v_t, qi_r, ki_a, ki_b, w_t, big3d)


def _merge_kernel(ya_ref, yb_ref, yc_ref, ga_ref, gb_ref, gc_ref, bias_ref, wb_ref, wo_ref, x_ref, o_ref):
    merged = None
    for n, (y_ref, g_ref) in enumerate(((ya_ref, ga_ref), (yb_ref, gb_ref), (yc_ref, gc_ref))):
        proj = _dot(y_ref[...], wb_ref[n])
        gate = jax.nn.sigmoid(g_ref[...].astype(F32) + bias_ref[n:n + 1, :])
        term = gate * proj
        merged = term if merged is None else merged + term
    o_ref[...] = x_ref[...] + _dot(merged.astype(BF16), wo_ref[...])


def _merge(ya, yb, yc, big2d, gate_b, w_branch, w_out, x2d):
    m = x2d.shape[0]
    tm = MERGE_TM
    yspec = pl.BlockSpec((tm, BRANCH_W), lambda i: (i, 0))
    gspec = lambda n: pl.BlockSpec((tm, D_MODEL), lambda i, n=n: (i, COL_GATES // D_MODEL + n))
    return pl.pallas_call(
        _merge_kernel,
        grid=(m // tm,),
        in_specs=[
            yspec, yspec, yspec, gspec(0), gspec(1), gspec(2),
            pl.BlockSpec((N_BRANCH, D_MODEL), lambda i: (0, 0)),
            pl.BlockSpec((N_BRANCH, BRANCH_W, D_MODEL), lambda i: (0, 0, 0)),
            pl.BlockSpec((D_MODEL, D_MODEL), lambda i: (0, 0)),
            pl.BlockSpec((tm, D_MODEL), lambda i: (i, 0)),
        ],
        out_specs=pl.BlockSpec((tm, D_MODEL), lambda i: (i, 0)),
        out_shape=jax.ShapeDtypeStruct((m, D_MODEL), F32),
        compiler_params=_cparams(1),
        name="merge",
    )(ya, yb, yc, big2d, big2d, big2d, gate_b, w_branch, w_out, x2d)


_W_IN_SEGMENTS = (
    (0, COL_QA, 2048),
    (2056, COL_UB, 1024),
    (3080, COL_QC, 2048),
    (5452, COL_GATES, N_BRANCH * D_MODEL),
    (5128, BIG_W, HEADS * IDX_DIM),
    (5384, BIG_W + HEADS * IDX_DIM, IDX_DIM),
    (2048, BIG_W + HEADS * IDX_DIM + SM_BETA, 8),
    (5448, BIG_W + HEADS * IDX_DIM + SM_IDXW, HEADS),
)
_W_IN_USED = BIG_W + HEADS * IDX_DIM + SM_IDXW + HEADS
PERM_ROWS = 128


def _permute_kernel(w_ref, o_ref):
    for src, dst, n in _W_IN_SEGMENTS:
        o_ref[:, dst:dst + n] = w_ref[:, src:src + n].astype(o_ref.dtype)
    pad = BIG_W + SMALL_W - _W_IN_USED
    o_ref[:, _W_IN_USED:] = jnp.zeros((o_ref.shape[0], pad), o_ref.dtype)


def _permute_w_in(w_in):
    depth, d, in_width = w_in.shape
    return pl.pallas_call(
        _permute_kernel,
        grid=(depth, d // PERM_ROWS),
        in_specs=[pl.BlockSpec((None, PERM_ROWS, in_width), lambda l, i: (l, i, 0))],
        out_specs=pl.BlockSpec((None, PERM_ROWS, BIG_W + SMALL_W), lambda l, i: (l, i, 0)),
        out_shape=jax.ShapeDtypeStruct((depth, d, BIG_W + SMALL_W), BF16),
        compiler_params=_cparams(2),
        name="permute_w_in",
    )(w_in)


def _rope_tables(t):
    def base(dim):
        inv_freq = ROPE_THETA ** (-jnp.arange(0, dim, 2, dtype=F32) / dim)
        ang = jnp.arange(t, dtype=F32)[:, None] * inv_freq[None, :]
        return jnp.cos(ang), jnp.sin(ang)

    cos_a, sin_a = base(HEAD_DIM)
    cos = jnp.concatenate([cos_a, cos_a], axis=-1)
    sin = jnp.concatenate([-sin_a, sin_a], axis=-1)
    cos_i, sin_i = base(IDX_DIM)
    zero = jnp.zeros_like(sin_i)
    icos = jnp.tile(jnp.concatenate([cos_i, cos_i], axis=-1), (1, HEADS))
    isin_up = jnp.tile(jnp.concatenate([-sin_i, zero], axis=-1), (1, HEADS))
    isin_dn = jnp.tile(jnp.concatenate([zero, sin_i], axis=-1), (1, HEADS))
    return cos, sin, icos, isin_up, isin_dn


def _lane_row(vals, offset):
    d, n = vals.shape
    return jnp.zeros((d, 1, HEAD_DIM), F32).at[:, 0, offset:offset + n].set(vals.astype(F32))


def kernel(x, norm_g, w_in, gate_b, conv_w, a_log, dt_bias, dn_onorm, pool_w, pool_scale, q_norm, k_norm,
           w_branch, w_out):
    b, t, d = x.shape
    depth = norm_g.shape[0]
    w_p = _permute_w_in(w_in)
    wb16 = w_branch.astype(BF16)
    wo16 = w_out.astype(BF16)
    pw16 = pool_w.astype(BF16)
    alog_rows = _lane_row(a_log, SM_DECAY)
    dtb_rows = _lane_row(dt_bias, SM_DECAY)
    cos, sin, *idx_tabs = _rope_tables(t)
    idx_tabs = tuple(idx_tabs)
    x2d = x.reshape(b * t, d)
    for layer in range(depth):
        big2d, small2d = _in_projection(x2d, norm_g[layer][None, :], w_p, layer)
        big3d = big2d.reshape(b, t, BIG_W)
        small3d = small2d.reshape(b, t, SMALL_W)
        cw = conv_w[layer]
        ya = _deltanet_branch(big3d, small3d, cw[:, 0:BRANCH_W], cw[:, BRANCH_W:2 * BRANCH_W],
                              cw[:, 2 * BRANCH_W:], alog_rows[layer], dtb_rows[layer],
                              dn_onorm[layer][None, :])
        yb = _pool_branch(big3d, pw16[layer], pool_scale[layer][None, :])
        half = HEAD_DIM // 2
        gain_tabs = (cos * q_norm[layer], sin * jnp.roll(q_norm[layer], half),
                     cos * k_norm[layer], sin * jnp.roll(k_norm[layer], half))
        prep = _attn_prep(big3d, small3d, gain_tabs + idx_tabs)
        yc = _dsa_branch(big3d, *prep)
        x2d = _merge(ya.reshape(b * t, BRANCH_W), yb.reshape(b * t, BRANCH_W), yc.reshape(b * t, BRANCH_W),
                     big2d, gate_b[layer], wb16[layer], wo16[layer], x2d)
    return x2d.reshape(b, t, d)
```

```python
import functools

import numpy as np
import jax
import jax.numpy as jnp
from jax import lax
from jax.experimental import pallas as pl
from jax.experimental.pallas import tpu as pltpu

F32 = jnp.float32
BF16 = jnp.bfloat16
HIGHEST = lax.Precision.HIGHEST

D_MODEL = 1024
HEADS = 4
HEAD_DIM = 128
BRANCH_W = HEADS * HEAD_DIM
DN_CONV = 4
POOL_WINDOWS = (2, 4, 8, 16)
POOL_GROUP = 128
IDX_DIM = 64
TOPK_MAX = 256
ROPE_THETA = 10000.0
NORM_EPS = 1e-6
N_BRANCH = 3

COL_QA, COL_KA, COL_VA, COL_ZA = 0, 512, 1024, 1536
COL_UB, COL_ZB = 2048, 2560
COL_QC, COL_KC, COL_VC, COL_ZC = 3072, 3584, 4096, 4608
COL_GATES = 5120
BIG_W = 8192
SMALL_W = 384
SM_BETA, SM_DECAY, SM_IDXW = 64, 68, 72

V7X_VMEM_LIMIT = 56 * 1024 * 1024

PROJ_TM = 512
PROJ_TN = 512
MERGE_TM = 1024
PREP_TT = 512
POOL_TT = 256
POOL_HALO = 16
DN_CHUNK = 128
DN_HALO = 16
DN_BASE = 8
DN_PAR = 2
ATT_QB = 256
ATT_KC = 256
NUM_BISECT = 36
BISECT_UNROLL = 4
COUNT_LANES = 4
MASK_NEG = -1e30
LOG2_E = 1.4426950408889634


def _nt_dot(a, b):
    return lax.dot_general(a, b, (((1,), (1,)), ((), ())), preferred_element_type=F32)


def _dot(a, b, precision=None):
    return jnp.dot(a, b, preferred_element_type=F32, precision=precision)


def _silu(x):
    return x * jax.nn.sigmoid(x)


def _cparams(n_axes):
    return pltpu.CompilerParams(dimension_semantics=("arbitrary",) * n_axes,
                                vmem_limit_bytes=V7X_VMEM_LIMIT)


def _inproj_kernel(x_ref, g_ref, w_ref, big_ref, small_ref):
    x = x_ref[...]
    ms = jnp.mean(x * x, axis=-1, keepdims=True)
    h = ((x * lax.rsqrt(ms + NORM_EPS)) * g_ref[...]).astype(BF16)
    for c in range(0, BIG_W, PROJ_TN):
        big_ref[:, c:c + PROJ_TN] = _dot(h, w_ref[:, c:c + PROJ_TN]).astype(big_ref.dtype)
    small_ref[...] = _dot(h, w_ref[:, BIG_W:])


def _in_projection(x2d, g_row, w_p, layer):
    m = x2d.shape[0]
    return pl.pallas_call(
        _inproj_kernel,
        grid=(m // PROJ_TM,),
        in_specs=[
            pl.BlockSpec((PROJ_TM, D_MODEL), lambda i: (i, 0)),
            pl.BlockSpec((1, D_MODEL), lambda i: (0, 0)),
            pl.BlockSpec((None, D_MODEL, BIG_W + SMALL_W), lambda i: (layer, 0, 0), pipeline_mode=pl.Buffered(1)),
        ],
        out_specs=[
            pl.BlockSpec((PROJ_TM, BIG_W), lambda i: (i, 0)),
            pl.BlockSpec((PROJ_TM, SMALL_W), lambda i: (i, 0)),
        ],
        out_shape=[
            jax.ShapeDtypeStruct((m, BIG_W), BF16),
            jax.ShapeDtypeStruct((m, SMALL_W), F32),
        ],
        compiler_params=_cparams(1),
        name="in_projection",
    )(x2d, g_row, w_p)


def _pool_kernel(u_ref, z_ref, pw_ref, ps_ref, y_ref):
    t_len = u_ref.shape[0]
    for t in range(t_len // POOL_TT):
        r0 = t * POOL_TT
        cur = u_ref[r0:r0 + POOL_TT, :].astype(F32)
        if t == 0:
            prev = jnp.zeros((POOL_HALO, cur.shape[1]), F32)
        else:
            prev = u_ref[r0 - POOL_HALO:r0, :].astype(F32)
        win_rows = jnp.concatenate([prev, cur], axis=0)
        pos = r0 + lax.broadcasted_iota(jnp.int32, (POOL_TT, 1), 0)
        for gi, win in enumerate(POOL_WINDOWS):
            sl = slice(gi * POOL_GROUP, (gi + 1) * POOL_GROUP)
            s = win_rows[:, sl]
            shift = 1
            while shift < win:
                s = s + pltpu.roll(s, shift, axis=0)
                shift *= 2
            count = jnp.minimum(pos + 1, win).astype(F32)
            pooled = s[POOL_HALO:, :] / count - cur[:, sl]
            mixed = _dot(pooled.astype(BF16), pw_ref[gi])
            zg = z_ref[r0:r0 + POOL_TT, sl].astype(F32)
            y_ref[r0:r0 + POOL_TT, sl] = (mixed * ps_ref[:, sl] * _silu(zg)).astype(y_ref.dtype)


def _pool_branch(big3d, pool_w, pool_scale_row):
    b, t, _ = big3d.shape
    width = POOL_GROUP * len(POOL_WINDOWS)
    return pl.pallas_call(
        _pool_kernel,
        grid=(b,),
        in_specs=[
            pl.BlockSpec((None, t, width), lambda i: (i, 0, COL_UB // width)),
            pl.BlockSpec((None, t, width), lambda i: (i, 0, COL_ZB // width)),
            pl.BlockSpec((len(POOL_WINDOWS), POOL_GROUP, POOL_GROUP), lambda i: (0, 0, 0)),
            pl.BlockSpec((1, width), lambda i: (0, 0)),
        ],
        out_specs=pl.BlockSpec((None, t, width), lambda i: (i, 0, 0)),
        out_shape=jax.ShapeDtypeStruct((b, t, width), BF16),
        compiler_params=_cparams(1),
        name="pool_branch",
    )(big3d, big3d, pool_w, pool_scale_row)


def _softplus(x):
    return jnp.maximum(x, 0.0) + jnp.log1p(jnp.exp(-jnp.abs(x)))


def _split_bf16(a):
    hi = a.astype(BF16)
    lo = (a - hi.astype(F32)).astype(BF16)
    return hi, lo


def _dot_split(a, b):
    a_hi, a_lo = a
    b_hi, b_lo = b
    return _dot(jnp.concatenate([a_hi, a_lo, a_hi], axis=1), jnp.concatenate([b_hi, b_hi, b_lo], axis=0))


def _dn_kernel(q_ref, k_ref, v_ref, z_ref, cq_ref, ck_ref, cv_ref, sm_ref, alog_ref, dtb_ref, on_ref,
               y_ref, u_s, w_s, qg_s, a_s, kdt_s, el_s, st_s):
    t_len = q_ref.shape[0]
    n_chunks = t_len // DN_CHUNK
    c = DN_CHUNK
    hd = HEAD_DIM
    row = lax.broadcasted_iota(jnp.int32, (c, c), 0)
    col = lax.broadcasted_iota(jnp.int32, (c, c), 1)
    tril = row >= col
    strict = row > col
    tril16 = tril.astype(BF16)
    eye_f = (row == col).astype(F32)
    base_blk = (row // DN_BASE) == (col // DN_BASE)
    pair_blks = []
    size = DN_BASE
    while size < c:
        pair_blks.append(((row // (2 * size)) == (col // (2 * size))) & ((row // size) != (col // size)))
        size *= 2

    def conv_silu(ref, cw_ref, ci):
        start = pl.multiple_of(ci * c, c)
        cur = ref[pl.ds(start, c), :].astype(F32)
        pstart = pl.multiple_of(jnp.maximum(start - DN_HALO, 0), DN_HALO)
        prev = ref[pl.ds(pstart, DN_HALO), :].astype(F32)
        prev = jnp.where(ci > 0, prev, 0.0)
        rows = jnp.concatenate([prev, cur], axis=0)
        cw = cw_ref[...]
        acc = rows * cw[DN_CONV - 1:DN_CONV, :]
        for s in range(1, DN_CONV):
            acc = acc + pltpu.roll(rows, s, axis=0) * cw[DN_CONV - 1 - s:DN_CONV - s, :]
        return _silu(acc[DN_HALO:, :])

    def prepare(gi, carry):
        cis = [gi * DN_PAR + j for j in range(DN_PAR)]
        starts = [pl.multiple_of(ci * c, c) for ci in cis]
        xq_all = [conv_silu(q_ref, cq_ref, ci) for ci in cis]
        xk_all = [conv_silu(k_ref, ck_ref, ci) for ci in cis]
        xv_all = [conv_silu(v_ref, cv_ref, ci) for ci in cis]
        beta_all, gc_all = [], []
        for start in starts:
            sm = sm_ref[pl.ds(start, c), :]
            beta_all.append(jax.nn.sigmoid(sm))
            g_all = -jnp.exp(alog_ref[...]) * _softplus(sm + dtb_ref[...])
            g_hi = g_all.astype(BF16)
            g_r = g_all - g_hi.astype(F32)
            g_mid = g_r.astype(BF16)
            g_lo = (g_r - g_mid.astype(F32)).astype(BF16)
            gc_all.append(_dot(tril16, g_hi) + (_dot(tril16, g_mid) + _dot(tril16, g_lo)))
        items = [(j, h) for j in range(DN_PAR) for h in range(HEADS)]
        ids = range(len(items))
        sls = [slice(h * hd, (h + 1) * hd) for _, h in items]
        xq = [xq_all[j][:, sls[i]] for i, (j, _) in enumerate(items)]
        xk = [xk_all[j][:, sls[i]] for i, (j, _) in enumerate(items)]
        qn = [xq[i] * (lax.rsqrt(jnp.sum(xq[i] * xq[i], axis=-1, keepdims=True) + NORM_EPS) * (hd ** -0.5))
              for i in ids]
        kn = [xk[i] * lax.rsqrt(jnp.sum(xk[i] * xk[i], axis=-1, keepdims=True) + NORM_EPS) for i in ids]
        beta_b = [jnp.broadcast_to(beta_all[j][:, SM_BETA + h:SM_BETA + h + 1], (c, hd)) for j, h in items]
        gc = [jnp.broadcast_to(gc_all[j][:, SM_DECAY + h:SM_DECAY + h + 1], (c, hd)) for j, h in items]
        decay = [jnp.exp(jnp.where(tril, gc[i] - gc[i].T, -jnp.inf)) for i in ids]
        kb = [kn[i] * beta_b[i] for i in ids]
        kn16 = [kn[i].astype(BF16) for i in ids]
        kq = [_nt_dot(jnp.concatenate([kb[i].astype(BF16), qn[i].astype(BF16)], axis=0), kn16[i]) for i in ids]
        lmat = [jnp.where(strict, kq[i][:c, :] * decay[i], 0.0) for i in ids]
        l_hi = [lmat[i].astype(BF16) for i in ids]
        l_lo = [(lmat[i] - l_hi[i].astype(F32)).astype(BF16) for i in ids]
        zero16 = jnp.zeros((c, c), BF16)
        d_parts = [(jnp.where(base_blk, l_hi[i], zero16), jnp.where(base_blk, l_lo[i], zero16)) for i in ids]
        tmat = [eye_f - jnp.where(base_blk, lmat[i], 0.0) for i in ids]
        power = [_dot_split(d_parts[i], d_parts[i]) for i in ids]
        span = 2
        while span < DN_BASE:
            parts = [_split_bf16(power[i]) for i in ids]
            t_parts = [_split_bf16(tmat[i]) for i in ids]
            span *= 2
            if span < DN_BASE:
                both = [_dot_split((jnp.concatenate([t_parts[i][0], parts[i][0]], axis=0),
                                    jnp.concatenate([t_parts[i][1], parts[i][1]], axis=0)), parts[i]) for i in ids]
                tmat = [tmat[i] + both[i][:c, :] for i in ids]
                power = [both[i][c:, :] for i in ids]
            else:
                tmat = [tmat[i] + _dot_split(t_parts[i], parts[i]) for i in ids]
        for pair_blk in pair_blks:
            t_parts = [_split_bf16(tmat[i]) for i in ids]
            off = [(jnp.where(pair_blk, l_hi[i], zero16), jnp.where(pair_blk, l_lo[i], zero16)) for i in ids]
            cx = [_dot_split(off[i], t_parts[i]) for i in ids]
            tmat = [tmat[i] - _dot_split(t_parts[i], _split_bf16(cx[i])) for i in ids]
        for i, (j, h) in enumerate(items):
            sl = sls[i]
            start = starts[j]
            t16 = tmat[i].astype(BF16)
            egc = jnp.exp(gc[i])
            uw = _dot(t16, jnp.concatenate([(xv_all[j][:, sl] * beta_b[i]).astype(BF16), (kb[i] * egc).astype(BF16)],
                                           axis=1))
            u_s[pl.ds(start, c), sl] = uw[:, :hd]
            w_s[pl.ds(start, c), sl] = uw[:, hd:].astype(BF16)
            qg_s[pl.ds(start, c), sl] = (qn[i] * egc).astype(BF16)
            a_s[pl.ds(start, c), sl] = jnp.where(tril, kq[i][c:, :] * decay[i], 0.0).astype(BF16)
            g_last = gc[i][c - 1:c, :]
            kd = kn[i] * jnp.exp(g_last - gc[i])
            kdt_s[pl.ds(start, c), sl] = kd.T.astype(BF16)
            el_s[pl.ds(pl.multiple_of(cis[j] * 8, 8), 8), sl] = jnp.broadcast_to(jnp.exp(g_last), (8, hd))
        return carry

    lax.fori_loop(0, n_chunks // DN_PAR, prepare, 0)

    st_s[...] = jnp.zeros(st_s.shape, F32)

    def scan(ci, carry):
        start = pl.multiple_of(ci * c, c)
        hs = range(HEADS)
        sls = [slice(h * hd, (h + 1) * hd) for h in hs]
        state = [st_s[h] for h in hs]
        s16 = [state[h].astype(BF16) for h in hs]
        v_new = [u_s[pl.ds(start, c), sls[h]] - _dot(w_s[pl.ds(start, c), sls[h]], s16[h]) for h in hs]
        v16 = [v_new[h].astype(BF16) for h in hs]
        for h in hs:
            e_last = el_s[pl.ds(pl.multiple_of(ci * 8, 8), 8), sls[h]][0:1, :]
            st_s[h] = state[h] * e_last + _dot(kdt_s[pl.ds(start, c), sls[h]], v16[h])
        for h in hs:
            sl = sls[h]
            o = _dot(qg_s[pl.ds(start, c), sl], s16[h]) + _dot(a_s[pl.ds(start, c), sl], v16[h])
            on = o * lax.rsqrt(jnp.mean(o * o, axis=-1, keepdims=True) + NORM_EPS) * on_ref[...]
            zg = z_ref[pl.ds(start, c), sl].astype(F32)
            y_ref[pl.ds(start, c), sl] = (on * _silu(zg)).astype(y_ref.dtype)
        return carry

    lax.fori_loop(0, n_chunks, scan, 0)


def _deltanet_branch(big3d, small3d, conv_q, conv_k, conv_v, alog_row, dtb_row, onorm_row):
    b, t, _ = big3d.shape
    hd = HEAD_DIM
    w = BRANCH_W
    seq_spec = lambda col: pl.BlockSpec((None, t, w), lambda i, col=col: (i, 0, col // w))
    conv_spec = pl.BlockSpec((DN_CONV, w), lambda i: (0, 0))
    row_spec = pl.BlockSpec((1, hd), lambda i: (0, 0))
    return pl.pallas_call(
        _dn_kernel,
        grid=(b,),
        in_specs=[
            seq_spec(COL_QA), seq_spec(COL_KA), seq_spec(COL_VA), seq_spec(COL_ZA),
            conv_spec, conv_spec, conv_spec,
            pl.BlockSpec((None, t, hd), lambda i: (i, 0, SMALL_W // hd - 1)),
            row_spec, row_spec, row_spec,
        ],
        out_specs=pl.BlockSpec((None, t, w), lambda i: (i, 0, 0)),
        out_shape=jax.ShapeDtypeStruct((b, t, w), BF16),
        scratch_shapes=[
            pltpu.VMEM((t, w), F32),
            pltpu.VMEM((t, w), BF16),
            pltpu.VMEM((t, w), BF16),
            pltpu.VMEM((t, w), BF16),
            pltpu.VMEM((t, w), BF16),
            pltpu.VMEM((8 * t // DN_CHUNK, w), F32),
            pltpu.VMEM((HEADS, hd, hd), F32),
        ],
        compiler_params=_cparams(1),
        name="deltanet_branch",
    )(big3d, big3d, big3d, big3d, conv_q, conv_k, conv_v, small3d, alog_row, dtb_row, onorm_row)


def _attn_prep_kernel(q_ref, k_ref, v_ref, sm_ref, qc_ref, qs_ref, kc_ref, ks_ref,
                      icos_ref, isin_up_ref, isin_dn_ref,
                      qo_ref, ko_ref, vt_ref, qio_ref, kia_ref, kib_ref, wt_ref):
    hd = HEAD_DIM
    lane_r = lax.broadcasted_iota(jnp.int32, (hd, hd), 0)
    lane_c = lax.broadcasted_iota(jnp.int32, (hd, hd), 1)
    swap_halves = (lane_r == (lane_c + hd // 2) % hd).astype(BF16)

    def norm_rope(ref, cos_g, sin_g, out_ref, scale):
        for h in range(HEADS):
            sl = slice(h * hd, (h + 1) * hd)
            x16 = ref[:, sl]
            x = x16.astype(F32)
            inv = lax.rsqrt(jnp.mean(x * x, axis=-1, keepdims=True) + NORM_EPS) * scale
            out_ref[:, sl] = ((x * cos_g + _dot(x16, swap_halves) * sin_g) * inv).astype(out_ref.dtype)

    norm_rope(q_ref, qc_ref[...], qs_ref[...], qo_ref, hd ** -0.5 * LOG2_E)
    norm_rope(k_ref, kc_ref[...], ks_ref[...], ko_ref, 1.0)
    w = v_ref.shape[1]
    eye = (lax.broadcasted_iota(jnp.int32, (w, w), 0) == lax.broadcasted_iota(jnp.int32, (w, w), 1)).astype(BF16)
    vt_ref[...] = _nt_dot(eye, v_ref[...]).astype(vt_ref.dtype)

    sm = sm_ref[...]
    iq_w = HEADS * IDX_DIM
    half = IDX_DIM // 2
    iq = sm[:, :iq_w]
    iq_r = (iq * icos_ref[...] + pltpu.roll(iq, iq_w - half, axis=1) * isin_up_ref[...]
            + pltpu.roll(iq, half, axis=1) * isin_dn_ref[...])
    qio_ref[...] = iq_r.astype(qio_ref.dtype)
    last = sm[:, iq_w:]
    lw = last.shape[1]
    ik_r = (last * icos_ref[:, :lw] + pltpu.roll(last, lw - half, axis=1) * isin_up_ref[:, :lw]
            + pltpu.roll(last, half, axis=1) * isin_dn_ref[:, :lw])
    lane = lax.broadcasted_iota(jnp.int32, ik_r.shape, 1)
    ik_r = jnp.where(lane < IDX_DIM, ik_r, 0.0)
    kia_ref[...] = ik_r.astype(kia_ref.dtype)
    kib_ref[...] = pltpu.roll(ik_r, IDX_DIM, axis=1).astype(kib_ref.dtype)
    wt = last.T
    wt_ref[...] = wt[SM_IDXW:SM_IDXW + 8, :] * (HEADS ** -0.5 * IDX_DIM ** -0.5)


def _attn_prep(big3d, small3d, tabs):
    b, t, _ = big3d.shape
    w = BRANCH_W
    tt = PREP_TT
    iq_w = HEADS * IDX_DIM
    seq = lambda col: pl.BlockSpec((None, tt, w), lambda j, i, col=col: (i, j, col // w))
    tab = lambda width: pl.BlockSpec((tt, width), lambda j, i: (j, 0))
    return pl.pallas_call(
        _attn_prep_kernel,
        grid=(t // tt, b),
        in_specs=[
            seq(COL_QC), seq(COL_KC), seq(COL_VC),
            pl.BlockSpec((None, tt, SMALL_W), lambda j, i: (i, j, 0)),
            tab(HEAD_DIM), tab(HEAD_DIM), tab(HEAD_DIM), tab(HEAD_DIM), tab(iq_w), tab(iq_w), tab(iq_w),
        ],
        out_specs=[
            pl.BlockSpec((None, tt, w), lambda j, i: (i, j, 0)),
            pl.BlockSpec((None, tt, w), lambda j, i: (i, j, 0)),
            pl.BlockSpec((None, w, tt), lambda j, i: (i, 0, j)),
            pl.BlockSpec((None, tt, iq_w), lambda j, i: (i, j, 0)),
            pl.BlockSpec((None, tt, 2 * IDX_DIM), lambda j, i: (i, j, 0)),
            pl.BlockSpec((None, tt, 2 * IDX_DIM), lambda j, i: (i, j, 0)),
            pl.BlockSpec((None, 8, tt), lambda j, i: (i, 0, j)),
        ],
        out_shape=[
            jax.ShapeDtypeStruct((b, t, w), BF16),
            jax.ShapeDtypeStruct((b, t, w), BF16),
            jax.ShapeDtypeStruct((b, w, t), BF16),
            jax.ShapeDtypeStruct((b, t, iq_w), BF16),
            jax.ShapeDtypeStruct((b, t, 2 * IDX_DIM), BF16),
            jax.ShapeDtypeStruct((b, t, 2 * IDX_DIM), BF16),
            jax.ShapeDtypeStruct((b, 8, t), F32),
        ],
        compiler_params=_cparams(2),
        name="attn_prep",
    )(big3d, big3d, big3d, small3d, *tabs)


def _dsa_kernel(q_ref, k_ref, vt_ref, qi_ref, kia_ref, kib_ref, wt_ref, z_ref, y_ref, s_ref, acc_ref, *, topk):
    qb, kc = ATT_QB, ATT_KC
    blk_i = pl.program_id(1)
    n_kc = blk_i + 1
    q_pos = blk_i * qb + lax.broadcasted_iota(jnp.int32, (1, qb), 1)
    qi = qi_ref[...]
    wt = wt_ref[...]
    inf = jnp.inf

    def col_sum(x):
        return x.reshape(kc // 8, 8, qb).sum(axis=0)

    def total(x):
        return jnp.sum(x, axis=0, keepdims=True)

    def score_chunk(ci, carry, diagonal):
        vmax, vmin, min_pos, n_pos, n_nonneg = carry
        off = pl.multiple_of(ci * kc, kc)
        ka = kia_ref[pl.ds(off, kc), :]
        kb = kib_ref[pl.ds(off, kc), :]
        s = jnp.zeros((kc, qb), F32)
        for h in range(HEADS):
            kk = ka if h % 2 == 0 else kb
            qq = qi[:, (h // 2) * 2 * IDX_DIM:(h // 2 + 1) * 2 * IDX_DIM]
            s = s + jnp.maximum(_nt_dot(kk, qq), 0.0) * wt[h:h + 1, :]
        s = jnp.where(s == 0.0, 0.0, s)
        if diagonal:
            key_pos = off + lax.broadcasted_iota(jnp.int32, (kc, 1), 0)
            causal = key_pos <= q_pos
            sc = jnp.where(causal, s, -inf)
            s_hi = jnp.where(causal, s, inf)
        else:
            sc = s_hi = s
        s_ref[pl.ds(off, kc), :] = sc
        vmax = jnp.maximum(vmax, jnp.max(sc, axis=0, keepdims=True))
        vmin = jnp.minimum(vmin, jnp.min(s_hi, axis=0, keepdims=True))
        min_pos = jnp.minimum(min_pos, jnp.min(jnp.where(sc > 0.0, sc, inf), axis=0, keepdims=True))
        n_pos = n_pos + col_sum(jnp.where(sc > 0.0, 1.0, 0.0))
        n_nonneg = n_nonneg + col_sum(jnp.where(sc >= 0.0, 1.0, 0.0))
        return vmax, vmin, min_pos, n_pos, n_nonneg

    row_inf = jnp.full((1, qb), inf, F32)
    zeros8 = jnp.zeros((8, qb), F32)
    stats = lax.fori_loop(0, blk_i, functools.partial(score_chunk, diagonal=False),
                          (-row_inf, row_inf, row_inf, zeros8, zeros8))
    vmax, vmin, min_pos, n_pos, n_nonneg = score_chunk(blk_i, stats, diagonal=True)
    n_pos = total(n_pos)
    n_nonneg = total(n_nonneg)

    def count_ge(thr):
        def body(ci, acc):
            off = pl.multiple_of(ci * kc, kc)
            hit = jnp.where(s_ref[pl.ds(off, kc), :] >= thr, 1.0, 0.0)
            return acc + hit.reshape(kc // (8 * COUNT_LANES), COUNT_LANES * 8, qb).sum(axis=0)
        return total(lax.fori_loop(0, n_kc, body, jnp.zeros((COUNT_LANES * 8, qb), F32)))

    k_sel = jnp.minimum(q_pos + 1, topk).astype(F32)
    n_causal = (q_pos + 1).astype(F32)
    at_zero = (n_pos < k_sel) & (k_sel <= n_nonneg)
    above = k_sel <= n_pos
    lo = jnp.where(at_zero, 0.0, jnp.where(above, min_pos, vmin))
    hi = jnp.where(at_zero, min_pos, jnp.where(above, vmax + (jnp.abs(vmax) + 1.0), 0.0))
    c_lo = jnp.where(at_zero, n_nonneg, jnp.where(above, n_pos, n_causal))
    c_hi = jnp.where(at_zero, n_pos, jnp.where(above, 0.0, n_nonneg))
    done = jnp.where(at_zero | (c_lo == k_sel), 1.0, 0.0)

    def n_open(d):
        return jnp.sum(1.0 - d)

    def bisect_cond(carry):
        it, n_left = carry[0], carry[1]
        return (it < NUM_BISECT) & (n_left > 0.0)

    def bisect_body(carry):
        it, _, lo, hi, c_lo, c_hi, done = carry
        for _ in range(BISECT_UNROLL):
            mid = 0.5 * lo + 0.5 * hi
            cnt = count_ge(mid)
            live = done < 0.5
            up = (cnt >= k_sel) & live
            dn = (cnt < k_sel) & live
            lo = jnp.where(up, mid, lo)
            c_lo = jnp.where(up, cnt, c_lo)
            hi = jnp.where(dn, mid, hi)
            c_hi = jnp.where(dn, cnt, c_hi)
            done = jnp.where(c_lo == k_sel, 1.0, done)
        return it + BISECT_UNROLL, n_open(done), lo, hi, c_lo, c_hi, done

    _, _, lo, hi, c_lo, c_hi, done = lax.while_loop(
        bisect_cond, bisect_body, (jnp.int32(0), n_open(done), lo, hi, c_lo, c_hi, done))
    need = k_sel - c_hi
    n_tied = jnp.sum(jnp.where(c_lo - c_hi > need, 1.0, 0.0))

    def tie_mask():
        tri = (lax.broadcasted_iota(jnp.int32, (kc, kc), 1)
               < lax.broadcasted_iota(jnp.int32, (kc, kc), 0)).astype(BF16)

        def mask_body(ci, seen):
            off = pl.multiple_of(ci * kc, kc)
            blk = s_ref[pl.ds(off, kc), :]
            tie = jnp.where((blk >= lo) & (blk < hi), 1.0, 0.0)
            rank = _dot(tri, tie.astype(BF16)) + seen
            sel = (blk >= hi) | ((tie > 0.5) & (rank < need))
            s_ref[pl.ds(off, kc), :] = jnp.where(sel, 1.0, 0.0)
            return seen + total(col_sum(tie))

        lax.fori_loop(0, n_kc, mask_body, jnp.zeros((1, qb), F32))

    pl.when(n_tied > 0.0)(tie_mask)
    sel_thr = jnp.where(n_tied > 0.0, 0.5, lo)

    hs = range(HEADS)
    sls = [slice(h * HEAD_DIM, (h + 1) * HEAD_DIM) for h in hs]
    qh = [q_ref[:, sl] for sl in sls]
    acc_ref[...] = jnp.zeros(acc_ref.shape, F32)

    def att_body(ci, carry):
        ms, ls = carry
        off = pl.multiple_of(ci * kc, kc)
        sel = s_ref[pl.ds(off, kc), :] >= sel_thr
        lm = [jnp.where(sel, _nt_dot(k_ref[pl.ds(off, kc), sls[h]], qh[h]), MASK_NEG) for h in hs]
        m_new = [jnp.maximum(ms[h], jnp.max(lm[h], axis=0, keepdims=True)) for h in hs]
        p = [jnp.exp2(lm[h] - m_new[h]) for h in hs]
        alpha = [jnp.exp2(ms[h] - m_new[h]) for h in hs]
        l_new = [alpha[h] * ls[h] + jnp.sum(p[h], axis=0, keepdims=True) for h in hs]
        pv = [_dot(vt_ref[sls[h], pl.ds(off, kc)], p[h].astype(BF16)) for h in hs]
        for h in hs:
            acc_ref[sls[h], :] = alpha[h] * acc_ref[sls[h], :] + pv[h]
        return tuple(m_new), tuple(l_new)

    row_neg = jnp.full((1, qb), MASK_NEG, F32)
    row_zero = jnp.zeros((1, qb), F32)
    _, ls = lax.fori_loop(0, n_kc, att_body, ((row_neg,) * HEADS, (row_zero,) * HEADS))
    for h in hs:
        o = (acc_ref[sls[h], :] / ls[h]).T
        y_ref[:, sls[h]] = (o * _silu(z_ref[:, sls[h]].astype(F32))).astype(y_ref.dtype)


def _dsa_branch(big3d, q_r, k_r, v_t, qi_r, ki_a, ki_b, w_t):
    b, t, _ = big3d.shape
    w = BRANCH_W
    qb = ATT_QB
    topk = min(TOPK_MAX, t // 4)
    return pl.pallas_call(
        functools.partial(_dsa_kernel, topk=topk),
        grid=(b, t // qb),
        in_specs=[
            pl.BlockSpec((None, qb, w), lambda i, j: (i, j, 0)),
            pl.BlockSpec((None, t, w), lambda i, j: (i, 0, 0)),
            pl.BlockSpec((None, w, t), lambda i, j: (i, 0, 0)),
            pl.BlockSpec((None, qb, HEADS * IDX_DIM), lambda i, j: (i, j, 0)),
            pl.BlockSpec((None, t, 2 * IDX_DIM), lambda i, j: (i, 0, 0)),
            pl.BlockSpec((None, t, 2 * IDX_DIM), lambda i, j: (i, 0, 0)),
            pl.BlockSpec((None, 8, qb), lambda i, j: (i, 0, j)),
            pl.BlockSpec((None, qb, w), lambda i, j: (i, j, COL_ZC // w)),
        ],
        out_specs=pl.BlockSpec((None, qb, w), lambda i, j: (i, j, 0)),
        out_shape=jax.ShapeDtypeStruct((b, t, w), BF16),
        scratch_shapes=[pltpu.VMEM((t, qb), F32),
                        pltpu.VMEM((w, qb), F32)],
        compiler_params=_cparams(2),
        name="sparse_attention",
    )(q_r, k_r, v_t, qi_r, ki_a, ki_b, w_t, big3d)


def _merge_kernel(ya_ref, yb_ref, yc_ref, ga_ref, gb_ref, gc_ref, bias_ref, wb_ref, wo_ref, x_ref, o_ref):
    merged = None
    for n, (y_ref, g_ref) in enumerate(((ya_ref, ga_ref), (yb_ref, gb_ref), (yc_ref, gc_ref))):
        proj = _dot(y_ref[...], wb_ref[n])
        gate = jax.nn.sigmoid(g_ref[...].astype(F32) + bias_ref[n:n + 1, :])
        term = gate * proj
        merged = term if merged is None else merged + term
    o_ref[...] = x_ref[...] + _dot(merged.astype(BF16), wo_ref[...])


def _merge(ya, yb, yc, big2d, gate_b, w_branch, w_out, x2d):
    m = x2d.shape[0]
    tm = MERGE_TM
    yspec = pl.BlockSpec((tm, BRANCH_W), lambda i: (i, 0))
    gspec = lambda n: pl.BlockSpec((tm, D_MODEL), lambda i, n=n: (i, COL_GATES // D_MODEL + n))
    return pl.pallas_call(
        _merge_kernel,
        grid=(m // tm,),
        in_specs=[
            yspec, yspec, yspec, gspec(0), gspec(1), gspec(2),
            pl.BlockSpec((N_BRANCH, D_MODEL), lambda i: (0, 0)),
            pl.BlockSpec((N_BRANCH, BRANCH_W, D_MODEL), lambda i: (0, 0, 0)),
            pl.BlockSpec((D_MODEL, D_MODEL), lambda i: (0, 0)),
            pl.BlockSpec((tm, D_MODEL), lambda i: (i, 0)),
        ],
        out_specs=pl.BlockSpec((tm, D_MODEL), lambda i: (i, 0)),
        out_shape=jax.ShapeDtypeStruct((m, D_MODEL), F32),
        compiler_params=_cparams(1),
        name="merge",
    )(ya, yb, yc, big2d, big2d, big2d, gate_b, w_branch, w_out, x2d)


_W_IN_SEGMENTS = (
    (0, COL_QA, 2048),
    (2056, COL_UB, 1024),
    (3080, COL_QC, 2048),
    (5452, COL_GATES, N_BRANCH * D_MODEL),
    (5128, BIG_W, HEADS * IDX_DIM),
    (5384, BIG_W + HEADS * IDX_DIM, IDX_DIM),
    (2048, BIG_W + HEADS * IDX_DIM + SM_BETA, 8),
    (5448, BIG_W + HEADS * IDX_DIM + SM_IDXW, HEADS),
)
_W_IN_USED = BIG_W + HEADS * IDX_DIM + SM_IDXW + HEADS
PERM_ROWS = 128


def _permute_kernel(w_ref, o_ref):
    for src, dst, n in _W_IN_SEGMENTS:
        o_ref[:, dst:dst + n] = w_ref[:, src:src + n].astype(o_ref.dtype)
    pad = BIG_W + SMALL_W - _W_IN_USED
    o_ref[:, _W_IN_USED:] = jnp.zeros((o_ref.shape[0], pad), o_ref.dtype)


def _permute_w_in(w_in):
    depth, d, in_width = w_in.shape
    return pl.pallas_call(
        _permute_kernel,
        grid=(depth, d // PERM_ROWS),
        in_specs=[pl.BlockSpec((None, PERM_ROWS, in_width), lambda l, i: (l, i, 0))],
        out_specs=pl.BlockSpec((None, PERM_ROWS, BIG_W + SMALL_W), lambda l, i: (l, i, 0)),
        out_shape=jax.ShapeDtypeStruct((depth, d, BIG_W + SMALL_W), BF16),
        compiler_params=_cparams(2),
        name="permute_w_in",
    )(w_in)


def _rope_tables(t):
    def base(dim):
        inv_freq = ROPE_THETA ** (-jnp.arange(0, dim, 2, dtype=F32) / dim)
        ang = jnp.arange(t, dtype=F32)[:, None] * inv_freq[None, :]
        return jnp.cos(ang), jnp.sin(ang)

    cos_a, sin_a = base(HEAD_DIM)
    cos = jnp.concatenate([cos_a, cos_a], axis=-1)
    sin = jnp.concatenate([-sin_a, sin_a], axis=-1)
    cos_i, sin_i = base(IDX_DIM)
    zero = jnp.zeros_like(sin_i)
    icos = jnp.tile(jnp.concatenate([cos_i, cos_i], axis=-1), (1, HEADS))
    isin_up = jnp.tile(jnp.concatenate([-sin_i, zero], axis=-1), (1, HEADS))
    isin_dn = jnp.tile(jnp.concatenate([zero, sin_i], axis=-1), (1, HEADS))
    return cos, sin, icos, isin_up, isin_dn


def _lane_row(vals, offset):
    d, n = vals.shape
    return jnp.zeros((d, 1, HEAD_DIM), F32).at[:, 0, offset:offset + n].set(vals.astype(F32))


def kernel(x, norm_g, w_in, gate_b, conv_w, a_log, dt_bias, dn_onorm, pool_w, pool_scale, q_norm, k_norm,
           w_branch, w_out):
    b, t, d = x.shape
    depth = norm_g.shape[0]
    w_p = _permute_w_in(w_in)
    wb16 = w_branch.astype(BF16)
    wo16 = w_out.astype(BF16)
    pw16 = pool_w.astype(BF16)
    alog_rows = _lane_row(a_log, SM_DECAY)
    dtb_rows = _lane_row(dt_bias, SM_DECAY)
    cos, sin, *idx_tabs = _rope_tables(t)
    idx_tabs = tuple(idx_tabs)
    x2d = x.reshape(b * t, d)
    for layer in range(depth):
        big2d, small2d = _in_projection(x2d, norm_g[layer][None, :], w_p, layer)
        big3d = big2d.reshape(b, t, BIG_W)
        small3d = small2d.reshape(b, t, SMALL_W)
        cw = conv_w[layer]
        ya = _deltanet_branch(big3d, small3d, cw[:, 0:BRANCH_W], cw[:, BRANCH_W:2 * BRANCH_W],
                              cw[:, 2 * BRANCH_W:], alog_rows[layer], dtb_rows[layer],
                              dn_onorm[layer][None, :])
        yb = _pool_branch(big3d, pw16[layer], pool_scale[layer][None, :])
        half = HEAD_DIM // 2
        gain_tabs = (cos * q_norm[layer], sin * jnp.roll(q_norm[layer], half),
                     cos * k_norm[layer], sin * jnp.roll(k_norm[layer], half))
        prep = _attn_prep(big3d, small3d, gain_tabs + idx_tabs)
        yc = _dsa_branch(big3d, *prep)
        x2d = _merge(ya.reshape(b * t, BRANCH_W), yb.reshape(b * t, BRANCH_W), yc.reshape(b * t, BRANCH_W),
                     big2d, gate_b[layer], wb16[layer], wo16[layer], x2d)
    return x2d.reshape(b, t, d)
```

```python
import functools

import numpy as np
import jax
import jax.numpy as jnp
from jax import lax
from jax.experimental import pallas as pl
from jax.experimental.pallas import tpu as pltpu

F32 = jnp.float32
BF16 = jnp.bfloat16
HIGHEST = lax.Precision.HIGHEST

D_MODEL = 1024
HEADS = 4
HEAD_DIM = 128
BRANCH_W = HEADS * HEAD_DIM
DN_CONV = 4
POOL_WINDOWS = (2, 4, 8, 16)
POOL_GROUP = 128
IDX_DIM = 64
TOPK_MAX = 256
ROPE_THETA = 10000.0
NORM_EPS = 1e-6
N_BRANCH = 3

COL_QA, COL_KA, COL_VA, COL_ZA = 0, 512, 1024, 1536
COL_UB, COL_ZB = 2048, 2560
COL_QC, COL_KC, COL_VC, COL_ZC = 3072, 3584, 4096, 4608
COL_GATES = 5120
BIG_W = 8192
SMALL_W = 384
SM_BETA, SM_DECAY, SM_IDXW = 64, 68, 72

V7X_VMEM_LIMIT = 56 * 1024 * 1024

PROJ_TM = 512
PROJ_TN = 512
MERGE_TM = 1024
PREP_TT = 512
POOL_TT = 256
POOL_HALO = 16
DN_CHUNK = 128
CONV_HALO = 8
CONV_ROWS = 128
DN_BASE = 8
DN_PAR = 2
ATT_QB = 256
ATT_KC = 256
NUM_BISECT = 36
BISECT_UNROLL = 4
COUNT_LANES = 4
MASK_NEG = -1e30
LOG2_E = 1.4426950408889634


def _nt_dot(a, b):
    return lax.dot_general(a, b, (((1,), (1,)), ((), ())), preferred_element_type=F32)


def _dot(a, b, precision=None):
    return jnp.dot(a, b, preferred_element_type=F32, precision=precision)


def _silu(x):
    return x * jax.nn.sigmoid(x)


def _cparams(n_axes):
    return pltpu.CompilerParams(dimension_semantics=("arbitrary",) * n_axes,
                                vmem_limit_bytes=V7X_VMEM_LIMIT)


def _inproj_kernel(x_ref, g_ref, w_ref, cw_ref, big_ref, small_ref, conv_buf, *, tiles_per_seq):
    step = pl.program_id(0)

    @pl.when(step == 0)
    def _():
        conv_buf[...] = jnp.zeros(conv_buf.shape, F32)

    x = x_ref[...]
    ms = jnp.mean(x * x, axis=-1, keepdims=True)
    h = ((x * lax.rsqrt(ms + NORM_EPS)) * g_ref[...]).astype(BF16)
    seq_start = (step % tiles_per_seq) == 0
    hd = HEAD_DIM
    halo = CONV_HALO

    def stage(seg):
        c = seg * PROJ_TN
        prev = jnp.where(seq_start, 0.0, conv_buf[seg, PROJ_TM:PROJ_TM + halo, :])
        conv_buf[seg, 0:halo, :] = prev
        conv_buf[seg, halo:PROJ_TM + halo, :] = _dot(h, w_ref[:, c:c + PROJ_TN])

    def conv_piece(seg, r, k, anchor):
        c = seg * PROJ_TN + k * hd
        cw = cw_ref[:, c:c + hd] + anchor
        rows = conv_buf[seg, r:r + CONV_ROWS + halo, k * hd:(k + 1) * hd]
        acc = rows * cw[DN_CONV - 1:DN_CONV, :]
        for s in range(1, DN_CONV):
            acc = acc + pltpu.roll(rows, s, axis=0) * cw[DN_CONV - 1 - s:DN_CONV - s, :]
        z = _silu(acc[halo:, :])
        if c < COL_VA:
            scale = hd ** -0.5 if c < COL_KA else 1.0
            z = z * (lax.rsqrt(jnp.sum(z * z, axis=-1, keepdims=True) + NORM_EPS) * scale)
        big_ref[r:r + CONV_ROWS, c:c + hd] = z.astype(big_ref.dtype)

    plain = list(range(COL_ZA, BIG_W, PROJ_TN))
    n_slot = 4
    for seg in range(COL_ZA // PROJ_TN):
        stage(seg)
        pieces = [(seg, r, k) for r in range(0, PROJ_TM, CONV_ROWS) for k in range(HEADS)]
        per_slot = -(-len(pieces) // n_slot)
        for _ in range(n_slot):
            c = plain.pop(0)
            y = _dot(h, w_ref[:, c:c + PROJ_TN])
            big_ref[:, c:c + PROJ_TN] = y.astype(big_ref.dtype)
            anchor = y[PROJ_TM - 1:PROJ_TM, 0:hd] * 0.0
            for piece in pieces[:per_slot]:
                conv_piece(*piece, anchor)
            pieces = pieces[per_slot:]
    for c in plain:
        big_ref[:, c:c + PROJ_TN] = _dot(h, w_ref[:, c:c + PROJ_TN]).astype(big_ref.dtype)
    small_ref[...] = _dot(h, w_ref[:, BIG_W:])


def _in_projection(x2d, g_row, w_p, conv_w, layer, tiles_per_seq):
    m = x2d.shape[0]
    return pl.pallas_call(
        functools.partial(_inproj_kernel, tiles_per_seq=tiles_per_seq),
        grid=(m // PROJ_TM,),
        in_specs=[
            pl.BlockSpec((PROJ_TM, D_MODEL), lambda i: (i, 0)),
            pl.BlockSpec((1, D_MODEL), lambda i: (0, 0)),
            pl.BlockSpec((None, D_MODEL, BIG_W + SMALL_W), lambda i: (layer, 0, 0), pipeline_mode=pl.Buffered(1)),
            pl.BlockSpec((DN_CONV, 3 * BRANCH_W), lambda i: (0, 0)),
        ],
        out_specs=[
            pl.BlockSpec((PROJ_TM, BIG_W), lambda i: (i, 0)),
            pl.BlockSpec((PROJ_TM, SMALL_W), lambda i: (i, 0)),
        ],
        out_shape=[
            jax.ShapeDtypeStruct((m, BIG_W), BF16),
            jax.ShapeDtypeStruct((m, SMALL_W), F32),
        ],
        scratch_shapes=[pltpu.VMEM((3, PROJ_TM + CONV_HALO, BRANCH_W), F32)],
        compiler_params=_cparams(1),
        name="in_projection",
    )(x2d, g_row, w_p, conv_w)


def _pool_kernel(u_ref, z_ref, pw_ref, ps_ref, y_ref):
    t_len = u_ref.shape[0]
    for t in range(t_len // POOL_TT):
        r0 = t * POOL_TT
        cur = u_ref[r0:r0 + POOL_TT, :].astype(F32)
        if t == 0:
            prev = jnp.zeros((POOL_HALO, cur.shape[1]), F32)
        else:
            prev = u_ref[r0 - POOL_HALO:r0, :].astype(F32)
        win_rows = jnp.concatenate([prev, cur], axis=0)
        pos = r0 + lax.broadcasted_iota(jnp.int32, (POOL_TT, 1), 0)
        for gi, win in enumerate(POOL_WINDOWS):
            sl = slice(gi * POOL_GROUP, (gi + 1) * POOL_GROUP)
            s = win_rows[:, sl]
            shift = 1
            while shift < win:
                s = s + pltpu.roll(s, shift, axis=0)
                shift *= 2
            count = jnp.minimum(pos + 1, win).astype(F32)
            pooled = s[POOL_HALO:, :] / count - cur[:, sl]
            mixed = _dot(pooled.astype(BF16), pw_ref[gi])
            zg = z_ref[r0:r0 + POOL_TT, sl].astype(F32)
            y_ref[r0:r0 + POOL_TT, sl] = (mixed * ps_ref[:, sl] * _silu(zg)).astype(y_ref.dtype)


def _pool_branch(big3d, pool_w, pool_scale_row):
    b, t, _ = big3d.shape
    width = POOL_GROUP * len(POOL_WINDOWS)
    return pl.pallas_call(
        _pool_kernel,
        grid=(b,),
        in_specs=[
            pl.BlockSpec((None, t, width), lambda i: (i, 0, COL_UB // width)),
            pl.BlockSpec((None, t, width), lambda i: (i, 0, COL_ZB // width)),
            pl.BlockSpec((len(POOL_WINDOWS), POOL_GROUP, POOL_GROUP), lambda i: (0, 0, 0)),
            pl.BlockSpec((1, width), lambda i: (0, 0)),
        ],
        out_specs=pl.BlockSpec((None, t, width), lambda i: (i, 0, 0)),
        out_shape=jax.ShapeDtypeStruct((b, t, width), BF16),
        compiler_params=_cparams(1),
        name="pool_branch",
    )(big3d, big3d, pool_w, pool_scale_row)


def _softplus(x):
    return jnp.maximum(x, 0.0) + jnp.log1p(jnp.exp(-jnp.abs(x)))


def _split_bf16(a):
    hi = a.astype(BF16)
    lo = (a - hi.astype(F32)).astype(BF16)
    return hi, lo


def _dot_split(a, b):
    a_hi, a_lo = a
    b_hi, b_lo = b
    return _dot(jnp.concatenate([a_hi, a_lo, a_hi], axis=1), jnp.concatenate([b_hi, b_hi, b_lo], axis=0))


def _dn_kernel(q_ref, k_ref, v_ref, z_ref, sm_ref, alog_ref, dtb_ref, on_ref,
               y_ref, u_s, w_s, qg_s, a_s, kdt_s, el_s, st_s):
    t_len = q_ref.shape[0]
    n_chunks = t_len // DN_CHUNK
    c = DN_CHUNK
    hd = HEAD_DIM
    row = lax.broadcasted_iota(jnp.int32, (c, c), 0)
    col = lax.broadcasted_iota(jnp.int32, (c, c), 1)
    tril = row >= col
    strict = row > col
    tril16 = tril.astype(BF16)
    eye_f = (row == col).astype(F32)
    base_blk = (row // DN_BASE) == (col // DN_BASE)
    pair_blks = []
    size = DN_BASE
    while size < c:
        pair_blks.append(((row // (2 * size)) == (col // (2 * size))) & ((row // size) != (col // size)))
        size *= 2

    def prepare(gi, carry):
        cis = [gi * DN_PAR + j for j in range(DN_PAR)]
        starts = [pl.multiple_of(ci * c, c) for ci in cis]
        beta_all, gc_all = [], []
        for start in starts:
            sm = sm_ref[pl.ds(start, c), :]
            beta_all.append(jax.nn.sigmoid(sm))
            g_all = -jnp.exp(alog_ref[...]) * _softplus(sm + dtb_ref[...])
            g_hi = g_all.astype(BF16)
            g_r = g_all - g_hi.astype(F32)
            g_mid = g_r.astype(BF16)
            g_lo = (g_r - g_mid.astype(F32)).astype(BF16)
            gc_all.append(_dot(tril16, g_hi) + (_dot(tril16, g_mid) + _dot(tril16, g_lo)))
        items = [(j, h) for j in range(DN_PAR) for h in range(HEADS)]
        ids = range(len(items))
        sls = [slice(h * hd, (h + 1) * hd) for _, h in items]
        qn = [q_ref[pl.ds(starts[j], c), sls[i]].astype(F32) for i, (j, _) in enumerate(items)]
        kn = [k_ref[pl.ds(starts[j], c), sls[i]].astype(F32) for i, (j, _) in enumerate(items)]
        beta_b = [jnp.broadcast_to(beta_all[j][:, SM_BETA + h:SM_BETA + h + 1], (c, hd)) for j, h in items]
        gc = [jnp.broadcast_to(gc_all[j][:, SM_DECAY + h:SM_DECAY + h + 1], (c, hd)) for j, h in items]
        decay = [jnp.exp(jnp.where(tril, gc[i] - gc[i].T, -jnp.inf)) for i in ids]
        kb = [kn[i] * beta_b[i] for i in ids]
        kn16 = [kn[i].astype(BF16) for i in ids]
        kq = [_nt_dot(jnp.concatenate([kb[i].astype(BF16), qn[i].astype(BF16)], axis=0), kn16[i]) for i in ids]
        lmat = [jnp.where(strict, kq[i][:c, :] * decay[i], 0.0) for i in ids]
        l_hi = [lmat[i].astype(BF16) for i in ids]
        l_lo = [(lmat[i] - l_hi[i].astype(F32)).astype(BF16) for i in ids]
        zero16 = jnp.zeros((c, c), BF16)
        d_parts = [(jnp.where(base_blk, l_hi[i], zero16), jnp.where(base_blk, l_lo[i], zero16)) for i in ids]
        tmat = [eye_f - jnp.where(base_blk, lmat[i], 0.0) for i in ids]
        power = [_dot_split(d_parts[i], d_parts[i]) for i in ids]
        span = 2
        while span < DN_BASE:
            parts = [_split_bf16(power[i]) for i in ids]
            t_parts = [_split_bf16(tmat[i]) for i in ids]
            span *= 2
            if span < DN_BASE:
                both = [_dot_split((jnp.concatenate([t_parts[i][0], parts[i][0]], axis=0),
                                    jnp.concatenate([t_parts[i][1], parts[i][1]], axis=0)), parts[i]) for i in ids]
                tmat = [tmat[i] + both[i][:c, :] for i in ids]
                power = [both[i][c:, :] for i in ids]
            else:
                tmat = [tmat[i] + _dot_split(t_parts[i], parts[i]) for i in ids]
        for pair_blk in pair_blks:
            t_parts = [_split_bf16(tmat[i]) for i in ids]
            off = [(jnp.where(pair_blk, l_hi[i], zero16), jnp.where(pair_blk, l_lo[i], zero16)) for i in ids]
            cx = [_dot_split(off[i], t_parts[i]) for i in ids]
            tmat = [tmat[i] - _dot_split(t_parts[i], _split_bf16(cx[i])) for i in ids]
        for i, (j, h) in enumerate(items):
            sl = sls[i]
            start = starts[j]
            t16 = tmat[i].astype(BF16)
            egc = jnp.exp(gc[i])
            vb = v_ref[pl.ds(start, c), sl].astype(F32) * beta_b[i]
            uw = _dot(t16, jnp.concatenate([vb.astype(BF16), (kb[i] * egc).astype(BF16)], axis=1))
            u_s[pl.ds(start, c), sl] = uw[:, :hd]
            w_s[pl.ds(start, c), sl] = uw[:, hd:].astype(BF16)
            qg_s[pl.ds(start, c), sl] = (qn[i] * egc).astype(BF16)
            a_s[pl.ds(start, c), sl] = jnp.where(tril, kq[i][c:, :] * decay[i], 0.0).astype(BF16)
            g_last = gc[i][c - 1:c, :]
            kd = kn[i] * jnp.exp(g_last - gc[i])
            kdt_s[pl.ds(start, c), sl] = kd.T.astype(BF16)
            el_s[pl.ds(pl.multiple_of(cis[j] * 8, 8), 8), sl] = jnp.broadcast_to(jnp.exp(g_last), (8, hd))
        return carry

    lax.fori_loop(0, n_chunks // DN_PAR, prepare, 0)

    st_s[...] = jnp.zeros(st_s.shape, F32)

    def scan(ci, carry):
        start = pl.multiple_of(ci * c, c)
        hs = range(HEADS)
        sls = [slice(h * hd, (h + 1) * hd) for h in hs]
        state = [st_s[h] for h in hs]
        s16 = [state[h].astype(BF16) for h in hs]
        v_new = [u_s[pl.ds(start, c), sls[h]] - _dot(w_s[pl.ds(start, c), sls[h]], s16[h]) for h in hs]
        v16 = [v_new[h].astype(BF16) for h in hs]
        for h in hs:
            e_last = el_s[pl.ds(pl.multiple_of(ci * 8, 8), 8), sls[h]][0:1, :]
            st_s[h] = state[h] * e_last + _dot(kdt_s[pl.ds(start, c), sls[h]], v16[h])
        for h in hs:
            sl = sls[h]
            o = _dot(qg_s[pl.ds(start, c), sl], s16[h]) + _dot(a_s[pl.ds(start, c), sl], v16[h])
            on = o * lax.rsqrt(jnp.mean(o * o, axis=-1, keepdims=True) + NORM_EPS) * on_ref[...]
            zg = z_ref[pl.ds(start, c), sl].astype(F32)
            y_ref[pl.ds(start, c), sl] = (on * _silu(zg)).astype(y_ref.dtype)
        return carry

    lax.fori_loop(0, n_chunks, scan, 0)


def _deltanet_branch(big3d, small3d, alog_row, dtb_row, onorm_row):
    b, t, _ = big3d.shape
    hd = HEAD_DIM
    w = BRANCH_W
    seq_spec = lambda col: pl.BlockSpec((None, t, w), lambda i, col=col: (i, 0, col // w))
    row_spec = pl.BlockSpec((1, hd), lambda i: (0, 0))
    return pl.pallas_call(
        _dn_kernel,
        grid=(b,),
        in_specs=[
            seq_spec(COL_QA), seq_spec(COL_KA), seq_spec(COL_VA), seq_spec(COL_ZA),
            pl.BlockSpec((None, t, hd), lambda i: (i, 0, SMALL_W // hd - 1)),
            row_spec, row_spec, row_spec,
        ],
        out_specs=pl.BlockSpec((None, t, w), lambda i: (i, 0, 0)),
        out_shape=jax.ShapeDtypeStruct((b, t, w), BF16),
        scratch_shapes=[
            pltpu.VMEM((t, w), F32),
            pltpu.VMEM((t, w), BF16),
            pltpu.VMEM((t, w), BF16),
            pltpu.VMEM((t, w), BF16),
            pltpu.VMEM((t, w), BF16),
            pltpu.VMEM((8 * t // DN_CHUNK, w), F32),
            pltpu.VMEM((HEADS, hd, hd), F32),
        ],
        compiler_params=_cparams(1),
        name="deltanet_branch",
    )(big3d, big3d, big3d, big3d, small3d, alog_row, dtb_row, onorm_row)


def _attn_prep_kernel(q_ref, k_ref, v_ref, sm_ref, qc_ref, qs_ref, kc_ref, ks_ref,
                      icos_ref, isin_up_ref, isin_dn_ref,
                      qo_ref, ko_ref, vt_ref, qio_ref, kia_ref, kib_ref, wt_ref):
    hd = HEAD_DIM
    lane_r = lax.broadcasted_iota(jnp.int32, (hd, hd), 0)
    lane_c = lax.broadcasted_iota(jnp.int32, (hd, hd), 1)
    swap_halves = (lane_r == (lane_c + hd // 2) % hd).astype(BF16)

    def norm_rope(ref, cos_g, sin_g, out_ref, scale):
        for h in range(HEADS):
            sl = slice(h * hd, (h + 1) * hd)
            x16 = ref[:, sl]
            x = x16.astype(F32)
            inv = lax.rsqrt(jnp.mean(x * x, axis=-1, keepdims=True) + NORM_EPS) * scale
            out_ref[:, sl] = ((x * cos_g + _dot(x16, swap_halves) * sin_g) * inv).astype(out_ref.dtype)

    norm_rope(q_ref, qc_ref[...], qs_ref[...], qo_ref, hd ** -0.5 * LOG2_E)
    norm_rope(k_ref, kc_ref[...], ks_ref[...], ko_ref, 1.0)
    w = v_ref.shape[1]
    eye = (lax.broadcasted_iota(jnp.int32, (w, w), 0) == lax.broadcasted_iota(jnp.int32, (w, w), 1)).astype(BF16)
    vt_ref[...] = _nt_dot(eye, v_ref[...]).astype(vt_ref.dtype)

    sm = sm_ref[...]
    iq_w = HEADS * IDX_DIM
    half = IDX_DIM // 2
    iq = sm[:, :iq_w]
    iq_r = (iq * icos_ref[...] + pltpu.roll(iq, iq_w - half, axis=1) * isin_up_ref[...]
            + pltpu.roll(iq, half, axis=1) * isin_dn_ref[...])
    qio_ref[...] = iq_r.astype(qio_ref.dtype)
    last = sm[:, iq_w:]
    lw = last.shape[1]
    ik_r = (last * icos_ref[:, :lw] + pltpu.roll(last, lw - half, axis=1) * isin_up_ref[:, :lw]
            + pltpu.roll(last, half, axis=1) * isin_dn_ref[:, :lw])
    lane = lax.broadcasted_iota(jnp.int32, ik_r.shape, 1)
    ik_r = jnp.where(lane < IDX_DIM, ik_r, 0.0)
    kia_ref[...] = ik_r.astype(kia_ref.dtype)
    kib_ref[...] = pltpu.roll(ik_r, IDX_DIM, axis=1).astype(kib_ref.dtype)
    wt = last.T
    wt_ref[...] = wt[SM_IDXW:SM_IDXW + 8, :] * (HEADS ** -0.5 * IDX_DIM ** -0.5)


def _attn_prep(big3d, small3d, tabs):
    b, t, _ = big3d.shape
    w = BRANCH_W
    tt = PREP_TT
    iq_w = HEADS * IDX_DIM
    seq = lambda col: pl.BlockSpec((None, tt, w), lambda j, i, col=col: (i, j, col // w))
    tab = lambda width: pl.BlockSpec((tt, width), lambda j, i: (j, 0))
    return pl.pallas_call(
        _attn_prep_kernel,
        grid=(t // tt, b),
        in_specs=[
            seq(COL_QC), seq(COL_KC), seq(COL_VC),
            pl.BlockSpec((None, tt, SMALL_W), lambda j, i: (i, j, 0)),
            tab(HEAD_DIM), tab(HEAD_DIM), tab(HEAD_DIM), tab(HEAD_DIM), tab(iq_w), tab(iq_w), tab(iq_w),
        ],
        out_specs=[
            pl.BlockSpec((None, tt, w), lambda j, i: (i, j, 0)),
            pl.BlockSpec((None, tt, w), lambda j, i: (i, j, 0)),
            pl.BlockSpec((None, w, tt), lambda j, i: (i, 0, j)),
            pl.BlockSpec((None, tt, iq_w), lambda j, i: (i, j, 0)),
            pl.BlockSpec((None, tt, 2 * IDX_DIM), lambda j, i: (i, j, 0)),
            pl.BlockSpec((None, tt, 2 * IDX_DIM), lambda j, i: (i, j, 0)),
            pl.BlockSpec((None, 8, tt), lambda j, i: (i, 0, j)),
        ],
        out_shape=[
            jax.ShapeDtypeStruct((b, t, w), BF16),
            jax.ShapeDtypeStruct((b, t, w), BF16),
            jax.ShapeDtypeStruct((b, w, t), BF16),
            jax.ShapeDtypeStruct((b, t, iq_w), BF16),
            jax.ShapeDtypeStruct((b, t, 2 * IDX_DIM), BF16),
            jax.ShapeDtypeStruct((b, t, 2 * IDX_DIM), BF16),
            jax.ShapeDtypeStruct((b, 8, t), F32),
        ],
        compiler_params=_cparams(2),
        name="attn_prep",
    )(big3d, big3d, big3d, small3d, *tabs)


def _dsa_kernel(q_ref, k_ref, vt_ref, qi_ref, kia_ref, kib_ref, wt_ref, z_ref, y_ref, s_ref, acc_ref, *, topk):
    qb, kc = ATT_QB, ATT_KC
    blk_i = pl.program_id(1)
    n_kc = blk_i + 1
    q_pos = blk_i * qb + lax.broadcasted_iota(jnp.int32, (1, qb), 1)
    qi = qi_ref[...]
    wt = wt_ref[...]
    inf = jnp.inf

    def col_sum(x):
        return x.reshape(kc // 8, 8, qb).sum(axis=0)

    def total(x):
        return jnp.sum(x, axis=0, keepdims=True)

    def score_chunk(ci, carry, diagonal):
        vmax, vmin, min_pos, n_pos, n_nonneg = carry
        off = pl.multiple_of(ci * kc, kc)
        ka = kia_ref[pl.ds(off, kc), :]
        kb = kib_ref[pl.ds(off, kc), :]
        s = jnp.zeros((kc, qb), F32)
        for h in range(HEADS):
            kk = ka if h % 2 == 0 else kb
            qq = qi[:, (h // 2) * 2 * IDX_DIM:(h // 2 + 1) * 2 * IDX_DIM]
            s = s + jnp.maximum(_nt_dot(kk, qq), 0.0) * wt[h:h + 1, :]
        s = jnp.where(s == 0.0, 0.0, s)
        if diagonal:
            key_pos = off + lax.broadcasted_iota(jnp.int32, (kc, 1), 0)
            causal = key_pos <= q_pos
            sc = jnp.where(causal, s, -inf)
            s_hi = jnp.where(causal, s, inf)
        else:
            sc = s_hi = s
        s_ref[pl.ds(off, kc), :] = sc
        vmax = jnp.maximum(vmax, jnp.max(sc, axis=0, keepdims=True))
        vmin = jnp.minimum(vmin, jnp.min(s_hi, axis=0, keepdims=True))
        min_pos = jnp.minimum(min_pos, jnp.min(jnp.where(sc > 0.0, sc, inf), axis=0, keepdims=True))
        n_pos = n_pos + col_sum(jnp.where(sc > 0.0, 1.0, 0.0))
        n_nonneg = n_nonneg + col_sum(jnp.where(sc >= 0.0, 1.0, 0.0))
        return vmax, vmin, min_pos, n_pos, n_nonneg

    row_inf = jnp.full((1, qb), inf, F32)
    zeros8 = jnp.zeros((8, qb), F32)
    stats = lax.fori_loop(0, blk_i, functools.partial(score_chunk, diagonal=False),
                          (-row_inf, row_inf, row_inf, zeros8, zeros8))
    vmax, vmin, min_pos, n_pos, n_nonneg = score_chunk(blk_i, stats, diagonal=True)
    n_pos = total(n_pos)
    n_nonneg = total(n_nonneg)

    def count_ge(thr):
        def body(ci, acc):
            off = pl.multiple_of(ci * kc, kc)
            hit = jnp.where(s_ref[pl.ds(off, kc), :] >= thr, 1.0, 0.0)
            return acc + hit.reshape(kc // (8 * COUNT_LANES), COUNT_LANES * 8, qb).sum(axis=0)
        return total(lax.fori_loop(0, n_kc, body, jnp.zeros((COUNT_LANES * 8, qb), F32)))

    k_sel = jnp.minimum(q_pos + 1, topk).astype(F32)
    n_causal = (q_pos + 1).astype(F32)
    at_zero = (n_pos < k_sel) & (k_sel <= n_nonneg)
    above = k_sel <= n_pos
    lo = jnp.where(at_zero, 0.0, jnp.where(above, min_pos, vmin))
    hi = jnp.where(at_zero, min_pos, jnp.where(above, vmax + (jnp.abs(vmax) + 1.0), 0.0))
    c_lo = jnp.where(at_zero, n_nonneg, jnp.where(above, n_pos, n_causal))
    c_hi = jnp.where(at_zero, n_pos, jnp.where(above, 0.0, n_nonneg))
    done = jnp.where(at_zero | (c_lo == k_sel), 1.0, 0.0)

    def n_open(d):
        return jnp.sum(1.0 - d)

    def bisect_cond(carry):
        it, n_left = carry[0], carry[1]
        return (it < NUM_BISECT) & (n_left > 0.0)

    def bisect_body(carry):
        it, _, lo, hi, c_lo, c_hi, done = carry
        for _ in range(BISECT_UNROLL):
            mid = 0.5 * lo + 0.5 * hi
            cnt = count_ge(mid)
            live = done < 0.5
            up = (cnt >= k_sel) & live
            dn = (cnt < k_sel) & live
            lo = jnp.where(up, mid, lo)
            c_lo = jnp.where(up, cnt, c_lo)
            hi = jnp.where(dn, mid, hi)
            c_hi = jnp.where(dn, cnt, c_hi)
            done = jnp.where(c_lo == k_sel, 1.0, done)
        return it + BISECT_UNROLL, n_open(done), lo, hi, c_lo, c_hi, done

    _, _, lo, hi, c_lo, c_hi, done = lax.while_loop(
        bisect_cond, bisect_body, (jnp.int32(0), n_open(done), lo, hi, c_lo, c_hi, done))
    need = k_sel - c_hi
    n_tied = jnp.sum(jnp.where(c_lo - c_hi > need, 1.0, 0.0))

    def tie_mask():
        tri = (lax.broadcasted_iota(jnp.int32, (kc, kc), 1)
               < lax.broadcasted_iota(jnp.int32, (kc, kc), 0)).astype(BF16)

        def mask_body(ci, seen):
            off = pl.multiple_of(ci * kc, kc)
            blk = s_ref[pl.ds(off, kc), :]
            tie = jnp.where((blk >= lo) & (blk < hi), 1.0, 0.0)
            rank = _dot(tri, tie.astype(BF16)) + seen
            sel = (blk >= hi) | ((tie > 0.5) & (rank < need))
            s_ref[pl.ds(off, kc), :] = jnp.where(sel, 1.0, 0.0)
            return seen + total(col_sum(tie))

        lax.fori_loop(0, n_kc, mask_body, jnp.zeros((1, qb), F32))

    pl.when(n_tied > 0.0)(tie_mask)
    sel_thr = jnp.where(n_tied > 0.0, 0.5, lo)

    hs = range(HEADS)
    sls = [slice(h * HEAD_DIM, (h + 1) * HEAD_DIM) for h in hs]
    qh = [q_ref[:, sl] for sl in sls]
    acc_ref[...] = jnp.zeros(acc_ref.shape, F32)

    def att_body(ci, carry):
        ms, ls = carry
        off = pl.multiple_of(ci * kc, kc)
        sel = s_ref[pl.ds(off, kc), :] >= sel_thr
        lm = [jnp.where(sel, _nt_dot(k_ref[pl.ds(off, kc), sls[h]], qh[h]), MASK_NEG) for h in hs]
        m_new = [jnp.maximum(ms[h], jnp.max(lm[h], axis=0, keepdims=True)) for h in hs]
        p = [jnp.exp2(lm[h] - m_new[h]) for h in hs]
        alpha = [jnp.exp2(ms[h] - m_new[h]) for h in hs]
        l_new = [alpha[h] * ls[h] + jnp.sum(p[h], axis=0, keepdims=True) for h in hs]
        pv = [_dot(vt_ref[sls[h], pl.ds(off, kc)], p[h].astype(BF16)) for h in hs]
        for h in hs:
            acc_ref[sls[h], :] = alpha[h] * acc_ref[sls[h], :] + pv[h]
        return tuple(m_new), tuple(l_new)

    row_neg = jnp.full((1, qb), MASK_NEG, F32)
    row_zero = jnp.zeros((1, qb), F32)
    _, ls = lax.fori_loop(0, n_kc, att_body, ((row_neg,) * HEADS, (row_zero,) * HEADS))
    for h in hs:
        o = (acc_ref[sls[h], :] / ls[h]).T
        y_ref[:, sls[h]] = (o * _silu(z_ref[:, sls[h]].astype(F32))).astype(y_ref.dtype)


def _dsa_branch(big3d, q_r, k_r, v_t, qi_r, ki_a, ki_b, w_t):
    b, t, _ = big3d.shape
    w = BRANCH_W
    qb = ATT_QB
    topk = min(TOPK_MAX, t // 4)
    return pl.pallas_call(
        functools.partial(_dsa_kernel, topk=topk),
        grid=(b, t // qb),
        in_specs=[
            pl.BlockSpec((None, qb, w), lambda i, j: (i, j, 0)),
            pl.BlockSpec((None, t, w), lambda i, j: (i, 0, 0)),
            pl.BlockSpec((None, w, t), lambda i, j: (i, 0, 0)),
            pl.BlockSpec((None, qb, HEADS * IDX_DIM), lambda i, j: (i, j, 0)),
            pl.BlockSpec((None, t, 2 * IDX_DIM), lambda i, j: (i, 0, 0)),
            pl.BlockSpec((None, t, 2 * IDX_DIM), lambda i, j: (i, 0, 0)),
            pl.BlockSpec((None, 8, qb), lambda i, j: (i, 0, j)),
            pl.BlockSpec((None, qb, w), lambda i, j: (i, j, COL_ZC // w)),
        ],
        out_specs=pl.BlockSpec((None, qb, w), lambda i, j: (i, j, 0)),
        out_shape=jax.ShapeDtypeStruct((b, t, w), BF16),
        scratch_shapes=[pltpu.VMEM((t, qb), F32),
                        pltpu.VMEM((w, qb), F32)],
        compiler_params=_cparams(2),
        name="sparse_attention",
    )(q_r, k_r, v_t, qi_r, ki_a, ki_b, w_t, big3d)


def _merge_kernel(ya_ref, yb_ref, yc_ref, ga_ref, gb_ref, gc_ref, bias_ref, wb_ref, wo_ref, x_ref, o_ref):
    merged = None
    for n, (y_ref, g_ref) in enumerate(((ya_ref, ga_ref), (yb_ref, gb_ref), (yc_ref, gc_ref))):
        proj = _dot(y_ref[...], wb_ref[n])
        gate = jax.nn.sigmoid(g_ref[...].astype(F32) + bias_ref[n:n + 1, :])
        term = gate * proj
        merged = term if merged is None else merged + term
    o_ref[...] = x_ref[...] + _dot(merged.astype(BF16), wo_ref[...])


def _merge(ya, yb, yc, big2d, gate_b, w_branch, w_out, x2d):
    m = x2d.shape[0]
    tm = MERGE_TM
    yspec = pl.BlockSpec((tm, BRANCH_W), lambda i: (i, 0))
    gspec = lambda n: pl.BlockSpec((tm, D_MODEL), lambda i, n=n: (i, COL_GATES // D_MODEL + n))
    return pl.pallas_call(
        _merge_kernel,
        grid=(m // tm,),
        in_specs=[
            yspec, yspec, yspec, gspec(0), gspec(1), gspec(2),
            pl.BlockSpec((N_BRANCH, D_MODEL), lambda i: (0, 0)),
            pl.BlockSpec((N_BRANCH, BRANCH_W, D_MODEL), lambda i: (0, 0, 0)),
            pl.BlockSpec((D_MODEL, D_MODEL), lambda i: (0, 0)),
            pl.BlockSpec((tm, D_MODEL), lambda i: (i, 0)),
        ],
        out_specs=pl.BlockSpec((tm, D_MODEL), lambda i: (i, 0)),
        out_shape=jax.ShapeDtypeStruct((m, D_MODEL), F32),
        compiler_params=_cparams(1),
        name="merge",
    )(ya, yb, yc, big2d, big2d, big2d, gate_b, w_branch, w_out, x2d)


_W_IN_SEGMENTS = (
    (0, COL_QA, 2048),
    (2056, COL_UB, 1024),
    (3080, COL_QC, 2048),
    (5452, COL_GATES, N_BRANCH * D_MODEL),
    (5128, BIG_W, HEADS * IDX_DIM),
    (5384, BIG_W + HEADS * IDX_DIM, IDX_DIM),
    (2048, BIG_W + HEADS * IDX_DIM + SM_BETA, 8),
    (5448, BIG_W + HEADS * IDX_DIM + SM_IDXW, HEADS),
)
_W_IN_USED = BIG_W + HEADS * IDX_DIM + SM_IDXW + HEADS
PERM_ROWS = 128


def _permute_kernel(w_ref, o_ref):
    for src, dst, n in _W_IN_SEGMENTS:
        o_ref[:, dst:dst + n] = w_ref[:, src:src + n].astype(o_ref.dtype)
    pad = BIG_W + SMALL_W - _W_IN_USED
    o_ref[:, _W_IN_USED:] = jnp.zeros((o_ref.shape[0], pad), o_ref.dtype)


def _permute_w_in(w_in):
    depth, d, in_width = w_in.shape
    return pl.pallas_call(
        _permute_kernel,
        grid=(depth, d // PERM_ROWS),
        in_specs=[pl.BlockSpec((None, PERM_ROWS, in_width), lambda l, i: (l, i, 0))],
        out_specs=pl.BlockSpec((None, PERM_ROWS, BIG_W + SMALL_W), lambda l, i: (l, i, 0)),
        out_shape=jax.ShapeDtypeStruct((depth, d, BIG_W + SMALL_W), BF16),
        compiler_params=_cparams(2),
        name="permute_w_in",
    )(w_in)


def _rope_tables(t):
    def base(dim):
        inv_freq = ROPE_THETA ** (-jnp.arange(0, dim, 2, dtype=F32) / dim)
        ang = jnp.arange(t, dtype=F32)[:, None] * inv_freq[None, :]
        return jnp.cos(ang), jnp.sin(ang)

    cos_a, sin_a = base(HEAD_DIM)
    cos = jnp.concatenate([cos_a, cos_a], axis=-1)
    sin = jnp.concatenate([-sin_a, sin_a], axis=-1)
    cos_i, sin_i = base(IDX_DIM)
    zero = jnp.zeros_like(sin_i)
    icos = jnp.tile(jnp.concatenate([cos_i, cos_i], axis=-1), (1, HEADS))
    isin_up = jnp.tile(jnp.concatenate([-sin_i, zero], axis=-1), (1, HEADS))
    isin_dn = jnp.tile(jnp.concatenate([zero, sin_i], axis=-1), (1, HEADS))
    return cos, sin, icos, isin_up, isin_dn


def _lane_row(vals, offset):
    d, n = vals.shape
    return jnp.zeros((d, 1, HEAD_DIM), F32).at[:, 0, offset:offset + n].set(vals.astype(F32))


def kernel(x, norm_g, w_in, gate_b, conv_w, a_log, dt_bias, dn_onorm, pool_w, pool_scale, q_norm, k_norm,
           w_branch, w_out):
    b, t, d = x.shape
    depth = norm_g.shape[0]
    w_p = _permute_w_in(w_in)
    wb16 = w_branch.astype(BF16)
    wo16 = w_out.astype(BF16)
    pw16 = pool_w.astype(BF16)
    alog_rows = _lane_row(a_log, SM_DECAY)
    dtb_rows = _lane_row(dt_bias, SM_DECAY)
    cos, sin, *idx_tabs = _rope_tables(t)
    idx_tabs = tuple(idx_tabs)
    x2d = x.reshape(b * t, d)
    for layer in range(depth):
        big2d, small2d = _in_projection(x2d, norm_g[layer][None, :], w_p, conv_w[layer], layer, t // PROJ_TM)
        big3d = big2d.reshape(b, t, BIG_W)
        small3d = small2d.reshape(b, t, SMALL_W)
        ya = _deltanet_branch(big3d, small3d, alog_rows[layer], dtb_rows[layer], dn_onorm[layer][None, :])
        yb = _pool_branch(big3d, pw16[layer], pool_scale[layer][None, :])
        half = HEAD_DIM // 2
        gain_tabs = (cos * q_norm[layer], sin * jnp.roll(q_norm[layer], half),
                     cos * k_norm[layer], sin * jnp.roll(k_norm[layer], half))
        prep = _attn_prep(big3d, small3d, gain_tabs + idx_tabs)
        yc = _dsa_branch(big3d, *prep)
        x2d = _merge(ya.reshape(b * t, BRANCH_W), yb.reshape(b * t, BRANCH_W), yc.reshape(b * t, BRANCH_W),
                     big2d, gate_b[layer], wb16[layer], wo16[layer], x2d)
    return x2d.reshape(b, t, d)
```

```python
import functools

import numpy as np
import jax
import jax.numpy as jnp
from jax import lax
from jax.experimental import pallas as pl
from jax.experimental.pallas import tpu as pltpu

F32 = jnp.float32
BF16 = jnp.bfloat16
HIGHEST = lax.Precision.HIGHEST

D_MODEL = 1024
HEADS = 4
HEAD_DIM = 128
BRANCH_W = HEADS * HEAD_DIM
DN_CONV = 4
POOL_WINDOWS = (2, 4, 8, 16)
POOL_GROUP = 128
IDX_DIM = 64
TOPK_MAX = 256
ROPE_THETA = 10000.0
NORM_EPS = 1e-6
N_BRANCH = 3

COL_QA, COL_KA, COL_VA, COL_ZA = 0, 512, 1024, 1536
COL_UB, COL_ZB = 2048, 2560
COL_QC, COL_KC, COL_VC, COL_ZC = 3072, 3584, 4096, 4608
COL_GATES = 5120
BIG_W = 8192
SMALL_W = 384
SM_BETA, SM_DECAY, SM_IDXW = 64, 68, 72

V7X_VMEM_LIMIT = 56 * 1024 * 1024

PROJ_TM = 512
PROJ_TN = 512
MERGE_TM = 1024
PREP_TT = 512
POOL_TT = 256
POOL_HALO = 16
DN_CHUNK = 128
CONV_HALO = 8
CONV_ROWS = 128
DN_BASE = 8
DN_PAR = 2
ATT_QB = 256
ATT_KC = 256
NUM_BISECT = 36
BISECT_UNROLL = 4
COUNT_LANES = 4
MASK_NEG = -1e30
LOG2_E = 1.4426950408889634


def _nt_dot(a, b):
    return lax.dot_general(a, b, (((1,), (1,)), ((), ())), preferred_element_type=F32)


def _dot(a, b, precision=None):
    return jnp.dot(a, b, preferred_element_type=F32, precision=precision)


def _silu(x):
    return x * jax.nn.sigmoid(x)


def _cparams(n_axes):
    return pltpu.CompilerParams(dimension_semantics=("arbitrary",) * n_axes,
                                vmem_limit_bytes=V7X_VMEM_LIMIT)


def _inproj_kernel(x_ref, g_ref, w_ref, cw_ref, big_ref, small_ref, conv_buf, *, tiles_per_seq):
    step = pl.program_id(0)

    @pl.when(step == 0)
    def _():
        conv_buf[...] = jnp.zeros(conv_buf.shape, F32)

    x = x_ref[...]
    ms = jnp.mean(x * x, axis=-1, keepdims=True)
    h = ((x * lax.rsqrt(ms + NORM_EPS)) * g_ref[...]).astype(BF16)
    seq_start = (step % tiles_per_seq) == 0
    hd = HEAD_DIM
    halo = CONV_HALO

    def stage(seg):
        c = seg * PROJ_TN
        prev = jnp.where(seq_start, 0.0, conv_buf[seg, PROJ_TM:PROJ_TM + halo, :])
        conv_buf[seg, 0:halo, :] = prev
        conv_buf[seg, halo:PROJ_TM + halo, :] = _dot(h, w_ref[:, c:c + PROJ_TN])

    def conv_piece(seg, r, k, anchor):
        c = seg * PROJ_TN + k * hd
        cw = cw_ref[:, c:c + hd] + anchor
        rows = conv_buf[seg, r:r + CONV_ROWS + halo, k * hd:(k + 1) * hd]
        acc = rows * cw[DN_CONV - 1:DN_CONV, :]
        for s in range(1, DN_CONV):
            acc = acc + pltpu.roll(rows, s, axis=0) * cw[DN_CONV - 1 - s:DN_CONV - s, :]
        z = _silu(acc[halo:, :])
        if c < COL_VA:
            scale = hd ** -0.5 if c < COL_KA else 1.0
            z = z * (lax.rsqrt(jnp.sum(z * z, axis=-1, keepdims=True) + NORM_EPS) * scale)
        big_ref[r:r + CONV_ROWS, c:c + hd] = z.astype(big_ref.dtype)

    plain = list(range(COL_ZA, BIG_W, PROJ_TN))
    n_slot = 4
    for seg in range(COL_ZA // PROJ_TN):
        stage(seg)
        pieces = [(seg, r, k) for r in range(0, PROJ_TM, CONV_ROWS) for k in range(HEADS)]
        per_slot = -(-len(pieces) // n_slot)
        for _ in range(n_slot):
            c = plain.pop(0)
            y = _dot(h, w_ref[:, c:c + PROJ_TN])
            big_ref[:, c:c + PROJ_TN] = y.astype(big_ref.dtype)
            anchor = y[PROJ_TM - 1:PROJ_TM, 0:hd] * 0.0
            for piece in pieces[:per_slot]:
                conv_piece(*piece, anchor)
            pieces = pieces[per_slot:]
    for c in plain:
        big_ref[:, c:c + PROJ_TN] = _dot(h, w_ref[:, c:c + PROJ_TN]).astype(big_ref.dtype)
    small_ref[...] = _dot(h, w_ref[:, BIG_W:])


def _in_projection(x2d, g_row, w_p, conv_w, layer, tiles_per_seq):
    m = x2d.shape[0]
    return pl.pallas_call(
        functools.partial(_inproj_kernel, tiles_per_seq=tiles_per_seq),
        grid=(m // PROJ_TM,),
        in_specs=[
            pl.BlockSpec((PROJ_TM, D_MODEL), lambda i: (i, 0)),
            pl.BlockSpec((1, D_MODEL), lambda i: (0, 0)),
            pl.BlockSpec((None, D_MODEL, BIG_W + SMALL_W), lambda i: (layer, 0, 0), pipeline_mode=pl.Buffered(1)),
            pl.BlockSpec((DN_CONV, 3 * BRANCH_W), lambda i: (0, 0)),
        ],
        out_specs=[
            pl.BlockSpec((PROJ_TM, BIG_W), lambda i: (i, 0)),
            pl.BlockSpec((PROJ_TM, SMALL_W), lambda i: (i, 0)),
        ],
        out_shape=[
            jax.ShapeDtypeStruct((m, BIG_W), BF16),
            jax.ShapeDtypeStruct((m, SMALL_W), F32),
        ],
        scratch_shapes=[pltpu.VMEM((3, PROJ_TM + CONV_HALO, BRANCH_W), F32)],
        compiler_params=_cparams(1),
        name="in_projection",
    )(x2d, g_row, w_p, conv_w)


def _pool_kernel(u_ref, z_ref, pw_ref, ps_ref, y_ref):
    t_len = u_ref.shape[0]
    for t in range(t_len // POOL_TT):
        r0 = t * POOL_TT
        cur = u_ref[r0:r0 + POOL_TT, :].astype(F32)
        if t == 0:
            prev = jnp.zeros((POOL_HALO, cur.shape[1]), F32)
        else:
            prev = u_ref[r0 - POOL_HALO:r0, :].astype(F32)
        win_rows = jnp.concatenate([prev, cur], axis=0)
        pos = r0 + lax.broadcasted_iota(jnp.int32, (POOL_TT, 1), 0)
        for gi, win in enumerate(POOL_WINDOWS):
            sl = slice(gi * POOL_GROUP, (gi + 1) * POOL_GROUP)
            s = win_rows[:, sl]
            shift = 1
            while shift < win:
                s = s + pltpu.roll(s, shift, axis=0)
                shift *= 2
            count = jnp.minimum(pos + 1, win).astype(F32)
            pooled = s[POOL_HALO:, :] / count - cur[:, sl]
            mixed = _dot(pooled.astype(BF16), pw_ref[gi])
            zg = z_ref[r0:r0 + POOL_TT, sl].astype(F32)
            y_ref[r0:r0 + POOL_TT, sl] = (mixed * ps_ref[:, sl] * _silu(zg)).astype(y_ref.dtype)


def _pool_branch(big3d, pool_w, pool_scale_row):
    b, t, _ = big3d.shape
    width = POOL_GROUP * len(POOL_WINDOWS)
    return pl.pallas_call(
        _pool_kernel,
        grid=(b,),
        in_specs=[
            pl.BlockSpec((None, t, width), lambda i: (i, 0, COL_UB // width)),
            pl.BlockSpec((None, t, width), lambda i: (i, 0, COL_ZB // width)),
            pl.BlockSpec((len(POOL_WINDOWS), POOL_GROUP, POOL_GROUP), lambda i: (0, 0, 0)),
            pl.BlockSpec((1, width), lambda i: (0, 0)),
        ],
        out_specs=pl.BlockSpec((None, t, width), lambda i: (i, 0, 0)),
        out_shape=jax.ShapeDtypeStruct((b, t, width), BF16),
        compiler_params=_cparams(1),
        name="pool_branch",
    )(big3d, big3d, pool_w, pool_scale_row)


def _softplus(x):
    return jnp.maximum(x, 0.0) + jnp.log1p(jnp.exp(-jnp.abs(x)))


def _split_bf16(a):
    hi = a.astype(BF16)
    lo = (a - hi.astype(F32)).astype(BF16)
    return hi, lo


def _dot_split(a, b):
    a_hi, a_lo = a
    b_hi, b_lo = b
    return _dot(jnp.concatenate([a_hi, a_lo, a_hi], axis=1), jnp.concatenate([b_hi, b_hi, b_lo], axis=0))


def _dn_kernel(q_ref, k_ref, v_ref, z_ref, sm_ref, alog_ref, dtb_ref, on_ref,
               y_ref, u_s, w_s, qg_s, a_s, kdt_s, el_s, st_s):
    t_len = q_ref.shape[0]
    n_chunks = t_len // DN_CHUNK
    c = DN_CHUNK
    hd = HEAD_DIM
    row = lax.broadcasted_iota(jnp.int32, (c, c), 0)
    col = lax.broadcasted_iota(jnp.int32, (c, c), 1)
    tril = row >= col
    strict = row > col
    tril16 = tril.astype(BF16)
    eye_f = (row == col).astype(F32)
    base_blk = (row // DN_BASE) == (col // DN_BASE)
    pair_blks = []
    size = DN_BASE
    while size < c:
        pair_blks.append(((row // (2 * size)) == (col // (2 * size))) & ((row // size) != (col // size)))
        size *= 2

    def prepare(gi, carry):
        cis = [gi * DN_PAR + j for j in range(DN_PAR)]
        starts = [pl.multiple_of(ci * c, c) for ci in cis]
        beta_all, gc_all = [], []
        for start in starts:
            sm = sm_ref[pl.ds(start, c), :]
            beta_all.append(jax.nn.sigmoid(sm))
            g_all = -jnp.exp(alog_ref[...]) * _softplus(sm + dtb_ref[...])
            g_hi = g_all.astype(BF16)
            g_r = g_all - g_hi.astype(F32)
            g_mid = g_r.astype(BF16)
            g_lo = (g_r - g_mid.astype(F32)).astype(BF16)
            gc_all.append(_dot(tril16, g_hi) + (_dot(tril16, g_mid) + _dot(tril16, g_lo)))
        items = [(j, h) for j in range(DN_PAR) for h in range(HEADS)]
        ids = range(len(items))
        sls = [slice(h * hd, (h + 1) * hd) for _, h in items]
        qn = [q_ref[pl.ds(starts[j], c), sls[i]].astype(F32) for i, (j, _) in enumerate(items)]
        kn = [k_ref[pl.ds(starts[j], c), sls[i]].astype(F32) for i, (j, _) in enumerate(items)]
        beta_b = [jnp.broadcast_to(beta_all[j][:, SM_BETA + h:SM_BETA + h + 1], (c, hd)) for j, h in items]
        gc = [jnp.broadcast_to(gc_all[j][:, SM_DECAY + h:SM_DECAY + h + 1], (c, hd)) for j, h in items]
        decay = [jnp.exp(jnp.where(tril, gc[i] - gc[i].T, -jnp.inf)) for i in ids]
        kb = [kn[i] * beta_b[i] for i in ids]
        kn16 = [kn[i].astype(BF16) for i in ids]
        kq = [_nt_dot(jnp.concatenate([kb[i].astype(BF16), qn[i].astype(BF16)], axis=0), kn16[i]) for i in ids]
        lmat = [jnp.where(strict, kq[i][:c, :] * decay[i], 0.0) for i in ids]
        l_hi = [lmat[i].astype(BF16) for i in ids]
        l_lo = [(lmat[i] - l_hi[i].astype(F32)).astype(BF16) for i in ids]
        zero16 = jnp.zeros((c, c), BF16)
        d_parts = [(jnp.where(base_blk, l_hi[i], zero16), jnp.where(base_blk, l_lo[i], zero16)) for i in ids]
        tmat = [eye_f - jnp.where(base_blk, lmat[i], 0.0) for i in ids]
        power = [_dot_split(d_parts[i], d_parts[i]) for i in ids]
        span = 2
        while span < DN_BASE:
            parts = [_split_bf16(power[i]) for i in ids]
            t_parts = [_split_bf16(tmat[i]) for i in ids]
            span *= 2
            if span < DN_BASE:
                both = [_dot_split((jnp.concatenate([t_parts[i][0], parts[i][0]], axis=0),
                                    jnp.concatenate([t_parts[i][1], parts[i][1]], axis=0)), parts[i]) for i in ids]
                tmat = [tmat[i] + both[i][:c, :] for i in ids]
                power = [both[i][c:, :] for i in ids]
            else:
                tmat = [tmat[i] + _dot_split(t_parts[i], parts[i]) for i in ids]
        for pair_blk in pair_blks:
            t_parts = [_split_bf16(tmat[i]) for i in ids]
            off = [(jnp.where(pair_blk, l_hi[i], zero16), jnp.where(pair_blk, l_lo[i], zero16)) for i in ids]
            cx = [_dot_split(off[i], t_parts[i]) for i in ids]
            tmat = [tmat[i] - _dot_split(t_parts[i], _split_bf16(cx[i])) for i in ids]
        for i, (j, h) in enumerate(items):
            sl = sls[i]
            start = starts[j]
            t16 = tmat[i].astype(BF16)
            egc = jnp.exp(gc[i])
            vb = v_ref[pl.ds(start, c), sl].astype(F32) * beta_b[i]
            uw = _dot(t16, jnp.concatenate([vb.astype(BF16), (kb[i] * egc).astype(BF16)], axis=1))
            u_s[pl.ds(start, c), sl] = uw[:, :hd]
            w_s[pl.ds(start, c), sl] = uw[:, hd:].astype(BF16)
            qg_s[pl.ds(start, c), sl] = (qn[i] * egc).astype(BF16)
            a_s[pl.ds(start, c), sl] = jnp.where(tril, kq[i][c:, :] * decay[i], 0.0).astype(BF16)
            g_last = gc[i][c - 1:c, :]
            kd = kn[i] * jnp.exp(g_last - gc[i])
            kdt_s[pl.ds(start, c), sl] = kd.T.astype(BF16)
            el_s[pl.ds(pl.multiple_of(cis[j] * 8, 8), 8), sl] = jnp.broadcast_to(jnp.exp(g_last), (8, hd))
        return carry

    lax.fori_loop(0, n_chunks // DN_PAR, prepare, 0)

    st_s[...] = jnp.zeros(st_s.shape, F32)

    def scan(ci, carry):
        start = pl.multiple_of(ci * c, c)
        hs = range(HEADS)
        sls = [slice(h * hd, (h + 1) * hd) for h in hs]
        state = [st_s[h] for h in hs]
        s16 = [state[h].astype(BF16) for h in hs]
        v_new = [u_s[pl.ds(start, c), sls[h]] - _dot(w_s[pl.ds(start, c), sls[h]], s16[h]) for h in hs]
        v16 = [v_new[h].astype(BF16) for h in hs]
        for h in hs:
            e_last = el_s[pl.ds(pl.multiple_of(ci * 8, 8), 8), sls[h]][0:1, :]
            st_s[h] = state[h] * e_last + _dot(kdt_s[pl.ds(start, c), sls[h]], v16[h])
        for h in hs:
            sl = sls[h]
            o = _dot(qg_s[pl.ds(start, c), sl], s16[h]) + _dot(a_s[pl.ds(start, c), sl], v16[h])
            on = o * lax.rsqrt(jnp.mean(o * o, axis=-1, keepdims=True) + NORM_EPS) * on_ref[...]
            zg = z_ref[pl.ds(start, c), sl].astype(F32)
            y_ref[pl.ds(start, c), sl] = (on * _silu(zg)).astype(y_ref.dtype)
        return carry

    lax.fori_loop(0, n_chunks, scan, 0)


def _deltanet_branch(big3d, small3d, alog_row, dtb_row, onorm_row):
    b, t, _ = big3d.shape
    hd = HEAD_DIM
    w = BRANCH_W
    seq_spec = lambda col: pl.BlockSpec((None, t, w), lambda i, col=col: (i, 0, col // w))
    row_spec = pl.BlockSpec((1, hd), lambda i: (0, 0))
    return pl.pallas_call(
        _dn_kernel,
        grid=(b,),
        in_specs=[
            seq_spec(COL_QA), seq_spec(COL_KA), seq_spec(COL_VA), seq_spec(COL_ZA),
            pl.BlockSpec((None, t, hd), lambda i: (i, 0, SMALL_W // hd - 1)),
            row_spec, row_spec, row_spec,
        ],
        out_specs=pl.BlockSpec((None, t, w), lambda i: (i, 0, 0)),
        out_shape=jax.ShapeDtypeStruct((b, t, w), BF16),
        scratch_shapes=[
            pltpu.VMEM((t, w), F32),
            pltpu.VMEM((t, w), BF16),
            pltpu.VMEM((t, w), BF16),
            pltpu.VMEM((t, w), BF16),
            pltpu.VMEM((t, w), BF16),
            pltpu.VMEM((8 * t // DN_CHUNK, w), F32),
            pltpu.VMEM((HEADS, hd, hd), F32),
        ],
        compiler_params=_cparams(1),
        name="deltanet_branch",
    )(big3d, big3d, big3d, big3d, small3d, alog_row, dtb_row, onorm_row)


def _attn_prep_kernel(q_ref, k_ref, v_ref, sm_ref, qc_ref, qs_ref, kc_ref, ks_ref,
                      icos_ref, isin_up_ref, isin_dn_ref,
                      qo_ref, ko_ref, vt_ref, qio_ref, kia_ref, kib_ref, wt_ref):
    hd = HEAD_DIM
    lane_r = lax.broadcasted_iota(jnp.int32, (hd, hd), 0)
    lane_c = lax.broadcasted_iota(jnp.int32, (hd, hd), 1)
    swap_halves = (lane_r == (lane_c + hd // 2) % hd).astype(BF16)

    def norm_rope(ref, cos_g, sin_g, out_ref, scale):
        for h in range(HEADS):
            sl = slice(h * hd, (h + 1) * hd)
            x16 = ref[:, sl]
            x = x16.astype(F32)
            inv = lax.rsqrt(jnp.mean(x * x, axis=-1, keepdims=True) + NORM_EPS) * scale
            out_ref[:, sl] = ((x * cos_g + _dot(x16, swap_halves) * sin_g) * inv).astype(out_ref.dtype)

    norm_rope(q_ref, qc_ref[...], qs_ref[...], qo_ref, hd ** -0.5 * LOG2_E)
    norm_rope(k_ref, kc_ref[...], ks_ref[...], ko_ref, 1.0)
    w = v_ref.shape[1]
    eye = (lax.broadcasted_iota(jnp.int32, (w, w), 0) == lax.broadcasted_iota(jnp.int32, (w, w), 1)).astype(BF16)
    vt_ref[...] = _nt_dot(eye, v_ref[...]).astype(vt_ref.dtype)

    sm = sm_ref[...]
    iq_w = HEADS * IDX_DIM
    half = IDX_DIM // 2
    iq = sm[:, :iq_w]
    iq_r = (iq * icos_ref[...] + pltpu.roll(iq, iq_w - half, axis=1) * isin_up_ref[...]
            + pltpu.roll(iq, half, axis=1) * isin_dn_ref[...])
    qio_ref[...] = iq_r.astype(qio_ref.dtype)
    last = sm[:, iq_w:]
    lw = last.shape[1]
    ik_r = (last * icos_ref[:, :lw] + pltpu.roll(last, lw - half, axis=1) * isin_up_ref[:, :lw]
            + pltpu.roll(last, half, axis=1) * isin_dn_ref[:, :lw])
    lane = lax.broadcasted_iota(jnp.int32, ik_r.shape, 1)
    ik_r = jnp.where(lane < IDX_DIM, ik_r, 0.0)
    kia_ref[...] = ik_r.astype(kia_ref.dtype)
    kib_ref[...] = pltpu.roll(ik_r, IDX_DIM, axis=1).astype(kib_ref.dtype)
    wt = last.T
    wt_ref[...] = wt[SM_IDXW:SM_IDXW + 8, :] * (HEADS ** -0.5 * IDX_DIM ** -0.5)


def _attn_prep(big3d, small3d, tabs):
    b, t, _ = big3d.shape
    w = BRANCH_W
    tt = PREP_TT
    iq_w = HEADS * IDX_DIM
    seq = lambda col: pl.BlockSpec((None, tt, w), lambda j, i, col=col: (i, j, col // w))
    tab = lambda width: pl.BlockSpec((tt, width), lambda j, i: (j, 0))
    return pl.pallas_call(
        _attn_prep_kernel,
        grid=(t // tt, b),
        in_specs=[
            seq(COL_QC), seq(COL_KC), seq(COL_VC),
            pl.BlockSpec((None, tt, SMALL_W), lambda j, i: (i, j, 0)),
            tab(HEAD_DIM), tab(HEAD_DIM), tab(HEAD_DIM), tab(HEAD_DIM), tab(iq_w), tab(iq_w), tab(iq_w),
        ],
        out_specs=[
            pl.BlockSpec((None, tt, w), lambda j, i: (i, j, 0)),
            pl.BlockSpec((None, tt, w), lambda j, i: (i, j, 0)),
            pl.BlockSpec((None, w, tt), lambda j, i: (i, 0, j)),
            pl.BlockSpec((None, tt, iq_w), lambda j, i: (i, j, 0)),
            pl.BlockSpec((None, tt, 2 * IDX_DIM), lambda j, i: (i, j, 0)),
            pl.BlockSpec((None, tt, 2 * IDX_DIM), lambda j, i: (i, j, 0)),
            pl.BlockSpec((None, 8, tt), lambda j, i: (i, 0, j)),
        ],
        out_shape=[
            jax.ShapeDtypeStruct((b, t, w), BF16),
            jax.ShapeDtypeStruct((b, t, w), BF16),
            jax.ShapeDtypeStruct((b, w, t), BF16),
            jax.ShapeDtypeStruct((b, t, iq_w), BF16),
            jax.ShapeDtypeStruct((b, t, 2 * IDX_DIM), BF16),
            jax.ShapeDtypeStruct((b, t, 2 * IDX_DIM), BF16),
            jax.ShapeDtypeStruct((b, 8, t), F32),
        ],
        compiler_params=_cparams(2),
        name="attn_prep",
    )(big3d, big3d, big3d, small3d, *tabs)


def _dsa_kernel(q_ref, k_ref, vt_ref, qi_ref, kia_ref, kib_ref, wt_ref, z_ref, y_ref, s_ref, acc_ref, *, topk):
    qb, kc = ATT_QB, ATT_KC
    blk_i = pl.program_id(1)
    n_kc = blk_i + 1
    q_pos = blk_i * qb + lax.broadcasted_iota(jnp.int32, (1, qb), 1)
    qi = qi_ref[...]
    wt = wt_ref[...]
    inf = jnp.inf

    def col_sum(x):
        return x.reshape(kc // 8, 8, qb).sum(axis=0)

    def total(x):
        return jnp.sum(x, axis=0, keepdims=True)

    def score_chunk(ci, carry, diagonal):
        vmax, vmin, min_pos, n_pos, n_nonneg = carry
        off = pl.multiple_of(ci * kc, kc)
        ka = kia_ref[pl.ds(off, kc), :]
        kb = kib_ref[pl.ds(off, kc), :]
        s = jnp.zeros((kc, qb), F32)
        for h in range(HEADS):
            kk = ka if h % 2 == 0 else kb
            qq = qi[:, (h // 2) * 2 * IDX_DIM:(h // 2 + 1) * 2 * IDX_DIM]
            s = s + jnp.maximum(_nt_dot(kk, qq), 0.0) * wt[h:h + 1, :]
        s = jnp.where(s == 0.0, 0.0, s)
        if diagonal:
            key_pos = off + lax.broadcasted_iota(jnp.int32, (kc, 1), 0)
            causal = key_pos <= q_pos
            sc = jnp.where(causal, s, -inf)
            s_hi = jnp.where(causal, s, inf)
        else:
            sc = s_hi = s
        s_ref[pl.ds(off, kc), :] = sc
        vmax = jnp.maximum(vmax, jnp.max(sc, axis=0, keepdims=True))
        vmin = jnp.minimum(vmin, jnp.min(s_hi, axis=0, keepdims=True))
        min_pos = jnp.minimum(min_pos, jnp.min(jnp.where(sc > 0.0, sc, inf), axis=0, keepdims=True))
        n_pos = n_pos + col_sum(jnp.where(sc > 0.0, 1.0, 0.0))
        n_nonneg = n_nonneg + col_sum(jnp.where(sc >= 0.0, 1.0, 0.0))
        return vmax, vmin, min_pos, n_pos, n_nonneg

    row_inf = jnp.full((1, qb), inf, F32)
    zeros8 = jnp.zeros((8, qb), F32)
    stats = lax.fori_loop(0, blk_i, functools.partial(score_chunk, diagonal=False),
                          (-row_inf, row_inf, row_inf, zeros8, zeros8))
    vmax, vmin, min_pos, n_pos, n_nonneg = score_chunk(blk_i, stats, diagonal=True)
    n_pos = total(n_pos)
    n_nonneg = total(n_nonneg)

    def count_ge(thr):
        def body(ci, acc):
            off = pl.multiple_of(ci * kc, kc)
            hit = jnp.where(s_ref[pl.ds(off, kc), :] >= thr, 1.0, 0.0)
            return acc + hit.reshape(kc // (8 * COUNT_LANES), COUNT_LANES * 8, qb).sum(axis=0)
        return total(lax.fori_loop(0, n_kc, body, jnp.zeros((COUNT_LANES * 8, qb), F32)))

    k_sel = jnp.minimum(q_pos + 1, topk).astype(F32)
    n_causal = (q_pos + 1).astype(F32)
    at_zero = (n_pos < k_sel) & (k_sel <= n_nonneg)
    above = k_sel <= n_pos
    lo = jnp.where(at_zero, 0.0, jnp.where(above, min_pos, vmin))
    hi = jnp.where(at_zero, min_pos, jnp.where(above, vmax + (jnp.abs(vmax) + 1.0), 0.0))
    c_lo = jnp.where(at_zero, n_nonneg, jnp.where(above, n_pos, n_causal))
    c_hi = jnp.where(at_zero, n_pos, jnp.where(above, 0.0, n_nonneg))
    done = jnp.where(at_zero | (c_lo == k_sel), 1.0, 0.0)

    def n_open(d):
        return jnp.sum(1.0 - d)

    def bisect_cond(carry):
        it, n_left = carry[0], carry[1]
        return (it < NUM_BISECT) & (n_left > 0.0)

    def bisect_body(carry):
        it, _, lo, hi, c_lo, c_hi, done = carry
        for _ in range(BISECT_UNROLL):
            mid = 0.5 * lo + 0.5 * hi
            cnt = count_ge(mid)
            live = done < 0.5
            up = (cnt >= k_sel) & live
            dn = (cnt < k_sel) & live
            lo = jnp.where(up, mid, lo)
            c_lo = jnp.where(up, cnt, c_lo)
            hi = jnp.where(dn, mid, hi)
            c_hi = jnp.where(dn, cnt, c_hi)
            done = jnp.where(c_lo == k_sel, 1.0, done)
        return it + BISECT_UNROLL, n_open(done), lo, hi, c_lo, c_hi, done

    _, _, lo, hi, c_lo, c_hi, done = lax.while_loop(
        bisect_cond, bisect_body, (jnp.int32(0), n_open(done), lo, hi, c_lo, c_hi, done))
    need = k_sel - c_hi
    n_tied = jnp.sum(jnp.where(c_lo - c_hi > need, 1.0, 0.0))

    def tie_mask():
        tri = (lax.broadcasted_iota(jnp.int32, (kc, kc), 1)
               < lax.broadcasted_iota(jnp.int32, (kc, kc), 0)).astype(BF16)

        def mask_body(ci, seen):
            off = pl.multiple_of(ci * kc, kc)
            blk = s_ref[pl.ds(off, kc), :]
            tie = jnp.where((blk >= lo) & (blk < hi), 1.0, 0.0)
            rank = _dot(tri, tie.astype(BF16)) + seen
            sel = (blk >= hi) | ((tie > 0.5) & (rank < need))
            s_ref[pl.ds(off, kc), :] = jnp.where(sel, 1.0, 0.0)
            return seen + total(col_sum(tie))

        lax.fori_loop(0, n_kc, mask_body, jnp.zeros((1, qb), F32))

    pl.when(n_tied > 0.0)(tie_mask)
    sel_thr = jnp.where(n_tied > 0.0, 0.5, lo)

    hs = range(HEADS)
    sls = [slice(h * HEAD_DIM, (h + 1) * HEAD_DIM) for h in hs]
    qh = [q_ref[:, sl] for sl in sls]
    acc_ref[...] = jnp.zeros(acc_ref.shape, F32)

    def att_body(ci, carry):
        ms, ls = carry
        off = pl.multiple_of(ci * kc, kc)
        sel = s_ref[pl.ds(off, kc), :] >= sel_thr
        lm = [jnp.where(sel, _nt_dot(k_ref[pl.ds(off, kc), sls[h]], qh[h]), MASK_NEG) for h in hs]
        m_new = [jnp.maximum(ms[h], jnp.max(lm[h], axis=0, keepdims=True)) for h in hs]
        p = [jnp.exp2(lm[h] - m_new[h]) for h in hs]
        alpha = [jnp.exp2(ms[h] - m_new[h]) for h in hs]
        l_new = [alpha[h] * ls[h] + jnp.sum(p[h], axis=0, keepdims=True) for h in hs]
        pv = [_dot(vt_ref[sls[h], pl.ds(off, kc)], p[h].astype(BF16)) for h in hs]
        for h in hs:
            acc_ref[sls[h], :] = alpha[h] * acc_ref[sls[h], :] + pv[h]
        return tuple(m_new), tuple(l_new)

    row_neg = jnp.full((1, qb), MASK_NEG, F32)
    row_zero = jnp.zeros((1, qb), F32)
    _, ls = lax.fori_loop(0, n_kc, att_body, ((row_neg,) * HEADS, (row_zero,) * HEADS))
    for h in hs:
        o = (acc_ref[sls[h], :] / ls[h]).T
        y_ref[:, sls[h]] = (o * _silu(z_ref[:, sls[h]].astype(F32))).astype(y_ref.dtype)


def _dsa_branch(big3d, q_r, k_r, v_t, qi_r, ki_a, ki_b, w_t):
    b, t, _ = big3d.shape
    w = BRANCH_W
    qb = ATT_QB
    topk = min(TOPK_MAX, t // 4)
    return pl.pallas_call(
        functools.partial(_dsa_kernel, topk=topk),
        grid=(b, t // qb),
        in_specs=[
            pl.BlockSpec((None, qb, w), lambda i, j: (i, j, 0)),
            pl.BlockSpec((None, t, w), lambda i, j: (i, 0, 0)),
            pl.BlockSpec((None, w, t), lambda i, j: (i, 0, 0)),
            pl.BlockSpec((None, qb, HEADS * IDX_DIM), lambda i, j: (i, j, 0)),
            pl.BlockSpec((None, t, 2 * IDX_DIM), lambda i, j: (i, 0, 0)),
            pl.BlockSpec((None, t, 2 * IDX_DIM), lambda i, j: (i, 0, 0)),
            pl.BlockSpec((None, 8, qb), lambda i, j: (i, 0, j)),
            pl.BlockSpec((None, qb, w), lambda i, j: (i, j, COL_ZC // w)),
        ],
        out_specs=pl.BlockSpec((None, qb, w), lambda i, j: (i, j, 0)),
        out_shape=jax.ShapeDtypeStruct((b, t, w), BF16),
        scratch_shapes=[pltpu.VMEM((t, qb), F32),
                        pltpu.VMEM((w, qb), F32)],
        compiler_params=_cparams(2),
        name="sparse_attention",
    )(q_r, k_r, v_t, qi_r, ki_a, ki_b, w_t, big3d)


def _merge_kernel(ya_ref, yb_ref, yc_ref, ga_ref, gb_ref, gc_ref, bias_ref, wb_ref, wo_ref, x_ref, o_ref):
    merged = None
    for n, (y_ref, g_ref) in enumerate(((ya_ref, ga_ref), (yb_ref, gb_ref), (yc_ref, gc_ref))):
        proj = _dot(y_ref[...], wb_ref[n])
        gate = jax.nn.sigmoid(g_ref[...].astype(F32) + bias_ref[n:n + 1, :])
        term = gate * proj
        merged = term if merged is None else merged + term
    o_ref[...] = x_ref[...] + _dot(merged.astype(BF16), wo_ref[...])


def _merge(ya, yb, yc, big2d, gate_b, w_branch, w_out, x2d):
    m = x2d.shape[0]
    tm = MERGE_TM
    yspec = pl.BlockSpec((tm, BRANCH_W), lambda i: (i, 0))
    gspec = lambda n: pl.BlockSpec((tm, D_MODEL), lambda i, n=n: (i, COL_GATES // D_MODEL + n))
    return pl.pallas_call(
        _merge_kernel,
        grid=(m // tm,),
        in_specs=[
            yspec, yspec, yspec, gspec(0), gspec(1), gspec(2),
            pl.BlockSpec((N_BRANCH, D_MODEL), lambda i: (0, 0)),
            pl.BlockSpec((N_BRANCH, BRANCH_W, D_MODEL), lambda i: (0, 0, 0)),
            pl.BlockSpec((D_MODEL, D_MODEL), lambda i: (0, 0)),
            pl.BlockSpec((tm, D_MODEL), lambda i: (i, 0)),
        ],
        out_specs=pl.BlockSpec((tm, D_MODEL), lambda i: (i, 0)),
        out_shape=jax.ShapeDtypeStruct((m, D_MODEL), F32),
        compiler_params=_cparams(1),
        name="merge",
    )(ya, yb, yc, big2d, big2d, big2d, gate_b, w_branch, w_out, x2d)


PERM_COLS = 128
_PERM_REGULAR = (BIG_W + HEADS * IDX_DIM) // PERM_COLS
_SRC_IDX_K, _SRC_BETA, _SRC_IDX_W = 5384, 2048, 5448


def _perm_source(blk):
    shift = jnp.where(blk < COL_UB // PERM_COLS, 0,
                      jnp.where(blk < COL_GATES // PERM_COLS, 2056 - COL_UB,
                                jnp.where(blk < BIG_W // PERM_COLS, 5452 - COL_GATES, 5128 - BIG_W)))
    return jnp.where(blk < _PERM_REGULAR, blk * PERM_COLS + shift, 0)


def _permute_kernel(w_ref, ik_ref, bd_ref, iw_ref, o_ref):
    blk = pl.program_id(0)
    depth = o_ref.shape[0]

    @pl.when(blk < _PERM_REGULAR)
    def _():
        for l in range(depth):
            o_ref[l] = w_ref[:, l, :].T.astype(o_ref.dtype)

    @pl.when(blk == _PERM_REGULAR)
    def _():
        row = lax.broadcasted_iota(jnp.int32, (8, o_ref.shape[1]), 0)
        for l in range(depth):
            idx_w = jnp.where(row < HEADS, iw_ref[:, l, :], 0.0)
            pad = jnp.zeros((PERM_COLS - IDX_DIM - 16, o_ref.shape[1]), F32)
            rows = jnp.concatenate([ik_ref[:, l, :], bd_ref[:, l, :], idx_w, pad], axis=0)
            o_ref[l] = rows.T.astype(o_ref.dtype)


def _permute_w_in(w_in):
    depth, d, _ = w_in.shape
    w_t = jnp.transpose(w_in, (2, 0, 1))
    el = pl.Element
    fixed = lambda rows, start: pl.BlockSpec((el(rows), el(depth), el(d)), lambda i: (start, 0, 0))
    return pl.pallas_call(
        _permute_kernel,
        grid=(_PERM_REGULAR + 1,),
        in_specs=[
            pl.BlockSpec((el(PERM_COLS), el(depth), el(d)), lambda i: (_perm_source(i), 0, 0)),
            fixed(IDX_DIM, _SRC_IDX_K), fixed(8, _SRC_BETA), fixed(8, _SRC_IDX_W),
        ],
        out_specs=pl.BlockSpec((depth, d, PERM_COLS), lambda i: (0, 0, i)),
        out_shape=jax.ShapeDtypeStruct((depth, d, BIG_W + SMALL_W), BF16),
        compiler_params=_cparams(1),
        name="permute_w_in",
    )(w_t, w_t, w_t, w_t)


def _rope_tables(t):
    def base(dim):
        inv_freq = ROPE_THETA ** (-jnp.arange(0, dim, 2, dtype=F32) / dim)
        ang = jnp.arange(t, dtype=F32)[:, None] * inv_freq[None, :]
        return jnp.cos(ang), jnp.sin(ang)

    cos_a, sin_a = base(HEAD_DIM)
    cos = jnp.concatenate([cos_a, cos_a], axis=-1)
    sin = jnp.concatenate([-sin_a, sin_a], axis=-1)
    cos_i, sin_i = base(IDX_DIM)
    zero = jnp.zeros_like(sin_i)
    icos = jnp.tile(jnp.concatenate([cos_i, cos_i], axis=-1), (1, HEADS))
    isin_up = jnp.tile(jnp.concatenate([-sin_i, zero], axis=-1), (1, HEADS))
    isin_dn = jnp.tile(jnp.concatenate([zero, sin_i], axis=-1), (1, HEADS))
    return cos, sin, icos, isin_up, isin_dn


def _lane_row(vals, offset):
    d, n = vals.shape
    return jnp.zeros((d, 1, HEAD_DIM), F32).at[:, 0, offset:offset + n].set(vals.astype(F32))


def kernel(x, norm_g, w_in, gate_b, conv_w, a_log, dt_bias, dn_onorm, pool_w, pool_scale, q_norm, k_norm,
           w_branch, w_out):
    b, t, d = x.shape
    depth = norm_g.shape[0]
    w_p = _permute_w_in(w_in)
    wb16 = w_branch.astype(BF16)
    wo16 = w_out.astype(BF16)
    pw16 = pool_w.astype(BF16)
    alog_rows = _lane_row(a_log, SM_DECAY)
    dtb_rows = _lane_row(dt_bias, SM_DECAY)
    cos, sin, *idx_tabs = _rope_tables(t)
    idx_tabs = tuple(idx_tabs)
    x2d = x.reshape(b * t, d)
    for layer in range(depth):
        big2d, small2d = _in_projection(x2d, norm_g[layer][None, :], w_p, conv_w[layer], layer, t // PROJ_TM)
        big3d = big2d.reshape(b, t, BIG_W)
        small3d = small2d.reshape(b, t, SMALL_W)
        ya = _deltanet_branch(big3d, small3d, alog_rows[layer], dtb_rows[layer], dn_onorm[layer][None, :])
        yb = _pool_branch(big3d, pw16[layer], pool_scale[layer][None, :])
        half = HEAD_DIM // 2
        gain_tabs = (cos * q_norm[layer], sin * jnp.roll(q_norm[layer], half),
                     cos * k_norm[layer], sin * jnp.roll(k_norm[layer], half))
        prep = _attn_prep(big3d, small3d, gain_tabs + idx_tabs)
        yc = _dsa_branch(big3d, *prep)
        x2d = _merge(ya.reshape(b * t, BRANCH_W), yb.reshape(b * t, BRANCH_W), yc.reshape(b * t, BRANCH_W),
                     big2d, gate_b[layer], wb16[layer], wo16[layer], x2d)
    return x2d.reshape(b, t, d)
```

```python
import functools

import jax
import jax.numpy as jnp
from jax import lax
from jax.experimental import pallas as pl
from jax.experimental.pallas import tpu as pltpu

F32 = jnp.float32
BF16 = jnp.bfloat16

D_MODEL = 1024
HEADS = 4
HEAD_DIM = 128
BRANCH_W = HEADS * HEAD_DIM
DN_CONV = 4
POOL_WINDOWS = (2, 4, 8, 16)
POOL_GROUP = 128
IDX_DIM = 64
TOPK_MAX = 256
ROPE_THETA = 10000.0
NORM_EPS = 1e-6
N_BRANCH = 3

COL_QA, COL_KA, COL_VA, COL_ZA = 0, 512, 1024, 1536
COL_UB, COL_ZB = 2048, 2560
COL_QC, COL_KC, COL_VC, COL_ZC = 3072, 3584, 4096, 4608
COL_GATES = 5120
BIG_W = 8192
SMALL_W = 384
SM_BETA, SM_DECAY, SM_IDXW = 64, 68, 72

V7X_VMEM_LIMIT = 56 * 1024 * 1024

PROJ_TM = 512
PROJ_TN = 512
MERGE_TM = 1024
PREP_TT = 512
POOL_TT = 256
POOL_HALO = 16
DN_CHUNK = 128
CONV_HALO = 8
CONV_ROWS = 128
DN_BASE = 8
DN_PAR = 2
ATT_QB = 256
ATT_KC = 256
NUM_BISECT = 36
BISECT_UNROLL = 4
COUNT_LANES = 4
MASK_NEG = -1e30
LOG2_E = 1.4426950408889634


def _nt_dot(a, b):
    return lax.dot_general(a, b, (((1,), (1,)), ((), ())), preferred_element_type=F32)


def _dot(a, b):
    return jnp.dot(a, b, preferred_element_type=F32)


def _silu(x):
    return x * jax.nn.sigmoid(x)


def _cparams(n_axes):
    return pltpu.CompilerParams(dimension_semantics=("arbitrary",) * n_axes,
                                vmem_limit_bytes=V7X_VMEM_LIMIT)


def _inproj_kernel(x_ref, g_ref, w_ref, cw_ref, big_ref, small_ref, conv_buf, *, tiles_per_seq):
    step = pl.program_id(0)

    @pl.when(step == 0)
    def _():
        conv_buf[...] = jnp.zeros(conv_buf.shape, F32)

    x = x_ref[...]
    ms = jnp.mean(x * x, axis=-1, keepdims=True)
    h = ((x * lax.rsqrt(ms + NORM_EPS)) * g_ref[...]).astype(BF16)
    seq_start = (step % tiles_per_seq) == 0
    hd = HEAD_DIM
    halo = CONV_HALO

    def stage(seg):
        c = seg * PROJ_TN
        prev = jnp.where(seq_start, 0.0, conv_buf[seg, PROJ_TM:PROJ_TM + halo, :])
        conv_buf[seg, 0:halo, :] = prev
        conv_buf[seg, halo:PROJ_TM + halo, :] = _dot(h, w_ref[:, c:c + PROJ_TN])

    def conv_piece(seg, r, k, anchor):
        c = seg * PROJ_TN + k * hd
        cw = cw_ref[:, c:c + hd] + anchor
        rows = conv_buf[seg, r:r + CONV_ROWS + halo, k * hd:(k + 1) * hd]
        acc = rows * cw[DN_CONV - 1:DN_CONV, :]
        for s in range(1, DN_CONV):
            acc = acc + pltpu.roll(rows, s, axis=0) * cw[DN_CONV - 1 - s:DN_CONV - s, :]
        z = _silu(acc[halo:, :])
        if c < COL_VA:
            scale = hd ** -0.5 if c < COL_KA else 1.0
            z = z * (lax.rsqrt(jnp.sum(z * z, axis=-1, keepdims=True) + NORM_EPS) * scale)
        big_ref[r:r + CONV_ROWS, c:c + hd] = z.astype(big_ref.dtype)

    plain = list(range(COL_ZA, BIG_W, PROJ_TN))
    n_slot = 4
    for seg in range(COL_ZA // PROJ_TN):
        stage(seg)
        pieces = [(seg, r, k) for r in range(0, PROJ_TM, CONV_ROWS) for k in range(HEADS)]
        per_slot = -(-len(pieces) // n_slot)
        for _ in range(n_slot):
            c = plain.pop(0)
            y = _dot(h, w_ref[:, c:c + PROJ_TN])
            big_ref[:, c:c + PROJ_TN] = y.astype(big_ref.dtype)
            anchor = y[PROJ_TM - 1:PROJ_TM, 0:hd] * 0.0
            for piece in pieces[:per_slot]:
                conv_piece(*piece, anchor)
            pieces = pieces[per_slot:]
    for c in plain:
        big_ref[:, c:c + PROJ_TN] = _dot(h, w_ref[:, c:c + PROJ_TN]).astype(big_ref.dtype)
    small_ref[...] = _dot(h, w_ref[:, BIG_W:])


def _in_projection(x2d, g_row, w_p, conv_w, layer, tiles_per_seq):
    m = x2d.shape[0]
    return pl.pallas_call(
        functools.partial(_inproj_kernel, tiles_per_seq=tiles_per_seq),
        grid=(m // PROJ_TM,),
        in_specs=[
            pl.BlockSpec((PROJ_TM, D_MODEL), lambda i: (i, 0)),
            pl.BlockSpec((1, D_MODEL), lambda i: (0, 0)),
            pl.BlockSpec((None, D_MODEL, BIG_W + SMALL_W), lambda i: (layer, 0, 0), pipeline_mode=pl.Buffered(1)),
            pl.BlockSpec((DN_CONV, 3 * BRANCH_W), lambda i: (0, 0)),
        ],
        out_specs=[
            pl.BlockSpec((PROJ_TM, BIG_W), lambda i: (i, 0)),
            pl.BlockSpec((PROJ_TM, SMALL_W), lambda i: (i, 0)),
        ],
        out_shape=[
            jax.ShapeDtypeStruct((m, BIG_W), BF16),
            jax.ShapeDtypeStruct((m, SMALL_W), F32),
        ],
        scratch_shapes=[pltpu.VMEM((3, PROJ_TM + CONV_HALO, BRANCH_W), F32)],
        compiler_params=_cparams(1),
        name="in_projection",
    )(x2d, g_row, w_p, conv_w)


def _pool_kernel(u_ref, z_ref, pw_ref, ps_ref, y_ref):
    t_len = u_ref.shape[0]
    for t in range(t_len // POOL_TT):
        r0 = t * POOL_TT
        cur = u_ref[r0:r0 + POOL_TT, :].astype(F32)
        if t == 0:
            prev = jnp.zeros((POOL_HALO, cur.shape[1]), F32)
        else:
            prev = u_ref[r0 - POOL_HALO:r0, :].astype(F32)
        win_rows = jnp.concatenate([prev, cur], axis=0)
        pos = r0 + lax.broadcasted_iota(jnp.int32, (POOL_TT, 1), 0)
        for gi, win in enumerate(POOL_WINDOWS):
            sl = slice(gi * POOL_GROUP, (gi + 1) * POOL_GROUP)
            s = win_rows[:, sl]
            shift = 1
            while shift < win:
                s = s + pltpu.roll(s, shift, axis=0)
                shift *= 2
            count = jnp.minimum(pos + 1, win).astype(F32)
            pooled = s[POOL_HALO:, :] / count - cur[:, sl]
            mixed = _dot(pooled.astype(BF16), pw_ref[gi])
            zg = z_ref[r0:r0 + POOL_TT, sl].astype(F32)
            y_ref[r0:r0 + POOL_TT, sl] = (mixed * ps_ref[:, sl] * _silu(zg)).astype(y_ref.dtype)


def _pool_branch(big3d, pool_w, pool_scale_row):
    b, t, _ = big3d.shape
    width = POOL_GROUP * len(POOL_WINDOWS)
    return pl.pallas_call(
        _pool_kernel,
        grid=(b,),
        in_specs=[
            pl.BlockSpec((None, t, width), lambda i: (i, 0, COL_UB // width)),
            pl.BlockSpec((None, t, width), lambda i: (i, 0, COL_ZB // width)),
            pl.BlockSpec((len(POOL_WINDOWS), POOL_GROUP, POOL_GROUP), lambda i: (0, 0, 0)),
            pl.BlockSpec((1, width), lambda i: (0, 0)),
        ],
        out_specs=pl.BlockSpec((None, t, width), lambda i: (i, 0, 0)),
        out_shape=jax.ShapeDtypeStruct((b, t, width), BF16),
        compiler_params=_cparams(1),
        name="pool_branch",
    )(big3d, big3d, pool_w, pool_scale_row)


def _softplus(x):
    return jnp.maximum(x, 0.0) + jnp.log1p(jnp.exp(-jnp.abs(x)))


def _split_bf16(a):
    hi = a.astype(BF16)
    lo = (a - hi.astype(F32)).astype(BF16)
    return hi, lo


def _dot_split(a, b):
    a_hi, a_lo = a
    b_hi, b_lo = b
    return _dot(jnp.concatenate([a_hi, a_lo, a_hi], axis=1), jnp.concatenate([b_hi, b_hi, b_lo], axis=0))


def _dn_kernel(q_ref, k_ref, v_ref, z_ref, sm_ref, alog_ref, dtb_ref, on_ref,
               y_ref, u_s, w_s, qg_s, a_s, kdt_s, el_s, st_s):
    t_len = q_ref.shape[0]
    n_chunks = t_len // DN_CHUNK
    c = DN_CHUNK
    hd = HEAD_DIM
    row = lax.broadcasted_iota(jnp.int32, (c, c), 0)
    col = lax.broadcasted_iota(jnp.int32, (c, c), 1)
    tril = row >= col
    strict = row > col
    tril16 = tril.astype(BF16)
    tril16x3 = jnp.concatenate([tril16, tril16, tril16], axis=1)
    eye_f = (row == col).astype(F32)
    base_blk = (row // DN_BASE) == (col // DN_BASE)
    pair_blks = []
    size = DN_BASE
    while size < c:
        pair_blks.append(((row // (2 * size)) == (col // (2 * size))) & ((row // size) != (col // size)))
        size *= 2

    def prepare(gi, carry):
        cis = [gi * DN_PAR + j for j in range(DN_PAR)]
        starts = [pl.multiple_of(ci * c, c) for ci in cis]
        beta_all, gc_all = [], []
        for start in starts:
            sm = sm_ref[pl.ds(start, c), :]
            beta_all.append(jax.nn.sigmoid(sm))
            g_all = -jnp.exp(alog_ref[...]) * _softplus(sm + dtb_ref[...])
            g_hi = g_all.astype(BF16)
            g_r = g_all - g_hi.astype(F32)
            g_mid = g_r.astype(BF16)
            g_lo = (g_r - g_mid.astype(F32)).astype(BF16)
            gc_all.append(_dot(tril16x3, jnp.concatenate([g_hi, g_mid, g_lo], axis=0)))
        items = [(j, h) for j in range(DN_PAR) for h in range(HEADS)]
        ids = range(len(items))
        sls = [slice(h * hd, (h + 1) * hd) for _, h in items]
        qn = [q_ref[pl.ds(starts[j], c), sls[i]].astype(F32) for i, (j, _) in enumerate(items)]
        kn = [k_ref[pl.ds(starts[j], c), sls[i]].astype(F32) for i, (j, _) in enumerate(items)]
        beta_b = [jnp.broadcast_to(beta_all[j][:, SM_BETA + h:SM_BETA + h + 1], (c, hd)) for j, h in items]
        gc = [jnp.broadcast_to(gc_all[j][:, SM_DECAY + h:SM_DECAY + h + 1], (c, hd)) for j, h in items]
        decay = [jnp.exp(jnp.where(tril, gc[i] - gc[i].T, -jnp.inf)) for i in ids]
        kb = [kn[i] * beta_b[i] for i in ids]
        kn16 = [kn[i].astype(BF16) for i in ids]
        kq = [_nt_dot(jnp.concatenate([kb[i].astype(BF16), qn[i].astype(BF16)], axis=0), kn16[i]) for i in ids]
        lmat = [jnp.where(strict, kq[i][:c, :] * decay[i], 0.0) for i in ids]
        l_hi = [lmat[i].astype(BF16) for i in ids]
        l_lo = [(lmat[i] - l_hi[i].astype(F32)).astype(BF16) for i in ids]
        zero16 = jnp.zeros((c, c), BF16)
        d_parts = [(jnp.where(base_blk, l_hi[i], zero16), jnp.where(base_blk, l_lo[i], zero16)) for i in ids]
        tmat = [eye_f - jnp.where(base_blk, lmat[i], 0.0) for i in ids]
        power = [_dot_split(d_parts[i], d_parts[i]) for i in ids]
        span = 2
        while span < DN_BASE:
            parts = [_split_bf16(power[i]) for i in ids]
            t_parts = [_split_bf16(tmat[i]) for i in ids]
            span *= 2
            if span < DN_BASE:
                both = [_dot_split((jnp.concatenate([t_parts[i][0], parts[i][0]], axis=0),
                                    jnp.concatenate([t_parts[i][1], parts[i][1]], axis=0)), parts[i]) for i in ids]
                tmat = [tmat[i] + both[i][:c, :] for i in ids]
                power = [both[i][c:, :] for i in ids]
            else:
                tmat = [tmat[i] + _dot_split(t_parts[i], parts[i]) for i in ids]
        for pair_blk in pair_blks:
            t_parts = [_split_bf16(tmat[i]) for i in ids]
            off = [(jnp.where(pair_blk, l_hi[i], zero16), jnp.where(pair_blk, l_lo[i], zero16)) for i in ids]
            cx = [_dot_split(off[i], t_parts[i]) for i in ids]
            tmat = [tmat[i] - _dot_split(t_parts[i], _split_bf16(cx[i])) for i in ids]
        for i, (j, h) in enumerate(items):
            sl = sls[i]
            start = starts[j]
            t16 = tmat[i].astype(BF16)
            egc = jnp.exp(gc[i])
            vb = v_ref[pl.ds(start, c), sl].astype(F32) * beta_b[i]
            uw = _dot(t16, jnp.concatenate([vb.astype(BF16), (kb[i] * egc).astype(BF16)], axis=1))
            u_s[pl.ds(start, c), sl] = uw[:, :hd]
            w_s[pl.ds(start, c), sl] = uw[:, hd:].astype(BF16)
            qg_s[pl.ds(start, c), sl] = (qn[i] * egc).astype(BF16)
            a_s[pl.ds(start, c), sl] = jnp.where(tril, kq[i][c:, :] * decay[i], 0.0).astype(BF16)
            g_last = gc[i][c - 1:c, :]
            kd = kn[i] * jnp.exp(g_last - gc[i])
            kdt_s[pl.ds(start, c), sl] = kd.T.astype(BF16)
            el_s[pl.ds(pl.multiple_of(cis[j] * 8, 8), 8), sl] = jnp.broadcast_to(jnp.exp(g_last), (8, hd))
        return carry

    lax.fori_loop(0, n_chunks // DN_PAR, prepare, 0)

    st_s[...] = jnp.zeros(st_s.shape, F32)

    def scan(ci, carry):
        start = pl.multiple_of(ci * c, c)
        hs = range(HEADS)
        sls = [slice(h * hd, (h + 1) * hd) for h in hs]
        state = [st_s[h] for h in hs]
        s16 = [state[h].astype(BF16) for h in hs]
        v_new = [u_s[pl.ds(start, c), sls[h]] - _dot(w_s[pl.ds(start, c), sls[h]], s16[h]) for h in hs]
        v16 = [v_new[h].astype(BF16) for h in hs]
        for h in hs:
            e_last = el_s[pl.ds(pl.multiple_of(ci * 8, 8), 8), sls[h]][0:1, :]
            st_s[h] = state[h] * e_last + _dot(kdt_s[pl.ds(start, c), sls[h]], v16[h])
        for h in hs:
            sl = sls[h]
            o = _dot(qg_s[pl.ds(start, c), sl], s16[h]) + _dot(a_s[pl.ds(start, c), sl], v16[h])
            on = o * lax.rsqrt(jnp.mean(o * o, axis=-1, keepdims=True) + NORM_EPS) * on_ref[...]
            zg = z_ref[pl.ds(start, c), sl].astype(F32)
            y_ref[pl.ds(start, c), sl] = (on * _silu(zg)).astype(y_ref.dtype)
        return carry

    lax.fori_loop(0, n_chunks, scan, 0)


def _deltanet_branch(big3d, small3d, alog_row, dtb_row, onorm_row):
    b, t, _ = big3d.shape
    hd = HEAD_DIM
    w = BRANCH_W
    seq_spec = lambda col: pl.BlockSpec((None, t, w), lambda i, col=col: (i, 0, col // w))
    row_spec = pl.BlockSpec((1, hd), lambda i: (0, 0))
    return pl.pallas_call(
        _dn_kernel,
        grid=(b,),
        in_specs=[
            seq_spec(COL_QA), seq_spec(COL_KA), seq_spec(COL_VA), seq_spec(COL_ZA),
            pl.BlockSpec((None, t, hd), lambda i: (i, 0, SMALL_W // hd - 1)),
            row_spec, row_spec, row_spec,
        ],
        out_specs=pl.BlockSpec((None, t, w), lambda i: (i, 0, 0)),
        out_shape=jax.ShapeDtypeStruct((b, t, w), BF16),
        scratch_shapes=[
            pltpu.VMEM((t, w), F32),
            pltpu.VMEM((t, w), BF16),
            pltpu.VMEM((t, w), BF16),
            pltpu.VMEM((t, w), BF16),
            pltpu.VMEM((t, w), BF16),
            pltpu.VMEM((8 * t // DN_CHUNK, w), F32),
            pltpu.VMEM((HEADS, hd, hd), F32),
        ],
        compiler_params=_cparams(1),
        name="deltanet_branch",
    )(big3d, big3d, big3d, big3d, small3d, alog_row, dtb_row, onorm_row)


def _attn_prep_kernel(q_ref, k_ref, v_ref, sm_ref, qc_ref, qs_ref, kc_ref, ks_ref,
                      icos_ref, isin_up_ref, isin_dn_ref,
                      qo_ref, ko_ref, vt_ref, qio_ref, kia_ref, kib_ref, wt_ref):
    hd = HEAD_DIM
    lane_r = lax.broadcasted_iota(jnp.int32, (hd, hd), 0)
    lane_c = lax.broadcasted_iota(jnp.int32, (hd, hd), 1)
    swap_halves = (lane_r == (lane_c + hd // 2) % hd).astype(BF16)

    def norm_rope(ref, cos_g, sin_g, out_ref, scale):
        for h in range(HEADS):
            sl = slice(h * hd, (h + 1) * hd)
            x16 = ref[:, sl]
            x = x16.astype(F32)
            inv = lax.rsqrt(jnp.mean(x * x, axis=-1, keepdims=True) + NORM_EPS) * scale
            out_ref[:, sl] = ((x * cos_g + _dot(x16, swap_halves) * sin_g) * inv).astype(out_ref.dtype)

    norm_rope(q_ref, qc_ref[...], qs_ref[...], qo_ref, hd ** -0.5 * LOG2_E)
    norm_rope(k_ref, kc_ref[...], ks_ref[...], ko_ref, 1.0)
    w = v_ref.shape[1]
    eye = (lax.broadcasted_iota(jnp.int32, (w, w), 0) == lax.broadcasted_iota(jnp.int32, (w, w), 1)).astype(BF16)
    vt_ref[...] = _nt_dot(eye, v_ref[...]).astype(vt_ref.dtype)

    sm = sm_ref[...]
    iq_w = HEADS * IDX_DIM
    half = IDX_DIM // 2
    iq = sm[:, :iq_w]
    iq_r = (iq * icos_ref[...] + pltpu.roll(iq, iq_w - half, axis=1) * isin_up_ref[...]
            + pltpu.roll(iq, half, axis=1) * isin_dn_ref[...])
    qio_ref[...] = iq_r.astype(qio_ref.dtype)
    last = sm[:, iq_w:]
    lw = last.shape[1]
    ik_r = (last * icos_ref[:, :lw] + pltpu.roll(last, lw - half, axis=1) * isin_up_ref[:, :lw]
            + pltpu.roll(last, half, axis=1) * isin_dn_ref[:, :lw])
    lane = lax.broadcasted_iota(jnp.int32, ik_r.shape, 1)
    ik_r = jnp.where(lane < IDX_DIM, ik_r, 0.0)
    kia_ref[...] = ik_r.astype(kia_ref.dtype)
    kib_ref[...] = pltpu.roll(ik_r, IDX_DIM, axis=1).astype(kib_ref.dtype)
    wt = last.T
    wt_ref[...] = wt[SM_IDXW:SM_IDXW + 8, :] * (HEADS ** -0.5 * IDX_DIM ** -0.5)


def _attn_prep(big3d, small3d, tabs):
    b, t, _ = big3d.shape
    w = BRANCH_W
    tt = PREP_TT
    iq_w = HEADS * IDX_DIM
    seq = lambda col: pl.BlockSpec((None, tt, w), lambda j, i, col=col: (i, j, col // w))
    tab = lambda width: pl.BlockSpec((tt, width), lambda j, i: (j, 0))
    return pl.pallas_call(
        _attn_prep_kernel,
        grid=(t // tt, b),
        in_specs=[
            seq(COL_QC), seq(COL_KC), seq(COL_VC),
            pl.BlockSpec((None, tt, SMALL_W), lambda j, i: (i, j, 0)),
            tab(HEAD_DIM), tab(HEAD_DIM), tab(HEAD_DIM), tab(HEAD_DIM), tab(iq_w), tab(iq_w), tab(iq_w),
        ],
        out_specs=[
            pl.BlockSpec((None, tt, w), lambda j, i: (i, j, 0)),
            pl.BlockSpec((None, tt, w), lambda j, i: (i, j, 0)),
            pl.BlockSpec((None, w, tt), lambda j, i: (i, 0, j)),
            pl.BlockSpec((None, tt, iq_w), lambda j, i: (i, j, 0)),
            pl.BlockSpec((None, tt, 2 * IDX_DIM), lambda j, i: (i, j, 0)),
            pl.BlockSpec((None, tt, 2 * IDX_DIM), lambda j, i: (i, j, 0)),
            pl.BlockSpec((None, 8, tt), lambda j, i: (i, 0, j)),
        ],
        out_shape=[
            jax.ShapeDtypeStruct((b, t, w), BF16),
            jax.ShapeDtypeStruct((b, t, w), BF16),
            jax.ShapeDtypeStruct((b, w, t), BF16),
            jax.ShapeDtypeStruct((b, t, iq_w), BF16),
            jax.ShapeDtypeStruct((b, t, 2 * IDX_DIM), BF16),
            jax.ShapeDtypeStruct((b, t, 2 * IDX_DIM), BF16),
            jax.ShapeDtypeStruct((b, 8, t), F32),
        ],
        compiler_params=_cparams(2),
        name="attn_prep",
    )(big3d, big3d, big3d, small3d, *tabs)


def _dsa_kernel(q_ref, k_ref, vt_ref, qi_ref, kia_ref, kib_ref, wt_ref, z_ref, y_ref, s_ref, acc_ref, *, topk):
    qb, kc = ATT_QB, ATT_KC
    blk_i = pl.program_id(1)
    n_kc = blk_i + 1
    q_pos = blk_i * qb + lax.broadcasted_iota(jnp.int32, (1, qb), 1)
    qi = qi_ref[...]
    wt = wt_ref[...]
    inf = jnp.inf

    def col_sum(x):
        return x.reshape(kc // 8, 8, qb).sum(axis=0)

    def total(x):
        return jnp.sum(x, axis=0, keepdims=True)

    def score_chunk(ci, carry, diagonal, width=kc):
        vmax, vmin, min_pos, n_pos, n_nonneg = carry
        off = pl.multiple_of(ci * width, width)
        ka = kia_ref[pl.ds(off, width), :]
        kb = kib_ref[pl.ds(off, width), :]
        s = jnp.zeros((width, qb), F32)
        for h in range(HEADS):
            kk = ka if h % 2 == 0 else kb
            qq = qi[:, (h // 2) * 2 * IDX_DIM:(h // 2 + 1) * 2 * IDX_DIM]
            s = s + jnp.maximum(_nt_dot(kk, qq), 0.0) * wt[h:h + 1, :]
        s = jnp.where(s == 0.0, 0.0, s)
        if diagonal:
            key_pos = off + lax.broadcasted_iota(jnp.int32, (width, 1), 0)
            causal = key_pos <= q_pos
            sc = jnp.where(causal, s, -inf)
            s_hi = jnp.where(causal, s, inf)
        else:
            sc = s_hi = s
        s_ref[pl.ds(off, width), :] = sc
        vmax = jnp.maximum(vmax, jnp.max(sc, axis=0, keepdims=True))
        vmin = jnp.minimum(vmin, jnp.min(s_hi, axis=0, keepdims=True))
        min_pos = jnp.minimum(min_pos, jnp.min(jnp.where(sc > 0.0, sc, inf), axis=0, keepdims=True))
        n_pos = n_pos + jnp.where(sc > 0.0, 1.0, 0.0).reshape(width // 8, 8, qb).sum(axis=0)
        n_nonneg = n_nonneg + jnp.where(sc >= 0.0, 1.0, 0.0).reshape(width // 8, 8, qb).sum(axis=0)
        return vmax, vmin, min_pos, n_pos, n_nonneg

    row_inf = jnp.full((1, qb), inf, F32)
    zeros8 = jnp.zeros((8, qb), F32)
    stats = lax.fori_loop(0, blk_i // 2, functools.partial(score_chunk, diagonal=False, width=2 * kc),
                          (-row_inf, row_inf, row_inf, zeros8, zeros8))
    stats = lax.fori_loop(blk_i - blk_i % 2, blk_i, functools.partial(score_chunk, diagonal=False), stats)
    vmax, vmin, min_pos, n_pos, n_nonneg = score_chunk(blk_i, stats, diagonal=True)
    n_pos = total(n_pos)
    n_nonneg = total(n_nonneg)

    def count_ge(thr):
        def body(ci, acc):
            off = pl.multiple_of(ci * kc, kc)
            hit = jnp.where(s_ref[pl.ds(off, kc), :] >= thr, 1.0, 0.0)
            return acc + hit.reshape(kc // (8 * COUNT_LANES), COUNT_LANES * 8, qb).sum(axis=0)
        return total(lax.fori_loop(0, n_kc, body, jnp.zeros((COUNT_LANES * 8, qb), F32)))

    k_sel = jnp.minimum(q_pos + 1, topk).astype(F32)
    n_causal = (q_pos + 1).astype(F32)
    at_zero = (n_pos < k_sel) & (k_sel <= n_nonneg)
    above = k_sel <= n_pos
    lo = jnp.where(at_zero, 0.0, jnp.where(above, min_pos, vmin))
    hi = jnp.where(at_zero, min_pos, jnp.where(above, vmax + (jnp.abs(vmax) + 1.0), 0.0))
    c_lo = jnp.where(at_zero, n_nonneg, jnp.where(above, n_pos, n_causal))
    c_hi = jnp.where(at_zero, n_pos, jnp.where(above, 0.0, n_nonneg))
    done = jnp.where(at_zero | (c_lo == k_sel), 1.0, 0.0)

    def n_open(d):
        return jnp.sum(1.0 - d)

    def bisect_cond(carry):
        it, n_left = carry[0], carry[1]
        return (it < NUM_BISECT) & (n_left > 0.0)

    def bisect_body(carry):
        it, _, lo, hi, c_lo, c_hi, done = carry
        for _ in range(BISECT_UNROLL):
            mid = 0.5 * lo + 0.5 * hi
            cnt = count_ge(mid)
            live = done < 0.5
            up = (cnt >= k_sel) & live
            dn = (cnt < k_sel) & live
            lo = jnp.where(up, mid, lo)
            c_lo = jnp.where(up, cnt, c_lo)
            hi = jnp.where(dn, mid, hi)
            c_hi = jnp.where(dn, cnt, c_hi)
            done = jnp.where(c_lo == k_sel, 1.0, done)
        return it + BISECT_UNROLL, n_open(done), lo, hi, c_lo, c_hi, done

    _, _, lo, hi, c_lo, c_hi, done = lax.while_loop(
        bisect_cond, bisect_body, (jnp.int32(0), n_open(done), lo, hi, c_lo, c_hi, done))
    need = k_sel - c_hi
    n_tied = jnp.sum(jnp.where(c_lo - c_hi > need, 1.0, 0.0))

    def tie_mask():
        tri = (lax.broadcasted_iota(jnp.int32, (kc, kc), 1)
               < lax.broadcasted_iota(jnp.int32, (kc, kc), 0)).astype(BF16)

        def mask_body(ci, seen):
            off = pl.multiple_of(ci * kc, kc)
            blk = s_ref[pl.ds(off, kc), :]
            tie = jnp.where((blk >= lo) & (blk < hi), 1.0, 0.0)
            rank = _dot(tri, tie.astype(BF16)) + seen
            sel = (blk >= hi) | ((tie > 0.5) & (rank < need))
            s_ref[pl.ds(off, kc), :] = jnp.where(sel, 1.0, 0.0)
            return seen + total(col_sum(tie))

        lax.fori_loop(0, n_kc, mask_body, jnp.zeros((1, qb), F32))

    pl.when(n_tied > 0.0)(tie_mask)
    sel_thr = jnp.where(n_tied > 0.0, 0.5, lo)

    hs = range(HEADS)
    sls = [slice(h * HEAD_DIM, (h + 1) * HEAD_DIM) for h in hs]
    qh = [q_ref[:, sl] for sl in sls]
    acc_ref[...] = jnp.zeros(acc_ref.shape, F32)

    def att_body(ci, carry):
        ms, ls = carry
        off = pl.multiple_of(ci * kc, kc)
        sel = s_ref[pl.ds(off, kc), :] >= sel_thr
        lm = [jnp.where(sel, _nt_dot(k_ref[pl.ds(off, kc), sls[h]], qh[h]), MASK_NEG) for h in hs]
        m_new = [jnp.maximum(ms[h], jnp.max(lm[h], axis=0, keepdims=True)) for h in hs]
        p = [jnp.exp2(lm[h] - m_new[h]) for h in hs]
        alpha = [jnp.exp2(ms[h] - m_new[h]) for h in hs]
        l_new = [alpha[h] * ls[h] + jnp.sum(p[h], axis=0, keepdims=True) for h in hs]
        pv = [_dot(vt_ref[sls[h], pl.ds(off, kc)], p[h].astype(BF16)) for h in hs]
        for h in hs:
            acc_ref[sls[h], :] = alpha[h] * acc_ref[sls[h], :] + pv[h]
        return tuple(m_new), tuple(l_new)

    row_neg = jnp.full((1, qb), MASK_NEG, F32)
    row_zero = jnp.zeros((1, qb), F32)
    _, ls = lax.fori_loop(0, n_kc, att_body, ((row_neg,) * HEADS, (row_zero,) * HEADS))
    for h in hs:
        o = (acc_ref[sls[h], :] / ls[h]).T
        y_ref[:, sls[h]] = (o * _silu(z_ref[:, sls[h]].astype(F32))).astype(y_ref.dtype)


def _dsa_branch(big3d, q_r, k_r, v_t, qi_r, ki_a, ki_b, w_t):
    b, t, _ = big3d.shape
    w = BRANCH_W
    qb = ATT_QB
    topk = min(TOPK_MAX, t // 4)
    return pl.pallas_call(
        functools.partial(_dsa_kernel, topk=topk),
        grid=(b, t // qb),
        in_specs=[
            pl.BlockSpec((None, qb, w), lambda i, j: (i, j, 0)),
            pl.BlockSpec((None, t, w), lambda i, j: (i, 0, 0)),
            pl.BlockSpec((None, w, t), lambda i, j: (i, 0, 0)),
            pl.BlockSpec((None, qb, HEADS * IDX_DIM), lambda i, j: (i, j, 0)),
            pl.BlockSpec((None, t, 2 * IDX_DIM), lambda i, j: (i, 0, 0)),
            pl.BlockSpec((None, t, 2 * IDX_DIM), lambda i, j: (i, 0, 0)),
            pl.BlockSpec((None, 8, qb), lambda i, j: (i, 0, j)),
            pl.BlockSpec((None, qb, w), lambda i, j: (i, j, COL_ZC // w)),
        ],
        out_specs=pl.BlockSpec((None, qb, w), lambda i, j: (i, j, 0)),
        out_shape=jax.ShapeDtypeStruct((b, t, w), BF16),
        scratch_shapes=[pltpu.VMEM((t, qb), F32),
                        pltpu.VMEM((w, qb), F32)],
        compiler_params=_cparams(2),
        name="sparse_attention",
    )(q_r, k_r, v_t, qi_r, ki_a, ki_b, w_t, big3d)


def _merge_kernel(ya_ref, yb_ref, yc_ref, ga_ref, gb_ref, gc_ref, bias_ref, wb_ref, wo_ref, x_ref, o_ref):
    merged = None
    for n, (y_ref, g_ref) in enumerate(((ya_ref, ga_ref), (yb_ref, gb_ref), (yc_ref, gc_ref))):
        proj = _dot(y_ref[...], wb_ref[n])
        gate = jax.nn.sigmoid(g_ref[...].astype(F32) + bias_ref[n:n + 1, :])
        term = gate * proj
        merged = term if merged is None else merged + term
    o_ref[...] = x_ref[...] + _dot(merged.astype(BF16), wo_ref[...])


def _merge(ya, yb, yc, big2d, gate_b, w_branch, w_out, x2d):
    m = x2d.shape[0]
    tm = MERGE_TM
    yspec = pl.BlockSpec((tm, BRANCH_W), lambda i: (i, 0))
    gspec = lambda n: pl.BlockSpec((tm, D_MODEL), lambda i, n=n: (i, COL_GATES // D_MODEL + n))
    return pl.pallas_call(
        _merge_kernel,
        grid=(m // tm,),
        in_specs=[
            yspec, yspec, yspec, gspec(0), gspec(1), gspec(2),
            pl.BlockSpec((N_BRANCH, D_MODEL), lambda i: (0, 0)),
            pl.BlockSpec((N_BRANCH, BRANCH_W, D_MODEL), lambda i: (0, 0, 0)),
            pl.BlockSpec((D_MODEL, D_MODEL), lambda i: (0, 0)),
            pl.BlockSpec((tm, D_MODEL), lambda i: (i, 0)),
        ],
        out_specs=pl.BlockSpec((tm, D_MODEL), lambda i: (i, 0)),
        out_shape=jax.ShapeDtypeStruct((m, D_MODEL), F32),
        compiler_params=_cparams(1),
        name="merge",
    )(ya, yb, yc, big2d, big2d, big2d, gate_b, w_branch, w_out, x2d)


_IN_SPLIT = (("dn_qkvz", 4 * BRANCH_W), ("beta_decay", 2 * HEADS), ("pool_uz", 2 * BRANCH_W),
             ("att_qkvz", 4 * BRANCH_W), ("idx_q", HEADS * IDX_DIM), ("idx_k", IDX_DIM), ("idx_w", HEADS),
             ("gates", N_BRANCH * D_MODEL))
_IN_OFFSET = {name: sum(w for _, w in _IN_SPLIT[:i]) for i, (name, _) in enumerate(_IN_SPLIT)}

PERM_COLS = 128
_PERM_REGULAR = (BIG_W + HEADS * IDX_DIM) // PERM_COLS


def _perm_source(blk):
    shift = jnp.where(blk < COL_UB // PERM_COLS, _IN_OFFSET["dn_qkvz"] - COL_QA,
                      jnp.where(blk < COL_GATES // PERM_COLS, _IN_OFFSET["pool_uz"] - COL_UB,
                                jnp.where(blk < BIG_W // PERM_COLS, _IN_OFFSET["gates"] - COL_GATES,
                                          _IN_OFFSET["idx_q"] - BIG_W)))
    return jnp.where(blk < _PERM_REGULAR, blk * PERM_COLS + shift, 0)


def _permute_kernel(w_ref, ik_ref, bd_ref, iw_ref, o_ref):
    blk = pl.program_id(0)
    depth = o_ref.shape[0]

    @pl.when(blk < _PERM_REGULAR)
    def _():
        for l in range(depth):
            o_ref[l] = w_ref[:, l, :].T.astype(o_ref.dtype)

    @pl.when(blk == _PERM_REGULAR)
    def _():
        row = lax.broadcasted_iota(jnp.int32, (8, o_ref.shape[1]), 0)
        for l in range(depth):
            idx_w = jnp.where(row < HEADS, iw_ref[:, l, :], 0.0)
            pad = jnp.zeros((PERM_COLS - IDX_DIM - 16, o_ref.shape[1]), F32)
            rows = jnp.concatenate([ik_ref[:, l, :], bd_ref[:, l, :], idx_w, pad], axis=0)
            o_ref[l] = rows.T.astype(o_ref.dtype)


def _permute_w_in(w_in):
    depth, d, _ = w_in.shape
    w_t = jnp.transpose(w_in, (2, 0, 1))
    el = pl.Element
    fixed = lambda rows, start: pl.BlockSpec((el(rows), el(depth), el(d)), lambda i: (start, 0, 0))
    return pl.pallas_call(
        _permute_kernel,
        grid=(_PERM_REGULAR + 1,),
        in_specs=[
            pl.BlockSpec((el(PERM_COLS), el(depth), el(d)), lambda i: (_perm_source(i), 0, 0)),
            fixed(IDX_DIM, _IN_OFFSET["idx_k"]), fixed(8, _IN_OFFSET["beta_decay"]), fixed(8, _IN_OFFSET["idx_w"]),
        ],
        out_specs=pl.BlockSpec((depth, d, PERM_COLS), lambda i: (0, 0, i)),
        out_shape=jax.ShapeDtypeStruct((depth, d, BIG_W + SMALL_W), BF16),
        compiler_params=_cparams(1),
        name="permute_w_in",
    )(w_t, w_t, w_t, w_t)


def _rope_tables(t):
    def base(dim):
        inv_freq = ROPE_THETA ** (-jnp.arange(0, dim, 2, dtype=F32) / dim)
        ang = jnp.arange(t, dtype=F32)[:, None] * inv_freq[None, :]
        return jnp.cos(ang), jnp.sin(ang)

    cos_a, sin_a = base(HEAD_DIM)
    cos = jnp.concatenate([cos_a, cos_a], axis=-1)
    sin = jnp.concatenate([-sin_a, sin_a], axis=-1)
    cos_i, sin_i = base(IDX_DIM)
    zero = jnp.zeros_like(sin_i)
    icos = jnp.tile(jnp.concatenate([cos_i, cos_i], axis=-1), (1, HEADS))
    isin_up = jnp.tile(jnp.concatenate([-sin_i, zero], axis=-1), (1, HEADS))
    isin_dn = jnp.tile(jnp.concatenate([zero, sin_i], axis=-1), (1, HEADS))
    return cos, sin, icos, isin_up, isin_dn


def _lane_row(vals, offset):
    d, n = vals.shape
    return jnp.zeros((d, 1, HEAD_DIM), F32).at[:, 0, offset:offset + n].set(vals.astype(F32))


def kernel(x, norm_g, w_in, gate_b, conv_w, a_log, dt_bias, dn_onorm, pool_w, pool_scale, q_norm, k_norm,
           w_branch, w_out):
    b, t, d = x.shape
    depth = norm_g.shape[0]
    w_p = _permute_w_in(w_in)
    wb16 = w_branch.astype(BF16)
    wo16 = w_out.astype(BF16)
    pw16 = pool_w.astype(BF16)
    alog_rows = _lane_row(a_log, SM_DECAY)
    dtb_rows = _lane_row(dt_bias, SM_DECAY)
    cos, sin, *idx_tabs = _rope_tables(t)
    idx_tabs = tuple(idx_tabs)
    x2d = x.reshape(b * t, d)
    for layer in range(depth):
        big2d, small2d = _in_projection(x2d, norm_g[layer][None, :], w_p, conv_w[layer], layer, t // PROJ_TM)
        big3d = big2d.reshape(b, t, BIG_W)
        small3d = small2d.reshape(b, t, SMALL_W)
        ya = _deltanet_branch(big3d, small3d, alog_rows[layer], dtb_rows[layer], dn_onorm[layer][None, :])
        yb = _pool_branch(big3d, pw16[layer], pool_scale[layer][None, :])
        half = HEAD_DIM // 2
        gain_tabs = (cos * q_norm[layer], sin * jnp.roll(q_norm[layer], half),
                     cos * k_norm[layer], sin * jnp.roll(k_norm[layer], half))
        prep = _attn_prep(big3d, small3d, gain_tabs + idx_tabs)
        yc = _dsa_branch(big3d, *prep)
        x2d = _merge(ya.reshape(b * t, BRANCH_W), yb.reshape(b * t, BRANCH_W), yc.reshape(b * t, BRANCH_W),
                     big2d, gate_b[layer], wb16[layer], wo16[layer], x2d)
    return x2d.reshape(b, t, d)
```

```python
import functools

import jax
import jax.numpy as jnp
from jax import lax
from jax.experimental import pallas as pl
from jax.experimental.pallas import tpu as pltpu

F32 = jnp.float32
BF16 = jnp.bfloat16

D_MODEL = 1024
HEADS = 4
HEAD_DIM = 128
BRANCH_W = HEADS * HEAD_DIM
DN_CONV = 4
POOL_WINDOWS = (2, 4, 8, 16)
POOL_GROUP = 128
IDX_DIM = 64
TOPK_MAX = 256
ROPE_THETA = 10000.0
NORM_EPS = 1e-6
N_BRANCH = 3

COL_QA, COL_KA, COL_VA, COL_ZA = 0, 512, 1024, 1536
COL_UB, COL_ZB = 2048, 2560
COL_QC, COL_KC, COL_VC, COL_ZC = 3072, 3584, 4096, 4608
COL_GATES = 5120
BIG_W = 8192
SMALL_W = 384
SM_BETA, SM_DECAY, SM_IDXW = 64, 68, 72

V7X_VMEM_LIMIT = 56 * 1024 * 1024

PROJ_TM = 512
PROJ_TN = 512
MERGE_TM = 1024
PREP_TT = 1024
POOL_TT = 256
POOL_HALO = 16
DN_CHUNK = 128
CONV_HALO = 8
CONV_ROWS = 128
DN_BASE = 8
DN_PAR = 2
ATT_QB = 256
ATT_KC = 256
NUM_BISECT = 36
BISECT_UNROLL = 4
COUNT_LANES = 4
MASK_NEG = -1e30
LOG2_E = 1.4426950408889634


def _nt_dot(a, b):
    return lax.dot_general(a, b, (((1,), (1,)), ((), ())), preferred_element_type=F32)


def _dot(a, b):
    return jnp.dot(a, b, preferred_element_type=F32)


def _silu(x):
    return x * jax.nn.sigmoid(x)


def _cparams(n_axes):
    return pltpu.CompilerParams(dimension_semantics=("arbitrary",) * n_axes,
                                vmem_limit_bytes=V7X_VMEM_LIMIT)


def _inproj_kernel(x_ref, g_ref, w_ref, cw_ref, big_ref, small_ref, conv_buf, *, tiles_per_seq):
    step = pl.program_id(0)

    @pl.when(step == 0)
    def _():
        conv_buf[...] = jnp.zeros(conv_buf.shape, F32)

    x = x_ref[...]
    ms = jnp.mean(x * x, axis=-1, keepdims=True)
    h = ((x * lax.rsqrt(ms + NORM_EPS)) * g_ref[...]).astype(BF16)
    seq_start = (step % tiles_per_seq) == 0
    hd = HEAD_DIM
    halo = CONV_HALO

    def stage(seg):
        c = seg * PROJ_TN
        prev = jnp.where(seq_start, 0.0, conv_buf[seg, PROJ_TM:PROJ_TM + halo, :])
        conv_buf[seg, 0:halo, :] = prev
        conv_buf[seg, halo:PROJ_TM + halo, :] = _dot(h, w_ref[:, c:c + PROJ_TN])

    def conv_piece(seg, r, k, anchor):
        c = seg * PROJ_TN + k * hd
        cw = cw_ref[:, c:c + hd] + anchor
        rows = conv_buf[seg, r:r + CONV_ROWS + halo, k * hd:(k + 1) * hd]
        acc = rows * cw[DN_CONV - 1:DN_CONV, :]
        for s in range(1, DN_CONV):
            acc = acc + pltpu.roll(rows, s, axis=0) * cw[DN_CONV - 1 - s:DN_CONV - s, :]
        z = _silu(acc[halo:, :])
        if c < COL_VA:
            scale = hd ** -0.5 if c < COL_KA else 1.0
            z = z * (lax.rsqrt(jnp.sum(z * z, axis=-1, keepdims=True) + NORM_EPS) * scale)
        big_ref[r:r + CONV_ROWS, c:c + hd] = z.astype(big_ref.dtype)

    plain = list(range(COL_ZA, BIG_W, PROJ_TN))
    n_slot = 4
    for seg in range(COL_ZA // PROJ_TN):
        stage(seg)
        pieces = [(seg, r, k) for r in range(0, PROJ_TM, CONV_ROWS) for k in range(HEADS)]
        per_slot = -(-len(pieces) // n_slot)
        for _ in range(n_slot):
            c = plain.pop(0)
            y = _dot(h, w_ref[:, c:c + PROJ_TN])
            big_ref[:, c:c + PROJ_TN] = y.astype(big_ref.dtype)
            anchor = y[PROJ_TM - 1:PROJ_TM, 0:hd] * 0.0
            for piece in pieces[:per_slot]:
                conv_piece(*piece, anchor)
            pieces = pieces[per_slot:]
    for c in plain:
        big_ref[:, c:c + PROJ_TN] = _dot(h, w_ref[:, c:c + PROJ_TN]).astype(big_ref.dtype)
    small_ref[...] = _dot(h, w_ref[:, BIG_W:])


def _in_projection(x2d, g_row, w_p, conv_w, layer, tiles_per_seq):
    m = x2d.shape[0]
    return pl.pallas_call(
        functools.partial(_inproj_kernel, tiles_per_seq=tiles_per_seq),
        grid=(m // PROJ_TM,),
        in_specs=[
            pl.BlockSpec((PROJ_TM, D_MODEL), lambda i: (i, 0)),
            pl.BlockSpec((1, D_MODEL), lambda i: (0, 0)),
            pl.BlockSpec((None, D_MODEL, BIG_W + SMALL_W), lambda i: (layer, 0, 0), pipeline_mode=pl.Buffered(1)),
            pl.BlockSpec((DN_CONV, 3 * BRANCH_W), lambda i: (0, 0)),
        ],
        out_specs=[
            pl.BlockSpec((PROJ_TM, BIG_W), lambda i: (i, 0)),
            pl.BlockSpec((PROJ_TM, SMALL_W), lambda i: (i, 0)),
        ],
        out_shape=[
            jax.ShapeDtypeStruct((m, BIG_W), BF16),
            jax.ShapeDtypeStruct((m, SMALL_W), F32),
        ],
        scratch_shapes=[pltpu.VMEM((3, PROJ_TM + CONV_HALO, BRANCH_W), F32)],
        compiler_params=_cparams(1),
        name="in_projection",
    )(x2d, g_row, w_p, conv_w)


def _pool_kernel(u_ref, z_ref, pw_ref, ps_ref, y_ref):
    t_len = u_ref.shape[0]
    for t in range(t_len // POOL_TT):
        r0 = t * POOL_TT
        cur = u_ref[r0:r0 + POOL_TT, :].astype(F32)
        if t == 0:
            prev = jnp.zeros((POOL_HALO, cur.shape[1]), F32)
        else:
            prev = u_ref[r0 - POOL_HALO:r0, :].astype(F32)
        win_rows = jnp.concatenate([prev, cur], axis=0)
        pos = r0 + lax.broadcasted_iota(jnp.int32, (POOL_TT, 1), 0)
        for gi, win in enumerate(POOL_WINDOWS):
            sl = slice(gi * POOL_GROUP, (gi + 1) * POOL_GROUP)
            s = win_rows[:, sl]
            shift = 1
            while shift < win:
                s = s + pltpu.roll(s, shift, axis=0)
                shift *= 2
            count = jnp.minimum(pos + 1, win).astype(F32)
            pooled = s[POOL_HALO:, :] / count - cur[:, sl]
            mixed = _dot(pooled.astype(BF16), pw_ref[gi])
            zg = z_ref[r0:r0 + POOL_TT, sl].astype(F32)
            y_ref[r0:r0 + POOL_TT, sl] = (mixed * ps_ref[:, sl] * _silu(zg)).astype(y_ref.dtype)


def _pool_branch(big3d, pool_w, pool_scale_row):
    b, t, _ = big3d.shape
    width = POOL_GROUP * len(POOL_WINDOWS)
    return pl.pallas_call(
        _pool_kernel,
        grid=(b,),
        in_specs=[
            pl.BlockSpec((None, t, width), lambda i: (i, 0, COL_UB // width)),
            pl.BlockSpec((None, t, width), lambda i: (i, 0, COL_ZB // width)),
            pl.BlockSpec((len(POOL_WINDOWS), POOL_GROUP, POOL_GROUP), lambda i: (0, 0, 0)),
            pl.BlockSpec((1, width), lambda i: (0, 0)),
        ],
        out_specs=pl.BlockSpec((None, t, width), lambda i: (i, 0, 0)),
        out_shape=jax.ShapeDtypeStruct((b, t, width), BF16),
        compiler_params=_cparams(1),
        name="pool_branch",
    )(big3d, big3d, pool_w, pool_scale_row)


def _softplus(x):
    return jnp.maximum(x, 0.0) + jnp.log1p(jnp.exp(-jnp.abs(x)))


def _split_bf16(a):
    hi = a.astype(BF16)
    lo = (a - hi.astype(F32)).astype(BF16)
    return hi, lo


def _dot_split(a, b):
    a_hi, a_lo = a
    b_hi, b_lo = b
    return _dot(jnp.concatenate([a_hi, a_lo, a_hi], axis=1), jnp.concatenate([b_hi, b_hi, b_lo], axis=0))


def _dn_kernel(q_ref, k_ref, v_ref, z_ref, sm_ref, alog_ref, dtb_ref, on_ref,
               y_ref, u_s, w_s, qg_s, a_s, kdt_s, el_s, st_s):
    t_len = q_ref.shape[0]
    n_chunks = t_len // DN_CHUNK
    c = DN_CHUNK
    hd = HEAD_DIM
    row = lax.broadcasted_iota(jnp.int32, (c, c), 0)
    col = lax.broadcasted_iota(jnp.int32, (c, c), 1)
    tril = row >= col
    strict = row > col
    tril16 = tril.astype(BF16)
    tril16x3 = jnp.concatenate([tril16, tril16, tril16], axis=1)
    eye_f = (row == col).astype(F32)
    base_blk = (row // DN_BASE) == (col // DN_BASE)
    pair_blks = []
    size = DN_BASE
    while size < c:
        pair_blks.append(((row // (2 * size)) == (col // (2 * size))) & ((row // size) != (col // size)))
        size *= 2

    def prepare(gi, carry):
        cis = [gi * DN_PAR + j for j in range(DN_PAR)]
        starts = [pl.multiple_of(ci * c, c) for ci in cis]
        beta_all, gc_all = [], []
        for start in starts:
            sm = sm_ref[pl.ds(start, c), :]
            beta_all.append(jax.nn.sigmoid(sm))
            g_all = -jnp.exp(alog_ref[...]) * _softplus(sm + dtb_ref[...])
            g_hi = g_all.astype(BF16)
            g_r = g_all - g_hi.astype(F32)
            g_mid = g_r.astype(BF16)
            g_lo = (g_r - g_mid.astype(F32)).astype(BF16)
            gc_all.append(_dot(tril16x3, jnp.concatenate([g_hi, g_mid, g_lo], axis=0)))
        items = [(j, h) for j in range(DN_PAR) for h in range(HEADS)]
        ids = range(len(items))
        sls = [slice(h * hd, (h + 1) * hd) for _, h in items]
        qn = [q_ref[pl.ds(starts[j], c), sls[i]].astype(F32) for i, (j, _) in enumerate(items)]
        kn = [k_ref[pl.ds(starts[j], c), sls[i]].astype(F32) for i, (j, _) in enumerate(items)]
        beta_b = [jnp.broadcast_to(beta_all[j][:, SM_BETA + h:SM_BETA + h + 1], (c, hd)) for j, h in items]
        gc = [jnp.broadcast_to(gc_all[j][:, SM_DECAY + h:SM_DECAY + h + 1], (c, hd)) for j, h in items]
        decay = [jnp.exp(jnp.where(tril, gc[i] - gc[i].T, -jnp.inf)) for i in ids]
        kb = [kn[i] * beta_b[i] for i in ids]
        kn16 = [kn[i].astype(BF16) for i in ids]
        kq = [_nt_dot(jnp.concatenate([kb[i].astype(BF16), qn[i].astype(BF16)], axis=0), kn16[i]) for i in ids]
        lmat = [jnp.where(strict, kq[i][:c, :] * decay[i], 0.0) for i in ids]
        l_hi = [lmat[i].astype(BF16) for i in ids]
        l_lo = [(lmat[i] - l_hi[i].astype(F32)).astype(BF16) for i in ids]
        zero16 = jnp.zeros((c, c), BF16)
        d_parts = [(jnp.where(base_blk, l_hi[i], zero16), jnp.where(base_blk, l_lo[i], zero16)) for i in ids]
        tmat = [eye_f - jnp.where(base_blk, lmat[i], 0.0) for i in ids]
        power = [_dot_split(d_parts[i], d_parts[i]) for i in ids]
        span = 2
        while span < DN_BASE:
            parts = [_split_bf16(power[i]) for i in ids]
            t_parts = [_split_bf16(tmat[i]) for i in ids]
            span *= 2
            if span < DN_BASE:
                both = [_dot_split((jnp.concatenate([t_parts[i][0], parts[i][0]], axis=0),
                                    jnp.concatenate([t_parts[i][1], parts[i][1]], axis=0)), parts[i]) for i in ids]
                tmat = [tmat[i] + both[i][:c, :] for i in ids]
                power = [both[i][c:, :] for i in ids]
            else:
                tmat = [tmat[i] + _dot_split(t_parts[i], parts[i]) for i in ids]
        for pair_blk in pair_blks:
            t_parts = [_split_bf16(tmat[i]) for i in ids]
            off = [(jnp.where(pair_blk, l_hi[i], zero16), jnp.where(pair_blk, l_lo[i], zero16)) for i in ids]
            cx = [_dot_split(off[i], t_parts[i]) for i in ids]
            tmat = [tmat[i] - _dot_split(t_parts[i], _split_bf16(cx[i])) for i in ids]
        for i, (j, h) in enumerate(items):
            sl = sls[i]
            start = starts[j]
            t16 = tmat[i].astype(BF16)
            egc = jnp.exp(gc[i])
            vb = v_ref[pl.ds(start, c), sl].astype(F32) * beta_b[i]
            uw = _dot(t16, jnp.concatenate([vb.astype(BF16), (kb[i] * egc).astype(BF16)], axis=1))
            u_s[pl.ds(start, c), sl] = uw[:, :hd]
            w_s[pl.ds(start, c), sl] = uw[:, hd:].astype(BF16)
            qg_s[pl.ds(start, c), sl] = (qn[i] * egc).astype(BF16)
            a_s[pl.ds(start, c), sl] = jnp.where(tril, kq[i][c:, :] * decay[i], 0.0).astype(BF16)
            g_last = gc[i][c - 1:c, :]
            kd = kn[i] * jnp.exp(g_last - gc[i])
            kdt_s[pl.ds(start, c), sl] = kd.T.astype(BF16)
            el_s[pl.ds(pl.multiple_of(cis[j] * 8, 8), 8), sl] = jnp.broadcast_to(jnp.exp(g_last), (8, hd))
        return carry

    def scan_chunk(ci):
        start = pl.multiple_of(ci * c, c)
        hs = range(HEADS)
        sls = [slice(h * hd, (h + 1) * hd) for h in hs]
        state = [st_s[h] for h in hs]
        s16 = [state[h].astype(BF16) for h in hs]
        v_new = [u_s[pl.ds(start, c), sls[h]] - _dot(w_s[pl.ds(start, c), sls[h]], s16[h]) for h in hs]
        v16 = [v_new[h].astype(BF16) for h in hs]
        for h in hs:
            e_last = el_s[pl.ds(pl.multiple_of(ci * 8, 8), 8), sls[h]][0:1, :]
            st_s[h] = state[h] * e_last + _dot(kdt_s[pl.ds(start, c), sls[h]], v16[h])
        for h in hs:
            sl = sls[h]
            o = _dot(qg_s[pl.ds(start, c), sl], s16[h]) + _dot(a_s[pl.ds(start, c), sl], v16[h])
            on = o * lax.rsqrt(jnp.mean(o * o, axis=-1, keepdims=True) + NORM_EPS) * on_ref[...]
            zg = z_ref[pl.ds(start, c), sl].astype(F32)
            y_ref[pl.ds(start, c), sl] = (on * _silu(zg)).astype(y_ref.dtype)

    def scan_group(gi):
        for j in range(DN_PAR):
            scan_chunk(gi * DN_PAR + j)

    n_groups = n_chunks // DN_PAR
    st_s[...] = jnp.zeros(st_s.shape, F32)
    prepare(0, 0)

    def trip(gi, carry):
        scan_group(gi - 1)
        return prepare(gi, carry)

    lax.fori_loop(1, n_groups, trip, 0)
    scan_group(n_groups - 1)


def _deltanet_branch(big3d, small3d, alog_row, dtb_row, onorm_row):
    b, t, _ = big3d.shape
    hd = HEAD_DIM
    w = BRANCH_W
    seq_spec = lambda col: pl.BlockSpec((None, t, w), lambda i, col=col: (i, 0, col // w))
    row_spec = pl.BlockSpec((1, hd), lambda i: (0, 0))
    return pl.pallas_call(
        _dn_kernel,
        grid=(b,),
        in_specs=[
            seq_spec(COL_QA), seq_spec(COL_KA), seq_spec(COL_VA), seq_spec(COL_ZA),
            pl.BlockSpec((None, t, hd), lambda i: (i, 0, SMALL_W // hd - 1)),
            row_spec, row_spec, row_spec,
        ],
        out_specs=pl.BlockSpec((None, t, w), lambda i: (i, 0, 0)),
        out_shape=jax.ShapeDtypeStruct((b, t, w), BF16),
        scratch_shapes=[
            pltpu.VMEM((t, w), F32),
            pltpu.VMEM((t, w), BF16),
            pltpu.VMEM((t, w), BF16),
            pltpu.VMEM((t, w), BF16),
            pltpu.VMEM((t, w), BF16),
            pltpu.VMEM((8 * t // DN_CHUNK, w), F32),
            pltpu.VMEM((HEADS, hd, hd), F32),
        ],
        compiler_params=_cparams(1),
        name="deltanet_branch",
    )(big3d, big3d, big3d, big3d, small3d, alog_row, dtb_row, onorm_row)


def _attn_prep_kernel(q_ref, k_ref, v_ref, sm_ref, qc_ref, qs_ref, kc_ref, ks_ref,
                      icos_ref, isin_up_ref, isin_dn_ref,
                      qo_ref, ko_ref, vt_ref, qio_ref, kia_ref, kib_ref, wt_ref):
    hd = HEAD_DIM
    lane_r = lax.broadcasted_iota(jnp.int32, (hd, hd), 0)
    lane_c = lax.broadcasted_iota(jnp.int32, (hd, hd), 1)
    swap_halves = (lane_r == (lane_c + hd // 2) % hd).astype(BF16)

    def norm_rope(ref, cos_g, sin_g, out_ref, scale):
        for h in range(HEADS):
            sl = slice(h * hd, (h + 1) * hd)
            x16 = ref[:, sl]
            x = x16.astype(F32)
            inv = lax.rsqrt(jnp.mean(x * x, axis=-1, keepdims=True) + NORM_EPS) * scale
            out_ref[:, sl] = ((x * cos_g + _dot(x16, swap_halves) * sin_g) * inv).astype(out_ref.dtype)

    norm_rope(q_ref, qc_ref[...], qs_ref[...], qo_ref, hd ** -0.5 * LOG2_E)
    norm_rope(k_ref, kc_ref[...], ks_ref[...], ko_ref, 1.0)
    w = v_ref.shape[1]
    eye = (lax.broadcasted_iota(jnp.int32, (w, w), 0) == lax.broadcasted_iota(jnp.int32, (w, w), 1)).astype(BF16)
    vt_ref[...] = _nt_dot(eye, v_ref[...]).astype(vt_ref.dtype)

    sm = sm_ref[...]
    iq_w = HEADS * IDX_DIM
    half = IDX_DIM // 2
    iq = sm[:, :iq_w]
    iq_r = (iq * icos_ref[...] + pltpu.roll(iq, iq_w - half, axis=1) * isin_up_ref[...]
            + pltpu.roll(iq, half, axis=1) * isin_dn_ref[...])
    qio_ref[...] = iq_r.astype(qio_ref.dtype)
    last = sm[:, iq_w:]
    lw = last.shape[1]
    ik_r = (last * icos_ref[:, :lw] + pltpu.roll(last, lw - half, axis=1) * isin_up_ref[:, :lw]
            + pltpu.roll(last, half, axis=1) * isin_dn_ref[:, :lw])
    lane = lax.broadcasted_iota(jnp.int32, ik_r.shape, 1)
    ik_r = jnp.where(lane < IDX_DIM, ik_r, 0.0)
    kia_ref[...] = ik_r.astype(kia_ref.dtype)
    kib_ref[...] = pltpu.roll(ik_r, IDX_DIM, axis=1).astype(kib_ref.dtype)
    wt = last.T
    wt_ref[...] = wt[SM_IDXW:SM_IDXW + 8, :] * (HEADS ** -0.5 * IDX_DIM ** -0.5)


def _attn_prep(big3d, small3d, tabs):
    b, t, _ = big3d.shape
    w = BRANCH_W
    tt = PREP_TT
    iq_w = HEADS * IDX_DIM
    seq = lambda col: pl.BlockSpec((None, tt, w), lambda j, i, col=col: (i, j, col // w))
    tab = lambda width: pl.BlockSpec((tt, width), lambda j, i: (j, 0))
    return pl.pallas_call(
        _attn_prep_kernel,
        grid=(t // tt, b),
        in_specs=[
            seq(COL_QC), seq(COL_KC), seq(COL_VC),
            pl.BlockSpec((None, tt, SMALL_W), lambda j, i: (i, j, 0)),
            tab(HEAD_DIM), tab(HEAD_DIM), tab(HEAD_DIM), tab(HEAD_DIM), tab(iq_w), tab(iq_w), tab(iq_w),
        ],
        out_specs=[
            pl.BlockSpec((None, tt, w), lambda j, i: (i, j, 0)),
            pl.BlockSpec((None, tt, w), lambda j, i: (i, j, 0)),
            pl.BlockSpec((None, w, tt), lambda j, i: (i, 0, j)),
            pl.BlockSpec((None, tt, iq_w), lambda j, i: (i, j, 0)),
            pl.BlockSpec((None, tt, 2 * IDX_DIM), lambda j, i: (i, j, 0)),
            pl.BlockSpec((None, tt, 2 * IDX_DIM), lambda j, i: (i, j, 0)),
            pl.BlockSpec((None, 8, tt), lambda j, i: (i, 0, j)),
        ],
        out_shape=[
            jax.ShapeDtypeStruct((b, t, w), BF16),
            jax.ShapeDtypeStruct((b, t, w), BF16),
            jax.ShapeDtypeStruct((b, w, t), BF16),
            jax.ShapeDtypeStruct((b, t, iq_w), BF16),
            jax.ShapeDtypeStruct((b, t, 2 * IDX_DIM), BF16),
            jax.ShapeDtypeStruct((b, t, 2 * IDX_DIM), BF16),
            jax.ShapeDtypeStruct((b, 8, t), F32),
        ],
        compiler_params=_cparams(2),
        name="attn_prep",
    )(big3d, big3d, big3d, small3d, *tabs)


def _dsa_kernel(q_ref, k_ref, vt_ref, qi_ref, kia_ref, kib_ref, wt_ref, z_ref, y_ref, s_ref, acc_ref, *, topk):
    qb, kc = ATT_QB, ATT_KC
    blk_i = pl.program_id(1)
    n_kc = blk_i + 1
    q_pos = blk_i * qb + lax.broadcasted_iota(jnp.int32, (1, qb), 1)
    qi = qi_ref[...]
    wt = wt_ref[...]
    inf = jnp.inf

    def col_sum(x):
        return x.reshape(kc // 8, 8, qb).sum(axis=0)

    def total(x):
        return jnp.sum(x, axis=0, keepdims=True)

    def score_chunk(ci, carry, diagonal, width=kc):
        vmax, vmin, min_pos, n_pos, n_nonneg = carry
        off = pl.multiple_of(ci * width, width)
        ka = kia_ref[pl.ds(off, width), :]
        kb = kib_ref[pl.ds(off, width), :]
        s = jnp.zeros((width, qb), F32)
        for h in range(HEADS):
            kk = ka if h % 2 == 0 else kb
            qq = qi[:, (h // 2) * 2 * IDX_DIM:(h // 2 + 1) * 2 * IDX_DIM]
            s = s + jnp.maximum(_nt_dot(kk, qq), 0.0) * wt[h:h + 1, :]
        s = jnp.where(s == 0.0, 0.0, s)
        if diagonal:
            key_pos = off + lax.broadcasted_iota(jnp.int32, (width, 1), 0)
            causal = key_pos <= q_pos
            sc = jnp.where(causal, s, -inf)
            s_hi = jnp.where(causal, s, inf)
        else:
            sc = s_hi = s
        s_ref[pl.ds(off, width), :] = sc
        vmax = jnp.maximum(vmax, jnp.max(sc, axis=0, keepdims=True))
        vmin = jnp.minimum(vmin, jnp.min(s_hi, axis=0, keepdims=True))
        min_pos = jnp.minimum(min_pos, jnp.min(jnp.where(sc > 0.0, sc, inf), axis=0, keepdims=True))
        n_pos = n_pos + jnp.where(sc > 0.0, 1.0, 0.0).reshape(width // 8, 8, qb).sum(axis=0)
        n_nonneg = n_nonneg + jnp.where(sc >= 0.0, 1.0, 0.0).reshape(width // 8, 8, qb).sum(axis=0)
        return vmax, vmin, min_pos, n_pos, n_nonneg

    row_inf = jnp.full((1, qb), inf, F32)
    zeros8 = jnp.zeros((8, qb), F32)
    stats = lax.fori_loop(0, blk_i // 2, functools.partial(score_chunk, diagonal=False, width=2 * kc),
                          (-row_inf, row_inf, row_inf, zeros8, zeros8))
    stats = lax.fori_loop(blk_i - blk_i % 2, blk_i, functools.partial(score_chunk, diagonal=False), stats)
    vmax, vmin, min_pos, n_pos, n_nonneg = score_chunk(blk_i, stats, diagonal=True)
    n_pos = total(n_pos)
    n_nonneg = total(n_nonneg)

    def count_ge(thr):
        def body(ci, acc):
            off = pl.multiple_of(ci * kc, kc)
            hit = jnp.where(s_ref[pl.ds(off, kc), :] >= thr, 1.0, 0.0)
            return acc + hit.reshape(kc // (8 * COUNT_LANES), COUNT_LANES * 8, qb).sum(axis=0)
        return total(lax.fori_loop(0, n_kc, body, jnp.zeros((COUNT_LANES * 8, qb), F32)))

    k_sel = jnp.minimum(q_pos + 1, topk).astype(F32)
    n_causal = (q_pos + 1).astype(F32)
    at_zero = (n_pos < k_sel) & (k_sel <= n_nonneg)
    above = k_sel <= n_pos
    lo = jnp.where(at_zero, 0.0, jnp.where(above, min_pos, vmin))
    hi = jnp.where(at_zero, min_pos, jnp.where(above, vmax + (jnp.abs(vmax) + 1.0), 0.0))
    c_lo = jnp.where(at_zero, n_nonneg, jnp.where(above, n_pos, n_causal))
    c_hi = jnp.where(at_zero, n_pos, jnp.where(above, 0.0, n_nonneg))
    done = jnp.where(at_zero | (c_lo == k_sel), 1.0, 0.0)

    def n_open(d):
        return jnp.sum(1.0 - d)

    def bisect_cond(carry):
        it, n_left = carry[0], carry[1]
        return (it < NUM_BISECT) & (n_left > 0.0)

    def bisect_body(carry):
        it, _, lo, hi, c_lo, c_hi, done = carry
        for _ in range(BISECT_UNROLL):
            mid = 0.5 * lo + 0.5 * hi
            cnt = count_ge(mid)
            live = done < 0.5
            up = (cnt >= k_sel) & live
            dn = (cnt < k_sel) & live
            lo = jnp.where(up, mid, lo)
            c_lo = jnp.where(up, cnt, c_lo)
            hi = jnp.where(dn, mid, hi)
            c_hi = jnp.where(dn, cnt, c_hi)
            done = jnp.where(c_lo == k_sel, 1.0, done)
        return it + BISECT_UNROLL, n_open(done), lo, hi, c_lo, c_hi, done

    _, _, lo, hi, c_lo, c_hi, done = lax.while_loop(
        bisect_cond, bisect_body, (jnp.int32(0), n_open(done), lo, hi, c_lo, c_hi, done))
    need = k_sel - c_hi
    n_tied = jnp.sum(jnp.where(c_lo - c_hi > need, 1.0, 0.0))

    def tie_mask():
        tri = (lax.broadcasted_iota(jnp.int32, (kc, kc), 1)
               < lax.broadcasted_iota(jnp.int32, (kc, kc), 0)).astype(BF16)

        def mask_body(ci, seen):
            off = pl.multiple_of(ci * kc, kc)
            blk = s_ref[pl.ds(off, kc), :]
            tie = jnp.where((blk >= lo) & (blk < hi), 1.0, 0.0)
            rank = _dot(tri, tie.astype(BF16)) + seen
            sel = (blk >= hi) | ((tie > 0.5) & (rank < need))
            s_ref[pl.ds(off, kc), :] = jnp.where(sel, 1.0, 0.0)
            return seen + total(col_sum(tie))

        lax.fori_loop(0, n_kc, mask_body, jnp.zeros((1, qb), F32))

    pl.when(n_tied > 0.0)(tie_mask)
    sel_thr = jnp.where(n_tied > 0.0, 0.5, lo)

    hs = range(HEADS)
    sls = [slice(h * HEAD_DIM, (h + 1) * HEAD_DIM) for h in hs]
    qh = [q_ref[:, sl] for sl in sls]
    acc_ref[...] = jnp.zeros(acc_ref.shape, F32)

    def att_body(ci, carry):
        ms, ls = carry
        off = pl.multiple_of(ci * kc, kc)
        sel = s_ref[pl.ds(off, kc), :] >= sel_thr
        lm = [jnp.where(sel, _nt_dot(k_ref[pl.ds(off, kc), sls[h]], qh[h]), MASK_NEG) for h in hs]
        m_new = [jnp.maximum(ms[h], jnp.max(lm[h], axis=0, keepdims=True)) for h in hs]
        p = [jnp.exp2(lm[h] - m_new[h]) for h in hs]
        alpha = [jnp.exp2(ms[h] - m_new[h]) for h in hs]
        l_new = [alpha[h] * ls[h] + jnp.sum(p[h], axis=0, keepdims=True) for h in hs]
        pv = [_dot(vt_ref[sls[h], pl.ds(off, kc)], p[h].astype(BF16)) for h in hs]
        for h in hs:
            acc_ref[sls[h], :] = alpha[h] * acc_ref[sls[h], :] + pv[h]
        return tuple(m_new), tuple(l_new)

    row_neg = jnp.full((1, qb), MASK_NEG, F32)
    row_zero = jnp.zeros((1, qb), F32)
    _, ls = lax.fori_loop(0, n_kc, att_body, ((row_neg,) * HEADS, (row_zero,) * HEADS))
    for h in hs:
        o = (acc_ref[sls[h], :] / ls[h]).T
        y_ref[:, sls[h]] = (o * _silu(z_ref[:, sls[h]].astype(F32))).astype(y_ref.dtype)


def _dsa_branch(big3d, q_r, k_r, v_t, qi_r, ki_a, ki_b, w_t):
    b, t, _ = big3d.shape
    w = BRANCH_W
    qb = ATT_QB
    topk = min(TOPK_MAX, t // 4)
    return pl.pallas_call(
        functools.partial(_dsa_kernel, topk=topk),
        grid=(b, t // qb),
        in_specs=[
            pl.BlockSpec((None, qb, w), lambda i, j: (i, j, 0)),
            pl.BlockSpec((None, t, w), lambda i, j: (i, 0, 0)),
            pl.BlockSpec((None, w, t), lambda i, j: (i, 0, 0)),
            pl.BlockSpec((None, qb, HEADS * IDX_DIM), lambda i, j: (i, j, 0)),
            pl.BlockSpec((None, t, 2 * IDX_DIM), lambda i, j: (i, 0, 0)),
            pl.BlockSpec((None, t, 2 * IDX_DIM), lambda i, j: (i, 0, 0)),
            pl.BlockSpec((None, 8, qb), lambda i, j: (i, 0, j)),
            pl.BlockSpec((None, qb, w), lambda i, j: (i, j, COL_ZC // w)),
        ],
        out_specs=pl.BlockSpec((None, qb, w), lambda i, j: (i, j, 0)),
        out_shape=jax.ShapeDtypeStruct((b, t, w), BF16),
        scratch_shapes=[pltpu.VMEM((t, qb), F32),
                        pltpu.VMEM((w, qb), F32)],
        compiler_params=_cparams(2),
        name="sparse_attention",
    )(q_r, k_r, v_t, qi_r, ki_a, ki_b, w_t, big3d)


def _merge_kernel(ya_ref, yb_ref, yc_ref, ga_ref, gb_ref, gc_ref, bias_ref, wb_ref, wo_ref, x_ref, o_ref):
    merged = None
    for n, (y_ref, g_ref) in enumerate(((ya_ref, ga_ref), (yb_ref, gb_ref), (yc_ref, gc_ref))):
        proj = _dot(y_ref[...], wb_ref[n])
        gate = jax.nn.sigmoid(g_ref[...].astype(F32) + bias_ref[n:n + 1, :])
        term = gate * proj
        merged = term if merged is None else merged + term
    o_ref[...] = x_ref[...] + _dot(merged.astype(BF16), wo_ref[...])


def _merge(ya, yb, yc, big2d, gate_b, w_branch, w_out, x2d):
    m = x2d.shape[0]
    tm = MERGE_TM
    yspec = pl.BlockSpec((tm, BRANCH_W), lambda i: (i, 0))
    gspec = lambda n: pl.BlockSpec((tm, D_MODEL), lambda i, n=n: (i, COL_GATES // D_MODEL + n))
    return pl.pallas_call(
        _merge_kernel,
        grid=(m // tm,),
        in_specs=[
            yspec, yspec, yspec, gspec(0), gspec(1), gspec(2),
            pl.BlockSpec((N_BRANCH, D_MODEL), lambda i: (0, 0)),
            pl.BlockSpec((N_BRANCH, BRANCH_W, D_MODEL), lambda i: (0, 0, 0)),
            pl.BlockSpec((D_MODEL, D_MODEL), lambda i: (0, 0)),
            pl.BlockSpec((tm, D_MODEL), lambda i: (i, 0)),
        ],
        out_specs=pl.BlockSpec((tm, D_MODEL), lambda i: (i, 0)),
        out_shape=jax.ShapeDtypeStruct((m, D_MODEL), F32),
        compiler_params=_cparams(1),
        name="merge",
    )(ya, yb, yc, big2d, big2d, big2d, gate_b, w_branch, w_out, x2d)


_IN_SPLIT = (("dn_qkvz", 4 * BRANCH_W), ("beta_decay", 2 * HEADS), ("pool_uz", 2 * BRANCH_W),
             ("att_qkvz", 4 * BRANCH_W), ("idx_q", HEADS * IDX_DIM), ("idx_k", IDX_DIM), ("idx_w", HEADS),
             ("gates", N_BRANCH * D_MODEL))
_IN_OFFSET = {name: sum(w for _, w in _IN_SPLIT[:i]) for i, (name, _) in enumerate(_IN_SPLIT)}

PERM_COLS = 128
_PERM_REGULAR = (BIG_W + HEADS * IDX_DIM) // PERM_COLS


def _perm_source(blk):
    shift = jnp.where(blk < COL_UB // PERM_COLS, _IN_OFFSET["dn_qkvz"] - COL_QA,
                      jnp.where(blk < COL_GATES // PERM_COLS, _IN_OFFSET["pool_uz"] - COL_UB,
                                jnp.where(blk < BIG_W // PERM_COLS, _IN_OFFSET["gates"] - COL_GATES,
                                          _IN_OFFSET["idx_q"] - BIG_W)))
    return jnp.where(blk < _PERM_REGULAR, blk * PERM_COLS + shift, 0)


def _permute_kernel(w_ref, ik_ref, bd_ref, iw_ref, o_ref):
    blk = pl.program_id(0)
    depth = o_ref.shape[0]

    @pl.when(blk < _PERM_REGULAR)
    def _():
        for l in range(depth):
            o_ref[l] = w_ref[:, l, :].T.astype(o_ref.dtype)

    @pl.when(blk == _PERM_REGULAR)
    def _():
        row = lax.broadcasted_iota(jnp.int32, (8, o_ref.shape[1]), 0)
        for l in range(depth):
            idx_w = jnp.where(row < HEADS, iw_ref[:, l, :], 0.0)
            pad = jnp.zeros((PERM_COLS - IDX_DIM - 16, o_ref.shape[1]), F32)
            rows = jnp.concatenate([ik_ref[:, l, :], bd_ref[:, l, :], idx_w, pad], axis=0)
            o_ref[l] = rows.T.astype(o_ref.dtype)


def _permute_w_in(w_in):
    depth, d, _ = w_in.shape
    w_t = jnp.transpose(w_in, (2, 0, 1))
    el = pl.Element
    fixed = lambda rows, start: pl.BlockSpec((el(rows), el(depth), el(d)), lambda i: (start, 0, 0))
    return pl.pallas_call(
        _permute_kernel,
        grid=(_PERM_REGULAR + 1,),
        in_specs=[
            pl.BlockSpec((el(PERM_COLS), el(depth), el(d)), lambda i: (_perm_source(i), 0, 0)),
            fixed(IDX_DIM, _IN_OFFSET["idx_k"]), fixed(8, _IN_OFFSET["beta_decay"]), fixed(8, _IN_OFFSET["idx_w"]),
        ],
        out_specs=pl.BlockSpec((depth, d, PERM_COLS), lambda i: (0, 0, i)),
        out_shape=jax.ShapeDtypeStruct((depth, d, BIG_W + SMALL_W), BF16),
        compiler_params=_cparams(1),
        name="permute_w_in",
    )(w_t, w_t, w_t, w_t)


def _rope_tables(t):
    def base(dim):
        inv_freq = ROPE_THETA ** (-jnp.arange(0, dim, 2, dtype=F32) / dim)
        ang = jnp.arange(t, dtype=F32)[:, None] * inv_freq[None, :]
        return jnp.cos(ang), jnp.sin(ang)

    cos_a, sin_a = base(HEAD_DIM)
    cos = jnp.concatenate([cos_a, cos_a], axis=-1)
    sin = jnp.concatenate([-sin_a, sin_a], axis=-1)
    cos_i, sin_i = base(IDX_DIM)
    zero = jnp.zeros_like(sin_i)
    icos = jnp.tile(jnp.concatenate([cos_i, cos_i], axis=-1), (1, HEADS))
    isin_up = jnp.tile(jnp.concatenate([-sin_i, zero], axis=-1), (1, HEADS))
    isin_dn = jnp.tile(jnp.concatenate([zero, sin_i], axis=-1), (1, HEADS))
    return cos, sin, icos, isin_up, isin_dn


def _lane_row(vals, offset):
    d, n = vals.shape
    return jnp.zeros((d, 1, HEAD_DIM), F32).at[:, 0, offset:offset + n].set(vals.astype(F32))


def kernel(x, norm_g, w_in, gate_b, conv_w, a_log, dt_bias, dn_onorm, pool_w, pool_scale, q_norm, k_norm,
           w_branch, w_out):
    b, t, d = x.shape
    depth = norm_g.shape[0]
    w_p = _permute_w_in(w_in)
    wb16 = w_branch.astype(BF16)
    wo16 = w_out.astype(BF16)
    pw16 = pool_w.astype(BF16)
    alog_rows = _lane_row(a_log, SM_DECAY)
    dtb_rows = _lane_row(dt_bias, SM_DECAY)
    cos, sin, *idx_tabs = _rope_tables(t)
    idx_tabs = tuple(idx_tabs)
    x2d = x.reshape(b * t, d)
    for layer in range(depth):
        big2d, small2d = _in_projection(x2d, norm_g[layer][None, :], w_p, conv_w[layer], layer, t // PROJ_TM)
        big3d = big2d.reshape(b, t, BIG_W)
        small3d = small2d.reshape(b, t, SMALL_W)
        ya = _deltanet_branch(big3d, small3d, alog_rows[layer], dtb_rows[layer], dn_onorm[layer][None, :])
        yb = _pool_branch(big3d, pw16[layer], pool_scale[layer][None, :])
        half = HEAD_DIM // 2
        gain_tabs = (cos * q_norm[layer], sin * jnp.roll(q_norm[layer], half),
                     cos * k_norm[layer], sin * jnp.roll(k_norm[layer], half))
        prep = _attn_prep(big3d, small3d, gain_tabs + idx_tabs)
        yc = _dsa_branch(big3d, *prep)
        x2d = _merge(ya.reshape(b * t, BRANCH_W), yb.reshape(b * t, BRANCH_W), yc.reshape(b * t, BRANCH_W),
                     big2d, gate_b[layer], wb16[layer], wo16[layer], x2d)
    return x2d.reshape(b, t, d)
```

```python
import functools

import jax
import jax.numpy as jnp
from jax import lax
from jax.experimental import pallas as pl
from jax.experimental.pallas import tpu as pltpu

F32 = jnp.float32
BF16 = jnp.bfloat16

D_MODEL = 1024
HEADS = 4
HEAD_DIM = 128
BRANCH_W = HEADS * HEAD_DIM
DN_CONV = 4
POOL_WINDOWS = (2, 4, 8, 16)
POOL_GROUP = 128
IDX_DIM = 64
TOPK_MAX = 256
ROPE_THETA = 10000.0
NORM_EPS = 1e-6
N_BRANCH = 3

COL_QA, COL_KA, COL_VA, COL_ZA = 0, 512, 1024, 1536
COL_UB, COL_ZB = 2048, 2560
COL_QC, COL_KC, COL_VC, COL_ZC = 3072, 3584, 4096, 4608
COL_GATES = 5120
BIG_W = 8192
SMALL_W = 384
SM_BETA, SM_DECAY, SM_IDXW = 64, 68, 72

V7X_VMEM_LIMIT = 56 * 1024 * 1024

PROJ_TM = 512
PROJ_TN = 512
MERGE_TM = 1024
PREP_TT = 1024
POOL_TT = 256
POOL_HALO = 16
DN_CHUNK = 128
CONV_HALO = 8
CONV_ROWS = 128
DN_BASE = 8
DN_PAR = 2
ATT_QB = 256
ATT_KC = 256
NUM_BISECT = 36
BISECT_UNROLL = 4
COUNT_LANES = 4
TRIP_CHUNKS = (4, 2, 1)
MASK_NEG = -1e30
LOG2_E = 1.4426950408889634


def _nt_dot(a, b):
    return lax.dot_general(a, b, (((1,), (1,)), ((), ())), preferred_element_type=F32)


def _dot(a, b):
    return jnp.dot(a, b, preferred_element_type=F32)


def _silu(x):
    return x * jax.nn.sigmoid(x)


def _cparams(n_axes):
    return pltpu.CompilerParams(dimension_semantics=("arbitrary",) * n_axes,
                                vmem_limit_bytes=V7X_VMEM_LIMIT)


def _inproj_kernel(x_ref, g_ref, w_ref, cw_ref, big_ref, small_ref, conv_buf, *, tiles_per_seq):
    step = pl.program_id(0)

    @pl.when(step == 0)
    def _():
        conv_buf[...] = jnp.zeros(conv_buf.shape, F32)

    x = x_ref[...]
    ms = jnp.mean(x * x, axis=-1, keepdims=True)
    h = ((x * lax.rsqrt(ms + NORM_EPS)) * g_ref[...]).astype(BF16)
    seq_start = (step % tiles_per_seq) == 0
    hd = HEAD_DIM
    halo = CONV_HALO

    def stage(seg):
        c = seg * PROJ_TN
        prev = jnp.where(seq_start, 0.0, conv_buf[seg, PROJ_TM:PROJ_TM + halo, :])
        conv_buf[seg, 0:halo, :] = prev
        conv_buf[seg, halo:PROJ_TM + halo, :] = _dot(h, w_ref[:, c:c + PROJ_TN])

    def conv_piece(seg, r, k, anchor):
        c = seg * PROJ_TN + k * hd
        cw = cw_ref[:, c:c + hd] + anchor
        rows = conv_buf[seg, r:r + CONV_ROWS + halo, k * hd:(k + 1) * hd]
        acc = rows * cw[DN_CONV - 1:DN_CONV, :]
        for s in range(1, DN_CONV):
            acc = acc + pltpu.roll(rows, s, axis=0) * cw[DN_CONV - 1 - s:DN_CONV - s, :]
        z = _silu(acc[halo:, :])
        if c < COL_VA:
            scale = hd ** -0.5 if c < COL_KA else 1.0
            z = z * (lax.rsqrt(jnp.sum(z * z, axis=-1, keepdims=True) + NORM_EPS) * scale)
        big_ref[r:r + CONV_ROWS, c:c + hd] = z.astype(big_ref.dtype)

    plain = list(range(COL_ZA, BIG_W, PROJ_TN))
    n_slot = 4
    for seg in range(COL_ZA // PROJ_TN):
        stage(seg)
        pieces = [(seg, r, k) for r in range(0, PROJ_TM, CONV_ROWS) for k in range(HEADS)]
        per_slot = -(-len(pieces) // n_slot)
        for _ in range(n_slot):
            c = plain.pop(0)
            y = _dot(h, w_ref[:, c:c + PROJ_TN])
            big_ref[:, c:c + PROJ_TN] = y.astype(big_ref.dtype)
            anchor = y[PROJ_TM - 1:PROJ_TM, 0:hd] * 0.0
            for piece in pieces[:per_slot]:
                conv_piece(*piece, anchor)
            pieces = pieces[per_slot:]
    for c in plain:
        big_ref[:, c:c + PROJ_TN] = _dot(h, w_ref[:, c:c + PROJ_TN]).astype(big_ref.dtype)
    small_ref[...] = _dot(h, w_ref[:, BIG_W:])


def _in_projection(x2d, g_row, w_p, conv_w, layer, tiles_per_seq):
    m = x2d.shape[0]
    return pl.pallas_call(
        functools.partial(_inproj_kernel, tiles_per_seq=tiles_per_seq),
        grid=(m // PROJ_TM,),
        in_specs=[
            pl.BlockSpec((PROJ_TM, D_MODEL), lambda i: (i, 0)),
            pl.BlockSpec((1, D_MODEL), lambda i: (0, 0)),
            pl.BlockSpec((None, D_MODEL, BIG_W + SMALL_W), lambda i: (layer, 0, 0), pipeline_mode=pl.Buffered(1)),
            pl.BlockSpec((DN_CONV, 3 * BRANCH_W), lambda i: (0, 0)),
        ],
        out_specs=[
            pl.BlockSpec((PROJ_TM, BIG_W), lambda i: (i, 0)),
            pl.BlockSpec((PROJ_TM, SMALL_W), lambda i: (i, 0)),
        ],
        out_shape=[
            jax.ShapeDtypeStruct((m, BIG_W), BF16),
            jax.ShapeDtypeStruct((m, SMALL_W), F32),
        ],
        scratch_shapes=[pltpu.VMEM((3, PROJ_TM + CONV_HALO, BRANCH_W), F32)],
        compiler_params=_cparams(1),
        name="in_projection",
    )(x2d, g_row, w_p, conv_w)


def _pool_kernel(u_ref, z_ref, pw_ref, ps_ref, y_ref):
    t_len = u_ref.shape[0]
    for t in range(t_len // POOL_TT):
        r0 = t * POOL_TT
        cur = u_ref[r0:r0 + POOL_TT, :].astype(F32)
        if t == 0:
            prev = jnp.zeros((POOL_HALO, cur.shape[1]), F32)
        else:
            prev = u_ref[r0 - POOL_HALO:r0, :].astype(F32)
        win_rows = jnp.concatenate([prev, cur], axis=0)
        pos = r0 + lax.broadcasted_iota(jnp.int32, (POOL_TT, 1), 0)
        for gi, win in enumerate(POOL_WINDOWS):
            sl = slice(gi * POOL_GROUP, (gi + 1) * POOL_GROUP)
            s = win_rows[:, sl]
            shift = 1
            while shift < win:
                s = s + pltpu.roll(s, shift, axis=0)
                shift *= 2
            count = jnp.minimum(pos + 1, win).astype(F32)
            pooled = s[POOL_HALO:, :] / count - cur[:, sl]
            mixed = _dot(pooled.astype(BF16), pw_ref[gi])
            zg = z_ref[r0:r0 + POOL_TT, sl].astype(F32)
            y_ref[r0:r0 + POOL_TT, sl] = (mixed * ps_ref[:, sl] * _silu(zg)).astype(y_ref.dtype)


def _pool_branch(big3d, pool_w, pool_scale_row):
    b, t, _ = big3d.shape
    width = POOL_GROUP * len(POOL_WINDOWS)
    return pl.pallas_call(
        _pool_kernel,
        grid=(b,),
        in_specs=[
            pl.BlockSpec((None, t, width), lambda i: (i, 0, COL_UB // width)),
            pl.BlockSpec((None, t, width), lambda i: (i, 0, COL_ZB // width)),
            pl.BlockSpec((len(POOL_WINDOWS), POOL_GROUP, POOL_GROUP), lambda i: (0, 0, 0)),
            pl.BlockSpec((1, width), lambda i: (0, 0)),
        ],
        out_specs=pl.BlockSpec((None, t, width), lambda i: (i, 0, 0)),
        out_shape=jax.ShapeDtypeStruct((b, t, width), BF16),
        compiler_params=_cparams(1),
        name="pool_branch",
    )(big3d, big3d, pool_w, pool_scale_row)


def _softplus(x):
    return jnp.maximum(x, 0.0) + jnp.log1p(jnp.exp(-jnp.abs(x)))


def _split_bf16(a):
    hi = a.astype(BF16)
    lo = (a - hi.astype(F32)).astype(BF16)
    return hi, lo


def _dot_split(a, b):
    a_hi, a_lo = a
    b_hi, b_lo = b
    return _dot(jnp.concatenate([a_hi, a_lo, a_hi], axis=1), jnp.concatenate([b_hi, b_hi, b_lo], axis=0))


def _dn_kernel(q_ref, k_ref, v_ref, z_ref, sm_ref, alog_ref, dtb_ref, on_ref,
               y_ref, u_s, w_s, qg_s, a_s, kdt_s, el_s, st_s):
    t_len = q_ref.shape[0]
    n_chunks = t_len // DN_CHUNK
    c = DN_CHUNK
    hd = HEAD_DIM
    row = lax.broadcasted_iota(jnp.int32, (c, c), 0)
    col = lax.broadcasted_iota(jnp.int32, (c, c), 1)
    tril = row >= col
    strict = row > col
    tril16 = tril.astype(BF16)
    tril16x3 = jnp.concatenate([tril16, tril16, tril16], axis=1)
    eye_f = (row == col).astype(F32)
    base_blk = (row // DN_BASE) == (col // DN_BASE)
    pair_blks = []
    size = DN_BASE
    while size < c:
        pair_blks.append(((row // (2 * size)) == (col // (2 * size))) & ((row // size) != (col // size)))
        size *= 2

    def prepare(gi, carry):
        cis = [gi * DN_PAR + j for j in range(DN_PAR)]
        starts = [pl.multiple_of(ci * c, c) for ci in cis]
        beta_all, gc_all = [], []
        for start in starts:
            sm = sm_ref[pl.ds(start, c), :]
            beta_all.append(jax.nn.sigmoid(sm))
            g_all = -jnp.exp(alog_ref[...]) * _softplus(sm + dtb_ref[...])
            g_hi = g_all.astype(BF16)
            g_r = g_all - g_hi.astype(F32)
            g_mid = g_r.astype(BF16)
            g_lo = (g_r - g_mid.astype(F32)).astype(BF16)
            gc_all.append(_dot(tril16x3, jnp.concatenate([g_hi, g_mid, g_lo], axis=0)))
        items = [(j, h) for j in range(DN_PAR) for h in range(HEADS)]
        ids = range(len(items))
        sls = [slice(h * hd, (h + 1) * hd) for _, h in items]
        qn = [q_ref[pl.ds(starts[j], c), sls[i]].astype(F32) for i, (j, _) in enumerate(items)]
        kn = [k_ref[pl.ds(starts[j], c), sls[i]].astype(F32) for i, (j, _) in enumerate(items)]
        beta_b = [jnp.broadcast_to(beta_all[j][:, SM_BETA + h:SM_BETA + h + 1], (c, hd)) for j, h in items]
        gc = [jnp.broadcast_to(gc_all[j][:, SM_DECAY + h:SM_DECAY + h + 1], (c, hd)) for j, h in items]
        decay = [jnp.exp(jnp.where(tril, gc[i] - gc[i].T, -jnp.inf)) for i in ids]
        kb = [kn[i] * beta_b[i] for i in ids]
        kn16 = [kn[i].astype(BF16) for i in ids]
        kq = [_nt_dot(jnp.concatenate([kb[i].astype(BF16), qn[i].astype(BF16)], axis=0), kn16[i]) for i in ids]
        lmat = [jnp.where(strict, kq[i][:c, :] * decay[i], 0.0) for i in ids]
        l_hi = [lmat[i].astype(BF16) for i in ids]
        l_lo = [(lmat[i] - l_hi[i].astype(F32)).astype(BF16) for i in ids]
        zero16 = jnp.zeros((c, c), BF16)
        d_parts = [(jnp.where(base_blk, l_hi[i], zero16), jnp.where(base_blk, l_lo[i], zero16)) for i in ids]
        tmat = [eye_f - jnp.where(base_blk, lmat[i], 0.0) for i in ids]
        power = [_dot_split(d_parts[i], d_parts[i]) for i in ids]
        span = 2
        while span < DN_BASE:
            parts = [_split_bf16(power[i]) for i in ids]
            t_parts = [_split_bf16(tmat[i]) for i in ids]
            span *= 2
            if span < DN_BASE:
                both = [_dot_split((jnp.concatenate([t_parts[i][0], parts[i][0]], axis=0),
                                    jnp.concatenate([t_parts[i][1], parts[i][1]], axis=0)), parts[i]) for i in ids]
                tmat = [tmat[i] + both[i][:c, :] for i in ids]
                power = [both[i][c:, :] for i in ids]
            else:
                tmat = [tmat[i] + _dot_split(t_parts[i], parts[i]) for i in ids]
        for pair_blk in pair_blks:
            t_parts = [_split_bf16(tmat[i]) for i in ids]
            off = [(jnp.where(pair_blk, l_hi[i], zero16), jnp.where(pair_blk, l_lo[i], zero16)) for i in ids]
            cx = [_dot_split(off[i], t_parts[i]) for i in ids]
            tmat = [tmat[i] - _dot_split(t_parts[i], _split_bf16(cx[i])) for i in ids]
        for i, (j, h) in enumerate(items):
            sl = sls[i]
            start = starts[j]
            t16 = tmat[i].astype(BF16)
            egc = jnp.exp(gc[i])
            vb = v_ref[pl.ds(start, c), sl].astype(F32) * beta_b[i]
            uw = _dot(t16, jnp.concatenate([vb.astype(BF16), (kb[i] * egc).astype(BF16)], axis=1))
            u_s[pl.ds(start, c), sl] = uw[:, :hd]
            w_s[pl.ds(start, c), sl] = uw[:, hd:].astype(BF16)
            qg_s[pl.ds(start, c), sl] = (qn[i] * egc).astype(BF16)
            a_s[pl.ds(start, c), sl] = jnp.where(tril, kq[i][c:, :] * decay[i], 0.0).astype(BF16)
            g_last = gc[i][c - 1:c, :]
            kd = kn[i] * jnp.exp(g_last - gc[i])
            kdt_s[pl.ds(start, c), sl] = kd.T.astype(BF16)
            el_s[pl.ds(pl.multiple_of(cis[j] * 8, 8), 8), sl] = jnp.broadcast_to(jnp.exp(g_last), (8, hd))
        return carry

    def scan_chunk(ci):
        start = pl.multiple_of(ci * c, c)
        hs = range(HEADS)
        sls = [slice(h * hd, (h + 1) * hd) for h in hs]
        state = [st_s[h] for h in hs]
        s16 = [state[h].astype(BF16) for h in hs]
        v_new = [u_s[pl.ds(start, c), sls[h]] - _dot(w_s[pl.ds(start, c), sls[h]], s16[h]) for h in hs]
        v16 = [v_new[h].astype(BF16) for h in hs]
        for h in hs:
            e_last = el_s[pl.ds(pl.multiple_of(ci * 8, 8), 8), sls[h]][0:1, :]
            st_s[h] = state[h] * e_last + _dot(kdt_s[pl.ds(start, c), sls[h]], v16[h])
        for h in hs:
            sl = sls[h]
            o = _dot(qg_s[pl.ds(start, c), sl], s16[h]) + _dot(a_s[pl.ds(start, c), sl], v16[h])
            on = o * lax.rsqrt(jnp.mean(o * o, axis=-1, keepdims=True) + NORM_EPS) * on_ref[...]
            zg = z_ref[pl.ds(start, c), sl].astype(F32)
            y_ref[pl.ds(start, c), sl] = (on * _silu(zg)).astype(y_ref.dtype)

    def scan_group(gi):
        for j in range(DN_PAR):
            scan_chunk(gi * DN_PAR + j)

    n_groups = n_chunks // DN_PAR
    st_s[...] = jnp.zeros(st_s.shape, F32)
    prepare(0, 0)

    def trip(gi, carry):
        scan_group(gi - 1)
        return prepare(gi, carry)

    lax.fori_loop(1, n_groups, trip, 0)
    scan_group(n_groups - 1)


def _deltanet_branch(big3d, small3d, alog_row, dtb_row, onorm_row):
    b, t, _ = big3d.shape
    hd = HEAD_DIM
    w = BRANCH_W
    seq_spec = lambda col: pl.BlockSpec((None, t, w), lambda i, col=col: (i, 0, col // w))
    row_spec = pl.BlockSpec((1, hd), lambda i: (0, 0))
    return pl.pallas_call(
        _dn_kernel,
        grid=(b,),
        in_specs=[
            seq_spec(COL_QA), seq_spec(COL_KA), seq_spec(COL_VA), seq_spec(COL_ZA),
            pl.BlockSpec((None, t, hd), lambda i: (i, 0, SMALL_W // hd - 1)),
            row_spec, row_spec, row_spec,
        ],
        out_specs=pl.BlockSpec((None, t, w), lambda i: (i, 0, 0)),
        out_shape=jax.ShapeDtypeStruct((b, t, w), BF16),
        scratch_shapes=[
            pltpu.VMEM((t, w), F32),
            pltpu.VMEM((t, w), BF16),
            pltpu.VMEM((t, w), BF16),
            pltpu.VMEM((t, w), BF16),
            pltpu.VMEM((t, w), BF16),
            pltpu.VMEM((8 * t // DN_CHUNK, w), F32),
            pltpu.VMEM((HEADS, hd, hd), F32),
        ],
        compiler_params=_cparams(1),
        name="deltanet_branch",
    )(big3d, big3d, big3d, big3d, small3d, alog_row, dtb_row, onorm_row)


def _attn_prep_kernel(q_ref, k_ref, v_ref, sm_ref, qc_ref, qs_ref, kc_ref, ks_ref,
                      icos_ref, isin_up_ref, isin_dn_ref,
                      qo_ref, ko_ref, vt_ref, qio_ref, kia_ref, kib_ref, wt_ref):
    hd = HEAD_DIM
    lane_r = lax.broadcasted_iota(jnp.int32, (hd, hd), 0)
    lane_c = lax.broadcasted_iota(jnp.int32, (hd, hd), 1)
    swap_halves = (lane_r == (lane_c + hd // 2) % hd).astype(BF16)

    def norm_rope(ref, cos_g, sin_g, out_ref, scale):
        for h in range(HEADS):
            sl = slice(h * hd, (h + 1) * hd)
            x16 = ref[:, sl]
            x = x16.astype(F32)
            inv = lax.rsqrt(jnp.mean(x * x, axis=-1, keepdims=True) + NORM_EPS) * scale
            out_ref[:, sl] = ((x * cos_g + _dot(x16, swap_halves) * sin_g) * inv).astype(out_ref.dtype)

    norm_rope(q_ref, qc_ref[...], qs_ref[...], qo_ref, hd ** -0.5 * LOG2_E)
    norm_rope(k_ref, kc_ref[...], ks_ref[...], ko_ref, 1.0)
    w = v_ref.shape[1]
    eye = (lax.broadcasted_iota(jnp.int32, (w, w), 0) == lax.broadcasted_iota(jnp.int32, (w, w), 1)).astype(BF16)
    vt_ref[...] = _nt_dot(eye, v_ref[...]).astype(vt_ref.dtype)

    sm = sm_ref[...]
    iq_w = HEADS * IDX_DIM
    half = IDX_DIM // 2
    iq = sm[:, :iq_w]
    iq_r = (iq * icos_ref[...] + pltpu.roll(iq, iq_w - half, axis=1) * isin_up_ref[...]
            + pltpu.roll(iq, half, axis=1) * isin_dn_ref[...])
    qio_ref[...] = iq_r.astype(qio_ref.dtype)
    last = sm[:, iq_w:]
    lw = last.shape[1]
    ik_r = (last * icos_ref[:, :lw] + pltpu.roll(last, lw - half, axis=1) * isin_up_ref[:, :lw]
            + pltpu.roll(last, half, axis=1) * isin_dn_ref[:, :lw])
    lane = lax.broadcasted_iota(jnp.int32, ik_r.shape, 1)
    ik_r = jnp.where(lane < IDX_DIM, ik_r, 0.0)
    kia_ref[...] = ik_r.astype(kia_ref.dtype)
    kib_ref[...] = pltpu.roll(ik_r, IDX_DIM, axis=1).astype(kib_ref.dtype)
    wt = last.T
    wt_ref[...] = wt[SM_IDXW:SM_IDXW + 8, :] * (HEADS ** -0.5 * IDX_DIM ** -0.5)


def _attn_prep(big3d, small3d, tabs):
    b, t, _ = big3d.shape
    w = BRANCH_W
    tt = PREP_TT
    iq_w = HEADS * IDX_DIM
    seq = lambda col: pl.BlockSpec((None, tt, w), lambda j, i, col=col: (i, j, col // w))
    tab = lambda width: pl.BlockSpec((tt, width), lambda j, i: (j, 0))
    return pl.pallas_call(
        _attn_prep_kernel,
        grid=(t // tt, b),
        in_specs=[
            seq(COL_QC), seq(COL_KC), seq(COL_VC),
            pl.BlockSpec((None, tt, SMALL_W), lambda j, i: (i, j, 0)),
            tab(HEAD_DIM), tab(HEAD_DIM), tab(HEAD_DIM), tab(HEAD_DIM), tab(iq_w), tab(iq_w), tab(iq_w),
        ],
        out_specs=[
            pl.BlockSpec((None, tt, w), lambda j, i: (i, j, 0)),
            pl.BlockSpec((None, tt, w), lambda j, i: (i, j, 0)),
            pl.BlockSpec((None, w, tt), lambda j, i: (i, 0, j)),
            pl.BlockSpec((None, tt, iq_w), lambda j, i: (i, j, 0)),
            pl.BlockSpec((None, tt, 2 * IDX_DIM), lambda j, i: (i, j, 0)),
            pl.BlockSpec((None, tt, 2 * IDX_DIM), lambda j, i: (i, j, 0)),
            pl.BlockSpec((None, 8, tt), lambda j, i: (i, 0, j)),
        ],
        out_shape=[
            jax.ShapeDtypeStruct((b, t, w), BF16),
            jax.ShapeDtypeStruct((b, t, w), BF16),
            jax.ShapeDtypeStruct((b, w, t), BF16),
            jax.ShapeDtypeStruct((b, t, iq_w), BF16),
            jax.ShapeDtypeStruct((b, t, 2 * IDX_DIM), BF16),
            jax.ShapeDtypeStruct((b, t, 2 * IDX_DIM), BF16),
            jax.ShapeDtypeStruct((b, 8, t), F32),
        ],
        compiler_params=_cparams(2),
        name="attn_prep",
    )(big3d, big3d, big3d, small3d, *tabs)


def _dsa_kernel(q_ref, k_ref, vt_ref, qi_ref, kia_ref, kib_ref, wt_ref, z_ref, y_ref, s_ref, acc_ref, *, topk):
    qb, kc = ATT_QB, ATT_KC
    blk_i = pl.program_id(1)
    n_kc = blk_i + 1
    q_pos = blk_i * qb + lax.broadcasted_iota(jnp.int32, (1, qb), 1)
    qi = qi_ref[...]
    wt = wt_ref[...]
    inf = jnp.inf

    def col_sum(x):
        return x.reshape(kc // 8, 8, qb).sum(axis=0)

    def total(x):
        return jnp.sum(x, axis=0, keepdims=True)

    def score_chunk(ci, carry, diagonal, width=kc):
        vmax, vmin, min_pos, n_pos, n_nonneg = carry
        off = pl.multiple_of(ci * width, width)
        ka = kia_ref[pl.ds(off, width), :]
        kb = kib_ref[pl.ds(off, width), :]
        s = jnp.zeros((width, qb), F32)
        for h in range(HEADS):
            kk = ka if h % 2 == 0 else kb
            qq = qi[:, (h // 2) * 2 * IDX_DIM:(h // 2 + 1) * 2 * IDX_DIM]
            s = s + jnp.maximum(_nt_dot(kk, qq), 0.0) * wt[h:h + 1, :]
        s = jnp.where(s == 0.0, 0.0, s)
        if diagonal:
            key_pos = off + lax.broadcasted_iota(jnp.int32, (width, 1), 0)
            causal = key_pos <= q_pos
            sc = jnp.where(causal, s, -inf)
            s_hi = jnp.where(causal, s, inf)
        else:
            sc = s_hi = s
        s_ref[pl.ds(off, width), :] = sc
        vmax = jnp.maximum(vmax, jnp.max(sc, axis=0, keepdims=True))
        vmin = jnp.minimum(vmin, jnp.min(s_hi, axis=0, keepdims=True))
        min_pos = jnp.minimum(min_pos, jnp.min(jnp.where(sc > 0.0, sc, inf), axis=0, keepdims=True))
        n_pos = n_pos + jnp.where(sc > 0.0, 1.0, 0.0).reshape(width // 8, 8, qb).sum(axis=0)
        n_nonneg = n_nonneg + jnp.where(sc >= 0.0, 1.0, 0.0).reshape(width // 8, 8, qb).sum(axis=0)
        return vmax, vmin, min_pos, n_pos, n_nonneg

    row_inf = jnp.full((1, qb), inf, F32)
    zeros8 = jnp.zeros((8, qb), F32)
    stats = (-row_inf, row_inf, row_inf, zeros8, zeros8)
    done_chunks = 0
    for group in TRIP_CHUNKS:
        n_trips = (blk_i - done_chunks) // group
        first = done_chunks // group
        stats = lax.fori_loop(first, first + n_trips,
                              functools.partial(score_chunk, diagonal=False, width=group * kc), stats)
        done_chunks = done_chunks + n_trips * group
    vmax, vmin, min_pos, n_pos, n_nonneg = score_chunk(blk_i, stats, diagonal=True)
    n_pos = total(n_pos)
    n_nonneg = total(n_nonneg)

    def count_ge(thr):
        def body(ci, acc):
            off = pl.multiple_of(ci * kc, kc)
            hit = jnp.where(s_ref[pl.ds(off, kc), :] >= thr, 1.0, 0.0)
            return acc + hit.reshape(kc // (8 * COUNT_LANES), COUNT_LANES * 8, qb).sum(axis=0)
        return total(lax.fori_loop(0, n_kc, body, jnp.zeros((COUNT_LANES * 8, qb), F32)))

    k_sel = jnp.minimum(q_pos + 1, topk).astype(F32)
    n_causal = (q_pos + 1).astype(F32)
    at_zero = (n_pos < k_sel) & (k_sel <= n_nonneg)
    above = k_sel <= n_pos
    lo = jnp.where(at_zero, 0.0, jnp.where(above, min_pos, vmin))
    hi = jnp.where(at_zero, min_pos, jnp.where(above, vmax + (jnp.abs(vmax) + 1.0), 0.0))
    c_lo = jnp.where(at_zero, n_nonneg, jnp.where(above, n_pos, n_causal))
    c_hi = jnp.where(at_zero, n_pos, jnp.where(above, 0.0, n_nonneg))
    done = jnp.where(at_zero | (c_lo == k_sel), 1.0, 0.0)

    def n_open(d):
        return jnp.sum(1.0 - d)

    def bisect_cond(carry):
        it, n_left = carry[0], carry[1]
        return (it < NUM_BISECT) & (n_left > 0.0)

    def bisect_body(carry):
        it, _, lo, hi, c_lo, c_hi, done = carry
        for _ in range(BISECT_UNROLL):
            mid = 0.5 * lo + 0.5 * hi
            cnt = count_ge(mid)
            live = done < 0.5
            up = (cnt >= k_sel) & live
            dn = (cnt < k_sel) & live
            lo = jnp.where(up, mid, lo)
            c_lo = jnp.where(up, cnt, c_lo)
            hi = jnp.where(dn, mid, hi)
            c_hi = jnp.where(dn, cnt, c_hi)
            done = jnp.where(c_lo == k_sel, 1.0, done)
        return it + BISECT_UNROLL, n_open(done), lo, hi, c_lo, c_hi, done

    _, _, lo, hi, c_lo, c_hi, done = lax.while_loop(
        bisect_cond, bisect_body, (jnp.int32(0), n_open(done), lo, hi, c_lo, c_hi, done))
    need = k_sel - c_hi
    n_tied = jnp.sum(jnp.where(c_lo - c_hi > need, 1.0, 0.0))

    def tie_mask():
        tri = (lax.broadcasted_iota(jnp.int32, (kc, kc), 1)
               < lax.broadcasted_iota(jnp.int32, (kc, kc), 0)).astype(BF16)

        def mask_body(ci, seen):
            off = pl.multiple_of(ci * kc, kc)
            blk = s_ref[pl.ds(off, kc), :]
            tie = jnp.where((blk >= lo) & (blk < hi), 1.0, 0.0)
            rank = _dot(tri, tie.astype(BF16)) + seen
            sel = (blk >= hi) | ((tie > 0.5) & (rank < need))
            s_ref[pl.ds(off, kc), :] = jnp.where(sel, 1.0, 0.0)
            return seen + total(col_sum(tie))

        lax.fori_loop(0, n_kc, mask_body, jnp.zeros((1, qb), F32))

    pl.when(n_tied > 0.0)(tie_mask)
    sel_thr = jnp.where(n_tied > 0.0, 0.5, lo)

    hs = range(HEADS)
    sls = [slice(h * HEAD_DIM, (h + 1) * HEAD_DIM) for h in hs]
    qh = [q_ref[:, sl] for sl in sls]
    acc_ref[...] = jnp.zeros(acc_ref.shape, F32)

    def att_body(ci, carry, width=kc):
        ms, ls = carry
        off = pl.multiple_of(ci * width, width)
        sel = s_ref[pl.ds(off, width), :] >= sel_thr
        lm = [jnp.where(sel, _nt_dot(k_ref[pl.ds(off, width), sls[h]], qh[h]), MASK_NEG) for h in hs]
        m_new = [jnp.maximum(ms[h], jnp.max(lm[h], axis=0, keepdims=True)) for h in hs]
        p = [jnp.exp2(lm[h] - m_new[h]) for h in hs]
        alpha = [jnp.exp2(ms[h] - m_new[h]) for h in hs]
        l_new = [alpha[h] * ls[h] + jnp.sum(p[h], axis=0, keepdims=True) for h in hs]
        pv = [_dot(vt_ref[sls[h], pl.ds(off, width)], p[h].astype(BF16)) for h in hs]
        for h in hs:
            acc_ref[sls[h], :] = alpha[h] * acc_ref[sls[h], :] + pv[h]
        return tuple(m_new), tuple(l_new)

    row_neg = jnp.full((1, qb), MASK_NEG, F32)
    row_zero = jnp.zeros((1, qb), F32)
    carry = ((row_neg,) * HEADS, (row_zero,) * HEADS)
    done_chunks = 0
    for group in TRIP_CHUNKS:
        n_trips = (n_kc - done_chunks) // group
        first = done_chunks // group
        carry = lax.fori_loop(first, first + n_trips, functools.partial(att_body, width=group * kc), carry)
        done_chunks = done_chunks + n_trips * group
    _, ls = carry
    for h in hs:
        o = (acc_ref[sls[h], :] / ls[h]).T
        y_ref[:, sls[h]] = (o * _silu(z_ref[:, sls[h]].astype(F32))).astype(y_ref.dtype)


def _dsa_branch(big3d, q_r, k_r, v_t, qi_r, ki_a, ki_b, w_t):
    b, t, _ = big3d.shape
    w = BRANCH_W
    qb = ATT_QB
    topk = min(TOPK_MAX, t // 4)
    return pl.pallas_call(
        functools.partial(_dsa_kernel, topk=topk),
        grid=(b, t // qb),
        in_specs=[
            pl.BlockSpec((None, qb, w), lambda i, j: (i, j, 0)),
            pl.BlockSpec((None, t, w), lambda i, j: (i, 0, 0)),
            pl.BlockSpec((None, w, t), lambda i, j: (i, 0, 0)),
            pl.BlockSpec((None, qb, HEADS * IDX_DIM), lambda i, j: (i, j, 0)),
            pl.BlockSpec((None, t, 2 * IDX_DIM), lambda i, j: (i, 0, 0)),
            pl.BlockSpec((None, t, 2 * IDX_DIM), lambda i, j: (i, 0, 0)),
            pl.BlockSpec((None, 8, qb), lambda i, j: (i, 0, j)),
            pl.BlockSpec((None, qb, w), lambda i, j: (i, j, COL_ZC // w)),
        ],
        out_specs=pl.BlockSpec((None, qb, w), lambda i, j: (i, j, 0)),
        out_shape=jax.ShapeDtypeStruct((b, t, w), BF16),
        scratch_shapes=[pltpu.VMEM((t, qb), F32),
                        pltpu.VMEM((w, qb), F32)],
        compiler_params=_cparams(2),
        name="sparse_attention",
    )(q_r, k_r, v_t, qi_r, ki_a, ki_b, w_t, big3d)


def _merge_kernel(ya_ref, yb_ref, yc_ref, ga_ref, gb_ref, gc_ref, bias_ref, wb_ref, wo_ref, x_ref, o_ref):
    merged = None
    for n, (y_ref, g_ref) in enumerate(((ya_ref, ga_ref), (yb_ref, gb_ref), (yc_ref, gc_ref))):
        proj = _dot(y_ref[...], wb_ref[n])
        gate = jax.nn.sigmoid(g_ref[...].astype(F32) + bias_ref[n:n + 1, :])
        term = gate * proj
        merged = term if merged is None else merged + term
    o_ref[...] = x_ref[...] + _dot(merged.astype(BF16), wo_ref[...])


def _merge(ya, yb, yc, big2d, gate_b, w_branch, w_out, x2d):
    m = x2d.shape[0]
    tm = MERGE_TM
    yspec = pl.BlockSpec((tm, BRANCH_W), lambda i: (i, 0))
    gspec = lambda n: pl.BlockSpec((tm, D_MODEL), lambda i, n=n: (i, COL_GATES // D_MODEL + n))
    return pl.pallas_call(
        _merge_kernel,
        grid=(m // tm,),
        in_specs=[
            yspec, yspec, yspec, gspec(0), gspec(1), gspec(2),
            pl.BlockSpec((N_BRANCH, D_MODEL), lambda i: (0, 0)),
            pl.BlockSpec((N_BRANCH, BRANCH_W, D_MODEL), lambda i: (0, 0, 0)),
            pl.BlockSpec((D_MODEL, D_MODEL), lambda i: (0, 0)),
            pl.BlockSpec((tm, D_MODEL), lambda i: (i, 0)),
        ],
        out_specs=pl.BlockSpec((tm, D_MODEL), lambda i: (i, 0)),
        out_shape=jax.ShapeDtypeStruct((m, D_MODEL), F32),
        compiler_params=_cparams(1),
        name="merge",
    )(ya, yb, yc, big2d, big2d, big2d, gate_b, w_branch, w_out, x2d)


_IN_SPLIT = (("dn_qkvz", 4 * BRANCH_W), ("beta_decay", 2 * HEADS), ("pool_uz", 2 * BRANCH_W),
             ("att_qkvz", 4 * BRANCH_W), ("idx_q", HEADS * IDX_DIM), ("idx_k", IDX_DIM), ("idx_w", HEADS),
             ("gates", N_BRANCH * D_MODEL))
_IN_OFFSET = {name: sum(w for _, w in _IN_SPLIT[:i]) for i, (name, _) in enumerate(_IN_SPLIT)}

PERM_COLS = 128
_PERM_REGULAR = (BIG_W + HEADS * IDX_DIM) // PERM_COLS


def _perm_source(blk):
    shift = jnp.where(blk < COL_UB // PERM_COLS, _IN_OFFSET["dn_qkvz"] - COL_QA,
                      jnp.where(blk < COL_GATES // PERM_COLS, _IN_OFFSET["pool_uz"] - COL_UB,
                                jnp.where(blk < BIG_W // PERM_COLS, _IN_OFFSET["gates"] - COL_GATES,
                                          _IN_OFFSET["idx_q"] - BIG_W)))
    return jnp.where(blk < _PERM_REGULAR, blk * PERM_COLS + shift, 0)


def _permute_kernel(w_ref, ik_ref, bd_ref, iw_ref, o_ref):
    blk = pl.program_id(0)
    depth = o_ref.shape[0]

    @pl.when(blk < _PERM_REGULAR)
    def _():
        for l in range(depth):
            o_ref[l] = w_ref[:, l, :].T.astype(o_ref.dtype)

    @pl.when(blk == _PERM_REGULAR)
    def _():
        row = lax.broadcasted_iota(jnp.int32, (8, o_ref.shape[1]), 0)
        for l in range(depth):
            idx_w = jnp.where(row < HEADS, iw_ref[:, l, :], 0.0)
            pad = jnp.zeros((PERM_COLS - IDX_DIM - 16, o_ref.shape[1]), F32)
            rows = jnp.concatenate([ik_ref[:, l, :], bd_ref[:, l, :], idx_w, pad], axis=0)
            o_ref[l] = rows.T.astype(o_ref.dtype)


def _permute_w_in(w_in):
    depth, d, _ = w_in.shape
    w_t = jnp.transpose(w_in, (2, 0, 1))
    el = pl.Element
    fixed = lambda rows, start: pl.BlockSpec((el(rows), el(depth), el(d)), lambda i: (start, 0, 0))
    return pl.pallas_call(
        _permute_kernel,
        grid=(_PERM_REGULAR + 1,),
        in_specs=[
            pl.BlockSpec((el(PERM_COLS), el(depth), el(d)), lambda i: (_perm_source(i), 0, 0)),
            fixed(IDX_DIM, _IN_OFFSET["idx_k"]), fixed(8, _IN_OFFSET["beta_decay"]), fixed(8, _IN_OFFSET["idx_w"]),
        ],
        out_specs=pl.BlockSpec((depth, d, PERM_COLS), lambda i: (0, 0, i)),
        out_shape=jax.ShapeDtypeStruct((depth, d, BIG_W + SMALL_W), BF16),
        compiler_params=_cparams(1),
        name="permute_w_in",
    )(w_t, w_t, w_t, w_t)


def _rope_tables(t):
    def base(dim):
        inv_freq = ROPE_THETA ** (-jnp.arange(0, dim, 2, dtype=F32) / dim)
        ang = jnp.arange(t, dtype=F32)[:, None] * inv_freq[None, :]
        return jnp.cos(ang), jnp.sin(ang)

    cos_a, sin_a = base(HEAD_DIM)
    cos = jnp.concatenate([cos_a, cos_a], axis=-1)
    sin = jnp.concatenate([-sin_a, sin_a], axis=-1)
    cos_i, sin_i = base(IDX_DIM)
    zero = jnp.zeros_like(sin_i)
    icos = jnp.tile(jnp.concatenate([cos_i, cos_i], axis=-1), (1, HEADS))
    isin_up = jnp.tile(jnp.concatenate([-sin_i, zero], axis=-1), (1, HEADS))
    isin_dn = jnp.tile(jnp.concatenate([zero, sin_i], axis=-1), (1, HEADS))
    return cos, sin, icos, isin_up, isin_dn


def _lane_row(vals, offset):
    d, n = vals.shape
    return jnp.zeros((d, 1, HEAD_DIM), F32).at[:, 0, offset:offset + n].set(vals.astype(F32))


def kernel(x, norm_g, w_in, gate_b, conv_w, a_log, dt_bias, dn_onorm, pool_w, pool_scale, q_norm, k_norm,
           w_branch, w_out):
    b, t, d = x.shape
    depth = norm_g.shape[0]
    w_p = _permute_w_in(w_in)
    wb16 = w_branch.astype(BF16)
    wo16 = w_out.astype(BF16)
    pw16 = pool_w.astype(BF16)
    alog_rows = _lane_row(a_log, SM_DECAY)
    dtb_rows = _lane_row(dt_bias, SM_DECAY)
    cos, sin, *idx_tabs = _rope_tables(t)
    idx_tabs = tuple(idx_tabs)
    x2d = x.reshape(b * t, d)
    for layer in range(depth):
        big2d, small2d = _in_projection(x2d, norm_g[layer][None, :], w_p, conv_w[layer], layer, t // PROJ_TM)
        big3d = big2d.reshape(b, t, BIG_W)
        small3d = small2d.reshape(b, t, SMALL_W)
        ya = _deltanet_branch(big3d, small3d, alog_rows[layer], dtb_rows[layer], dn_onorm[layer][None, :])
        yb = _pool_branch(big3d, pw16[layer], pool_scale[layer][None, :])
        half = HEAD_DIM // 2
        gain_tabs = (cos * q_norm[layer], sin * jnp.roll(q_norm[layer], half),
                     cos * k_norm[layer], sin * jnp.roll(k_norm[layer], half))
        prep = _attn_prep(big3d, small3d, gain_tabs + idx_tabs)
        yc = _dsa_branch(big3d, *prep)
        x2d = _merge(ya.reshape(b * t, BRANCH_W), yb.reshape(b * t, BRANCH_W), yc.reshape(b * t, BRANCH_W),
                     big2d, gate_b[layer], wb16[layer], wo16[layer], x2d)
    return x2d.reshape(b, t, d)
```

```python
import functools

import jax
import jax.numpy as jnp
from jax import lax
from jax.experimental import pallas as pl
from jax.experimental.pallas import tpu as pltpu

F32 = jnp.float32
BF16 = jnp.bfloat16

D_MODEL = 1024
HEADS = 4
HEAD_DIM = 128
BRANCH_W = HEADS * HEAD_DIM
DN_CONV = 4
POOL_WINDOWS = (2, 4, 8, 16)
POOL_GROUP = 128
IDX_DIM = 64
TOPK_MAX = 256
ROPE_THETA = 10000.0
NORM_EPS = 1e-6
N_BRANCH = 3

COL_QA, COL_KA, COL_VA, COL_ZA = 0, 512, 1024, 1536
COL_UB, COL_ZB = 2048, 2560
COL_QC, COL_KC, COL_VC, COL_ZC = 3072, 3584, 4096, 4608
COL_GATES = 5120
BIG_W = 8192
SMALL_W = 384
SM_BETA, SM_DECAY, SM_IDXW = 64, 68, 72

V7X_VMEM_LIMIT = 56 * 1024 * 1024

PROJ_TM = 512
PROJ_TN = 512
MERGE_TM = 1024
PREP_TT = 1024
POOL_TT = 256
POOL_HALO = 16
DN_CHUNK = 128
CONV_HALO = 8
CONV_ROWS = 128
DN_BASE = 8
DN_PAR = 2
ATT_QB = 256
ATT_KC = 256
NUM_BISECT = 36
BISECT_UNROLL = 4
COUNT_LANES = 4
TRIP_CHUNKS = (4, 2, 1)
MASK_NEG = -1e30
LOG2_E = 1.4426950408889634


def _nt_dot(a, b):
    return lax.dot_general(a, b, (((1,), (1,)), ((), ())), preferred_element_type=F32)


def _dot(a, b):
    return jnp.dot(a, b, preferred_element_type=F32)


def _silu(x):
    return x * jax.nn.sigmoid(x)


def _cparams(n_axes):
    return pltpu.CompilerParams(dimension_semantics=("arbitrary",) * n_axes,
                                vmem_limit_bytes=V7X_VMEM_LIMIT)


def _inproj_kernel(x_ref, g_ref, w_ref, ws_ref, cw_ref, big_ref, small_ref, conv_buf, *, tiles_per_seq):
    step = pl.program_id(0)

    @pl.when(step == 0)
    def _():
        conv_buf[...] = jnp.zeros(conv_buf.shape, F32)

    x = x_ref[...]
    ms = jnp.mean(x * x, axis=-1, keepdims=True)
    h = ((x * lax.rsqrt(ms + NORM_EPS)) * g_ref[...]).astype(BF16)
    seq_start = (step % tiles_per_seq) == 0
    hd = HEAD_DIM
    halo = CONV_HALO

    def stage(seg):
        c = seg * PROJ_TN
        prev = jnp.where(seq_start, 0.0, conv_buf[seg, PROJ_TM:PROJ_TM + halo, :])
        conv_buf[seg, 0:halo, :] = prev
        conv_buf[seg, halo:PROJ_TM + halo, :] = _dot(h, w_ref[:, c:c + PROJ_TN])

    def conv_piece(seg, r, k, anchor):
        c = seg * PROJ_TN + k * hd
        cw = cw_ref[:, c:c + hd] + anchor
        rows = conv_buf[seg, r:r + CONV_ROWS + halo, k * hd:(k + 1) * hd]
        acc = rows * cw[DN_CONV - 1:DN_CONV, :]
        for s in range(1, DN_CONV):
            acc = acc + pltpu.roll(rows, s, axis=0) * cw[DN_CONV - 1 - s:DN_CONV - s, :]
        z = _silu(acc[halo:, :])
        if c < COL_VA:
            scale = hd ** -0.5 if c < COL_KA else 1.0
            z = z * (lax.rsqrt(jnp.sum(z * z, axis=-1, keepdims=True) + NORM_EPS) * scale)
        big_ref[r:r + CONV_ROWS, c:c + hd] = z.astype(big_ref.dtype)

    plain = list(range(COL_ZA, BIG_W, PROJ_TN))
    n_slot = 4
    for seg in range(COL_ZA // PROJ_TN):
        stage(seg)
        pieces = [(seg, r, k) for r in range(0, PROJ_TM, CONV_ROWS) for k in range(HEADS)]
        per_slot = -(-len(pieces) // n_slot)
        for _ in range(n_slot):
            c = plain.pop(0)
            y = _dot(h, w_ref[:, c:c + PROJ_TN])
            big_ref[:, c:c + PROJ_TN] = y.astype(big_ref.dtype)
            anchor = y[PROJ_TM - 1:PROJ_TM, 0:hd] * 0.0
            for piece in pieces[:per_slot]:
                conv_piece(*piece, anchor)
            pieces = pieces[per_slot:]
    for c in plain:
        big_ref[:, c:c + PROJ_TN] = _dot(h, w_ref[:, c:c + PROJ_TN]).astype(big_ref.dtype)
    small_ref[...] = _dot(h, ws_ref[...])


def _in_projection(x2d, g_row, w_main, w_side, conv_w, layer, tiles_per_seq):
    m = x2d.shape[0]
    return pl.pallas_call(
        functools.partial(_inproj_kernel, tiles_per_seq=tiles_per_seq),
        grid=(m // PROJ_TM,),
        in_specs=[
            pl.BlockSpec((PROJ_TM, D_MODEL), lambda i: (i, 0)),
            pl.BlockSpec((1, D_MODEL), lambda i: (0, 0)),
            pl.BlockSpec((None, D_MODEL, BIG_W), lambda i: (layer, 0, 0), pipeline_mode=pl.Buffered(1)),
            pl.BlockSpec((None, D_MODEL, SMALL_W), lambda i: (layer, 0, 0), pipeline_mode=pl.Buffered(1)),
            pl.BlockSpec((DN_CONV, 3 * BRANCH_W), lambda i: (0, 0)),
        ],
        out_specs=[
            pl.BlockSpec((PROJ_TM, BIG_W), lambda i: (i, 0)),
            pl.BlockSpec((PROJ_TM, SMALL_W), lambda i: (i, 0)),
        ],
        out_shape=[
            jax.ShapeDtypeStruct((m, BIG_W), BF16),
            jax.ShapeDtypeStruct((m, SMALL_W), F32),
        ],
        scratch_shapes=[pltpu.VMEM((3, PROJ_TM + CONV_HALO, BRANCH_W), F32)],
        compiler_params=_cparams(1),
        name="in_projection",
    )(x2d, g_row, w_main, w_side, conv_w)


def _pool_kernel(u_ref, z_ref, pw_ref, ps_ref, y_ref):
    t_len = u_ref.shape[0]
    for t in range(t_len // POOL_TT):
        r0 = t * POOL_TT
        cur = u_ref[r0:r0 + POOL_TT, :].astype(F32)
        if t == 0:
            prev = jnp.zeros((POOL_HALO, cur.shape[1]), F32)
        else:
            prev = u_ref[r0 - POOL_HALO:r0, :].astype(F32)
        win_rows = jnp.concatenate([prev, cur], axis=0)
        pos = r0 + lax.broadcasted_iota(jnp.int32, (POOL_TT, 1), 0)
        for gi, win in enumerate(POOL_WINDOWS):
            sl = slice(gi * POOL_GROUP, (gi + 1) * POOL_GROUP)
            s = win_rows[:, sl]
            shift = 1
            while shift < win:
                s = s + pltpu.roll(s, shift, axis=0)
                shift *= 2
            count = jnp.minimum(pos + 1, win).astype(F32)
            pooled = s[POOL_HALO:, :] / count - cur[:, sl]
            mixed = _dot(pooled.astype(BF16), pw_ref[gi])
            zg = z_ref[r0:r0 + POOL_TT, sl].astype(F32)
            y_ref[r0:r0 + POOL_TT, sl] = (mixed * ps_ref[:, sl] * _silu(zg)).astype(y_ref.dtype)


def _pool_branch(big3d, pool_w, pool_scale_row):
    b, t, _ = big3d.shape
    width = POOL_GROUP * len(POOL_WINDOWS)
    return pl.pallas_call(
        _pool_kernel,
        grid=(b,),
        in_specs=[
            pl.BlockSpec((None, t, width), lambda i: (i, 0, COL_UB // width)),
            pl.BlockSpec((None, t, width), lambda i: (i, 0, COL_ZB // width)),
            pl.BlockSpec((len(POOL_WINDOWS), POOL_GROUP, POOL_GROUP), lambda i: (0, 0, 0)),
            pl.BlockSpec((1, width), lambda i: (0, 0)),
        ],
        out_specs=pl.BlockSpec((None, t, width), lambda i: (i, 0, 0)),
        out_shape=jax.ShapeDtypeStruct((b, t, width), BF16),
        compiler_params=_cparams(1),
        name="pool_branch",
    )(big3d, big3d, pool_w, pool_scale_row)


def _softplus(x):
    return jnp.maximum(x, 0.0) + jnp.log1p(jnp.exp(-jnp.abs(x)))


def _split_bf16(a):
    hi = a.astype(BF16)
    lo = (a - hi.astype(F32)).astype(BF16)
    return hi, lo


def _dot_split(a, b):
    a_hi, a_lo = a
    b_hi, b_lo = b
    return _dot(jnp.concatenate([a_hi, a_lo, a_hi], axis=1), jnp.concatenate([b_hi, b_hi, b_lo], axis=0))


def _dn_kernel(q_ref, k_ref, v_ref, z_ref, sm_ref, alog_ref, dtb_ref, on_ref,
               y_ref, u_s, w_s, qg_s, a_s, kdt_s, el_s, st_s):
    t_len = q_ref.shape[0]
    n_chunks = t_len // DN_CHUNK
    c = DN_CHUNK
    hd = HEAD_DIM
    row = lax.broadcasted_iota(jnp.int32, (c, c), 0)
    col = lax.broadcasted_iota(jnp.int32, (c, c), 1)
    tril = row >= col
    strict = row > col
    tril16 = tril.astype(BF16)
    tril16x3 = jnp.concatenate([tril16, tril16, tril16], axis=1)
    eye_f = (row == col).astype(F32)
    base_blk = (row // DN_BASE) == (col // DN_BASE)
    pair_blks = []
    size = DN_BASE
    while size < c:
        pair_blks.append(((row // (2 * size)) == (col // (2 * size))) & ((row // size) != (col // size)))
        size *= 2

    def prepare(gi, carry):
        cis = [gi * DN_PAR + j for j in range(DN_PAR)]
        starts = [pl.multiple_of(ci * c, c) for ci in cis]
        beta_all, gc_all = [], []
        for start in starts:
            sm = sm_ref[pl.ds(start, c), :]
            beta_all.append(jax.nn.sigmoid(sm))
            g_all = -jnp.exp(alog_ref[...]) * _softplus(sm + dtb_ref[...])
            g_hi = g_all.astype(BF16)
            g_r = g_all - g_hi.astype(F32)
            g_mid = g_r.astype(BF16)
            g_lo = (g_r - g_mid.astype(F32)).astype(BF16)
            gc_all.append(_dot(tril16x3, jnp.concatenate([g_hi, g_mid, g_lo], axis=0)))
        items = [(j, h) for j in range(DN_PAR) for h in range(HEADS)]
        ids = range(len(items))
        sls = [slice(h * hd, (h + 1) * hd) for _, h in items]
        qn = [q_ref[pl.ds(starts[j], c), sls[i]].astype(F32) for i, (j, _) in enumerate(items)]
        kn = [k_ref[pl.ds(starts[j], c), sls[i]].astype(F32) for i, (j, _) in enumerate(items)]
        beta_b = [jnp.broadcast_to(beta_all[j][:, SM_BETA + h:SM_BETA + h + 1], (c, hd)) for j, h in items]
        gc = [jnp.broadcast_to(gc_all[j][:, SM_DECAY + h:SM_DECAY + h + 1], (c, hd)) for j, h in items]
        decay = [jnp.exp(jnp.where(tril, gc[i] - gc[i].T, -jnp.inf)) for i in ids]
        kb = [kn[i] * beta_b[i] for i in ids]
        kn16 = [kn[i].astype(BF16) for i in ids]
        kq = [_nt_dot(jnp.concatenate([kb[i].astype(BF16), qn[i].astype(BF16)], axis=0), kn16[i]) for i in ids]
        lmat = [jnp.where(strict, kq[i][:c, :] * decay[i], 0.0) for i in ids]
        l_hi = [lmat[i].astype(BF16) for i in ids]
        l_lo = [(lmat[i] - l_hi[i].astype(F32)).astype(BF16) for i in ids]
        zero16 = jnp.zeros((c, c), BF16)
        d_parts = [(jnp.where(base_blk, l_hi[i], zero16), jnp.where(base_blk, l_lo[i], zero16)) for i in ids]
        tmat = [eye_f - jnp.where(base_blk, lmat[i], 0.0) for i in ids]
        power = [_dot_split(d_parts[i], d_parts[i]) for i in ids]
        span = 2
        while span < DN_BASE:
            parts = [_split_bf16(power[i]) for i in ids]
            t_parts = [_split_bf16(tmat[i]) for i in ids]
            span *= 2
            if span < DN_BASE:
                both = [_dot_split((jnp.concatenate([t_parts[i][0], parts[i][0]], axis=0),
                                    jnp.concatenate([t_parts[i][1], parts[i][1]], axis=0)), parts[i]) for i in ids]
                tmat = [tmat[i] + both[i][:c, :] for i in ids]
                power = [both[i][c:, :] for i in ids]
            else:
                tmat = [tmat[i] + _dot_split(t_parts[i], parts[i]) for i in ids]
        for pair_blk in pair_blks:
            t_parts = [_split_bf16(tmat[i]) for i in ids]
            off = [(jnp.where(pair_blk, l_hi[i], zero16), jnp.where(pair_blk, l_lo[i], zero16)) for i in ids]
            cx = [_dot_split(off[i], t_parts[i]) for i in ids]
            tmat = [tmat[i] - _dot_split(t_parts[i], _split_bf16(cx[i])) for i in ids]
        for i, (j, h) in enumerate(items):
            sl = sls[i]
            start = starts[j]
            t16 = tmat[i].astype(BF16)
            egc = jnp.exp(gc[i])
            vb = v_ref[pl.ds(start, c), sl].astype(F32) * beta_b[i]
            uw = _dot(t16, jnp.concatenate([vb.astype(BF16), (kb[i] * egc).astype(BF16)], axis=1))
            u_s[pl.ds(start, c), sl] = uw[:, :hd]
            w_s[pl.ds(start, c), sl] = uw[:, hd:].astype(BF16)
            qg_s[pl.ds(start, c), sl] = (qn[i] * egc).astype(BF16)
            a_s[pl.ds(start, c), sl] = jnp.where(tril, kq[i][c:, :] * decay[i], 0.0).astype(BF16)
            g_last = gc[i][c - 1:c, :]
            kd = kn[i] * jnp.exp(g_last - gc[i])
            kdt_s[pl.ds(start, c), sl] = kd.T.astype(BF16)
            el_s[pl.ds(pl.multiple_of(cis[j] * 8, 8), 8), sl] = jnp.broadcast_to(jnp.exp(g_last), (8, hd))
        return carry

    def scan_chunk(ci):
        start = pl.multiple_of(ci * c, c)
        hs = range(HEADS)
        sls = [slice(h * hd, (h + 1) * hd) for h in hs]
        state = [st_s[h] for h in hs]
        s16 = [state[h].astype(BF16) for h in hs]
        v_new = [u_s[pl.ds(start, c), sls[h]] - _dot(w_s[pl.ds(start, c), sls[h]], s16[h]) for h in hs]
        v16 = [v_new[h].astype(BF16) for h in hs]
        for h in hs:
            e_last = el_s[pl.ds(pl.multiple_of(ci * 8, 8), 8), sls[h]][0:1, :]
            st_s[h] = state[h] * e_last + _dot(kdt_s[pl.ds(start, c), sls[h]], v16[h])
        for h in hs:
            sl = sls[h]
            o = _dot(qg_s[pl.ds(start, c), sl], s16[h]) + _dot(a_s[pl.ds(start, c), sl], v16[h])
            on = o * lax.rsqrt(jnp.mean(o * o, axis=-1, keepdims=True) + NORM_EPS) * on_ref[...]
            zg = z_ref[pl.ds(start, c), sl].astype(F32)
            y_ref[pl.ds(start, c), sl] = (on * _silu(zg)).astype(y_ref.dtype)

    def scan_group(gi):
        for j in range(DN_PAR):
            scan_chunk(gi * DN_PAR + j)

    n_groups = n_chunks // DN_PAR
    st_s[...] = jnp.zeros(st_s.shape, F32)
    prepare(0, 0)

    def trip(gi, carry):
        scan_group(gi - 1)
        return prepare(gi, carry)

    lax.fori_loop(1, n_groups, trip, 0)
    scan_group(n_groups - 1)


def _deltanet_branch(big3d, small3d, alog_row, dtb_row, onorm_row):
    b, t, _ = big3d.shape
    hd = HEAD_DIM
    w = BRANCH_W
    seq_spec = lambda col: pl.BlockSpec((None, t, w), lambda i, col=col: (i, 0, col // w))
    row_spec = pl.BlockSpec((1, hd), lambda i: (0, 0))
    return pl.pallas_call(
        _dn_kernel,
        grid=(b,),
        in_specs=[
            seq_spec(COL_QA), seq_spec(COL_KA), seq_spec(COL_VA), seq_spec(COL_ZA),
            pl.BlockSpec((None, t, hd), lambda i: (i, 0, SMALL_W // hd - 1)),
            row_spec, row_spec, row_spec,
        ],
        out_specs=pl.BlockSpec((None, t, w), lambda i: (i, 0, 0)),
        out_shape=jax.ShapeDtypeStruct((b, t, w), BF16),
        scratch_shapes=[
            pltpu.VMEM((t, w), F32),
            pltpu.VMEM((t, w), BF16),
            pltpu.VMEM((t, w), BF16),
            pltpu.VMEM((t, w), BF16),
            pltpu.VMEM((t, w), BF16),
            pltpu.VMEM((8 * t // DN_CHUNK, w), F32),
            pltpu.VMEM((HEADS, hd, hd), F32),
        ],
        compiler_params=_cparams(1),
        name="deltanet_branch",
    )(big3d, big3d, big3d, big3d, small3d, alog_row, dtb_row, onorm_row)


def _attn_prep_kernel(q_ref, k_ref, v_ref, sm_ref, qc_ref, qs_ref, kc_ref, ks_ref,
                      icos_ref, isin_up_ref, isin_dn_ref,
                      qo_ref, ko_ref, vt_ref, qio_ref, kia_ref, kib_ref, wt_ref):
    hd = HEAD_DIM
    lane_r = lax.broadcasted_iota(jnp.int32, (hd, hd), 0)
    lane_c = lax.broadcasted_iota(jnp.int32, (hd, hd), 1)
    swap_halves = (lane_r == (lane_c + hd // 2) % hd).astype(BF16)

    def norm_rope(ref, cos_g, sin_g, out_ref, scale):
        for h in range(HEADS):
            sl = slice(h * hd, (h + 1) * hd)
            x16 = ref[:, sl]
            x = x16.astype(F32)
            inv = lax.rsqrt(jnp.mean(x * x, axis=-1, keepdims=True) + NORM_EPS) * scale
            out_ref[:, sl] = ((x * cos_g + _dot(x16, swap_halves) * sin_g) * inv).astype(out_ref.dtype)

    norm_rope(q_ref, qc_ref[...], qs_ref[...], qo_ref, hd ** -0.5 * LOG2_E)
    norm_rope(k_ref, kc_ref[...], ks_ref[...], ko_ref, 1.0)
    w = v_ref.shape[1]
    eye = (lax.broadcasted_iota(jnp.int32, (w, w), 0) == lax.broadcasted_iota(jnp.int32, (w, w), 1)).astype(BF16)
    vt_ref[...] = _nt_dot(eye, v_ref[...]).astype(vt_ref.dtype)

    sm = sm_ref[...]
    iq_w = HEADS * IDX_DIM
    half = IDX_DIM // 2
    iq = sm[:, :iq_w]
    iq_r = (iq * icos_ref[...] + pltpu.roll(iq, iq_w - half, axis=1) * isin_up_ref[...]
            + pltpu.roll(iq, half, axis=1) * isin_dn_ref[...])
    qio_ref[...] = iq_r.astype(qio_ref.dtype)
    last = sm[:, iq_w:]
    lw = last.shape[1]
    ik_r = (last * icos_ref[:, :lw] + pltpu.roll(last, lw - half, axis=1) * isin_up_ref[:, :lw]
            + pltpu.roll(last, half, axis=1) * isin_dn_ref[:, :lw])
    lane = lax.broadcasted_iota(jnp.int32, ik_r.shape, 1)
    ik_r = jnp.where(lane < IDX_DIM, ik_r, 0.0)
    kia_ref[...] = ik_r.astype(kia_ref.dtype)
    kib_ref[...] = pltpu.roll(ik_r, IDX_DIM, axis=1).astype(kib_ref.dtype)
    wt = last.T
    wt_ref[...] = wt[SM_IDXW:SM_IDXW + 8, :] * (HEADS ** -0.5 * IDX_DIM ** -0.5)


def _attn_prep(big3d, small3d, tabs):
    b, t, _ = big3d.shape
    w = BRANCH_W
    tt = PREP_TT
    iq_w = HEADS * IDX_DIM
    seq = lambda col: pl.BlockSpec((None, tt, w), lambda j, i, col=col: (i, j, col // w))
    tab = lambda width: pl.BlockSpec((tt, width), lambda j, i: (j, 0))
    return pl.pallas_call(
        _attn_prep_kernel,
        grid=(t // tt, b),
        in_specs=[
            seq(COL_QC), seq(COL_KC), seq(COL_VC),
            pl.BlockSpec((None, tt, SMALL_W), lambda j, i: (i, j, 0)),
            tab(HEAD_DIM), tab(HEAD_DIM), tab(HEAD_DIM), tab(HEAD_DIM), tab(iq_w), tab(iq_w), tab(iq_w),
        ],
        out_specs=[
            pl.BlockSpec((None, tt, w), lambda j, i: (i, j, 0)),
            pl.BlockSpec((None, tt, w), lambda j, i: (i, j, 0)),
            pl.BlockSpec((None, w, tt), lambda j, i: (i, 0, j)),
            pl.BlockSpec((None, tt, iq_w), lambda j, i: (i, j, 0)),
            pl.BlockSpec((None, tt, 2 * IDX_DIM), lambda j, i: (i, j, 0)),
            pl.BlockSpec((None, tt, 2 * IDX_DIM), lambda j, i: (i, j, 0)),
            pl.BlockSpec((None, 8, tt), lambda j, i: (i, 0, j)),
        ],
        out_shape=[
            jax.ShapeDtypeStruct((b, t, w), BF16),
            jax.ShapeDtypeStruct((b, t, w), BF16),
            jax.ShapeDtypeStruct((b, w, t), BF16),
            jax.ShapeDtypeStruct((b, t, iq_w), BF16),
            jax.ShapeDtypeStruct((b, t, 2 * IDX_DIM), BF16),
            jax.ShapeDtypeStruct((b, t, 2 * IDX_DIM), BF16),
            jax.ShapeDtypeStruct((b, 8, t), F32),
        ],
        compiler_params=_cparams(2),
        name="attn_prep",
    )(big3d, big3d, big3d, small3d, *tabs)


def _dsa_kernel(q_ref, k_ref, vt_ref, qi_ref, kia_ref, kib_ref, wt_ref, z_ref, y_ref, s_ref, acc_ref, *, topk):
    qb, kc = ATT_QB, ATT_KC
    blk_i = pl.program_id(1)
    n_kc = blk_i + 1
    q_pos = blk_i * qb + lax.broadcasted_iota(jnp.int32, (1, qb), 1)
    qi = qi_ref[...]
    wt = wt_ref[...]
    inf = jnp.inf

    def col_sum(x):
        return x.reshape(kc // 8, 8, qb).sum(axis=0)

    def total(x):
        return jnp.sum(x, axis=0, keepdims=True)

    def score_chunk(ci, carry, diagonal, width=kc):
        vmax, vmin, min_pos, n_pos, n_nonneg = carry
        off = pl.multiple_of(ci * width, width)
        ka = kia_ref[pl.ds(off, width), :]
        kb = kib_ref[pl.ds(off, width), :]
        s = jnp.zeros((width, qb), F32)
        for h in range(HEADS):
            kk = ka if h % 2 == 0 else kb
            qq = qi[:, (h // 2) * 2 * IDX_DIM:(h // 2 + 1) * 2 * IDX_DIM]
            s = s + jnp.maximum(_nt_dot(kk, qq), 0.0) * wt[h:h + 1, :]
        s = jnp.where(s == 0.0, 0.0, s)
        if diagonal:
            key_pos = off + lax.broadcasted_iota(jnp.int32, (width, 1), 0)
            causal = key_pos <= q_pos
            sc = jnp.where(causal, s, -inf)
            s_hi = jnp.where(causal, s, inf)
        else:
            sc = s_hi = s
        s_ref[pl.ds(off, width), :] = sc
        vmax = jnp.maximum(vmax, jnp.max(sc, axis=0, keepdims=True))
        vmin = jnp.minimum(vmin, jnp.min(s_hi, axis=0, keepdims=True))
        min_pos = jnp.minimum(min_pos, jnp.min(jnp.where(sc > 0.0, sc, inf), axis=0, keepdims=True))
        n_pos = n_pos + jnp.where(sc > 0.0, 1.0, 0.0).reshape(width // 8, 8, qb).sum(axis=0)
        n_nonneg = n_nonneg + jnp.where(sc >= 0.0, 1.0, 0.0).reshape(width // 8, 8, qb).sum(axis=0)
        return vmax, vmin, min_pos, n_pos, n_nonneg

    row_inf = jnp.full((1, qb), inf, F32)
    zeros8 = jnp.zeros((8, qb), F32)
    stats = (-row_inf, row_inf, row_inf, zeros8, zeros8)
    done_chunks = 0
    for group in TRIP_CHUNKS:
        n_trips = (blk_i - done_chunks) // group
        first = done_chunks // group
        stats = lax.fori_loop(first, first + n_trips,
                              functools.partial(score_chunk, diagonal=False, width=group * kc), stats)
        done_chunks = done_chunks + n_trips * group
    vmax, vmin, min_pos, n_pos, n_nonneg = score_chunk(blk_i, stats, diagonal=True)
    n_pos = total(n_pos)
    n_nonneg = total(n_nonneg)

    def count_ge(thr):
        def body(ci, acc):
            off = pl.multiple_of(ci * kc, kc)
            hit = jnp.where(s_ref[pl.ds(off, kc), :] >= thr, 1.0, 0.0)
            return acc + hit.reshape(kc // (8 * COUNT_LANES), COUNT_LANES * 8, qb).sum(axis=0)
        return total(lax.fori_loop(0, n_kc, body, jnp.zeros((COUNT_LANES * 8, qb), F32)))

    k_sel = jnp.minimum(q_pos + 1, topk).astype(F32)
    n_causal = (q_pos + 1).astype(F32)
    at_zero = (n_pos < k_sel) & (k_sel <= n_nonneg)
    above = k_sel <= n_pos
    lo = jnp.where(at_zero, 0.0, jnp.where(above, min_pos, vmin))
    hi = jnp.where(at_zero, min_pos, jnp.where(above, vmax + (jnp.abs(vmax) + 1.0), 0.0))
    c_lo = jnp.where(at_zero, n_nonneg, jnp.where(above, n_pos, n_causal))
    c_hi = jnp.where(at_zero, n_pos, jnp.where(above, 0.0, n_nonneg))
    done = jnp.where(at_zero | (c_lo == k_sel), 1.0, 0.0)

    def n_open(d):
        return jnp.sum(1.0 - d)

    def bisect_cond(carry):
        it, n_left = carry[0], carry[1]
        return (it < NUM_BISECT) & (n_left > 0.0)

    def bisect_body(carry):
        it, _, lo, hi, c_lo, c_hi, done = carry
        for _ in range(BISECT_UNROLL):
            mid = 0.5 * lo + 0.5 * hi
            cnt = count_ge(mid)
            live = done < 0.5
            up = (cnt >= k_sel) & live
            dn = (cnt < k_sel) & live
            lo = jnp.where(up, mid, lo)
            c_lo = jnp.where(up, cnt, c_lo)
            hi = jnp.where(dn, mid, hi)
            c_hi = jnp.where(dn, cnt, c_hi)
            done = jnp.where(c_lo == k_sel, 1.0, done)
        return it + BISECT_UNROLL, n_open(done), lo, hi, c_lo, c_hi, done

    _, _, lo, hi, c_lo, c_hi, done = lax.while_loop(
        bisect_cond, bisect_body, (jnp.int32(0), n_open(done), lo, hi, c_lo, c_hi, done))
    need = k_sel - c_hi
    n_tied = jnp.sum(jnp.where(c_lo - c_hi > need, 1.0, 0.0))

    def tie_mask():
        tri = (lax.broadcasted_iota(jnp.int32, (kc, kc), 1)
               < lax.broadcasted_iota(jnp.int32, (kc, kc), 0)).astype(BF16)

        def mask_body(ci, seen):
            off = pl.multiple_of(ci * kc, kc)
            blk = s_ref[pl.ds(off, kc), :]
            tie = jnp.where((blk >= lo) & (blk < hi), 1.0, 0.0)
            rank = _dot(tri, tie.astype(BF16)) + seen
            sel = (blk >= hi) | ((tie > 0.5) & (rank < need))
            s_ref[pl.ds(off, kc), :] = jnp.where(sel, 1.0, 0.0)
            return seen + total(col_sum(tie))

        lax.fori_loop(0, n_kc, mask_body, jnp.zeros((1, qb), F32))

    pl.when(n_tied > 0.0)(tie_mask)
    sel_thr = jnp.where(n_tied > 0.0, 0.5, lo)

    hs = range(HEADS)
    sls = [slice(h * HEAD_DIM, (h + 1) * HEAD_DIM) for h in hs]
    qh = [q_ref[:, sl] for sl in sls]
    acc_ref[...] = jnp.zeros(acc_ref.shape, F32)

    def att_body(ci, carry, width=kc):
        ms, ls = carry
        off = pl.multiple_of(ci * width, width)
        sel = s_ref[pl.ds(off, width), :] >= sel_thr
        lm = [jnp.where(sel, _nt_dot(k_ref[pl.ds(off, width), sls[h]], qh[h]), MASK_NEG) for h in hs]
        m_new = [jnp.maximum(ms[h], jnp.max(lm[h], axis=0, keepdims=True)) for h in hs]
        p = [jnp.exp2(lm[h] - m_new[h]) for h in hs]
        alpha = [jnp.exp2(ms[h] - m_new[h]) for h in hs]
        l_new = [alpha[h] * ls[h] + jnp.sum(p[h], axis=0, keepdims=True) for h in hs]
        pv = [_dot(vt_ref[sls[h], pl.ds(off, width)], p[h].astype(BF16)) for h in hs]
        for h in hs:
            acc_ref[sls[h], :] = alpha[h] * acc_ref[sls[h], :] + pv[h]
        return tuple(m_new), tuple(l_new)

    row_neg = jnp.full((1, qb), MASK_NEG, F32)
    row_zero = jnp.zeros((1, qb), F32)
    carry = ((row_neg,) * HEADS, (row_zero,) * HEADS)
    done_chunks = 0
    for group in TRIP_CHUNKS:
        n_trips = (n_kc - done_chunks) // group
        first = done_chunks // group
        carry = lax.fori_loop(first, first + n_trips, functools.partial(att_body, width=group * kc), carry)
        done_chunks = done_chunks + n_trips * group
    _, ls = carry
    for h in hs:
        o = (acc_ref[sls[h], :] / ls[h]).T
        y_ref[:, sls[h]] = (o * _silu(z_ref[:, sls[h]].astype(F32))).astype(y_ref.dtype)


def _dsa_branch(big3d, q_r, k_r, v_t, qi_r, ki_a, ki_b, w_t):
    b, t, _ = big3d.shape
    w = BRANCH_W
    qb = ATT_QB
    topk = min(TOPK_MAX, t // 4)
    return pl.pallas_call(
        functools.partial(_dsa_kernel, topk=topk),
        grid=(b, t // qb),
        in_specs=[
            pl.BlockSpec((None, qb, w), lambda i, j: (i, j, 0)),
            pl.BlockSpec((None, t, w), lambda i, j: (i, 0, 0)),
            pl.BlockSpec((None, w, t), lambda i, j: (i, 0, 0)),
            pl.BlockSpec((None, qb, HEADS * IDX_DIM), lambda i, j: (i, j, 0)),
            pl.BlockSpec((None, t, 2 * IDX_DIM), lambda i, j: (i, 0, 0)),
            pl.BlockSpec((None, t, 2 * IDX_DIM), lambda i, j: (i, 0, 0)),
            pl.BlockSpec((None, 8, qb), lambda i, j: (i, 0, j)),
            pl.BlockSpec((None, qb, w), lambda i, j: (i, j, COL_ZC // w)),
        ],
        out_specs=pl.BlockSpec((None, qb, w), lambda i, j: (i, j, 0)),
        out_shape=jax.ShapeDtypeStruct((b, t, w), BF16),
        scratch_shapes=[pltpu.VMEM((t, qb), F32),
                        pltpu.VMEM((w, qb), F32)],
        compiler_params=_cparams(2),
        name="sparse_attention",
    )(q_r, k_r, v_t, qi_r, ki_a, ki_b, w_t, big3d)


def _merge_kernel(ya_ref, yb_ref, yc_ref, ga_ref, gb_ref, gc_ref, bias_ref, wb_ref, wo_ref, x_ref, o_ref):
    merged = None
    for n, (y_ref, g_ref) in enumerate(((ya_ref, ga_ref), (yb_ref, gb_ref), (yc_ref, gc_ref))):
        proj = _dot(y_ref[...], wb_ref[n])
        gate = jax.nn.sigmoid(g_ref[...].astype(F32) + bias_ref[n:n + 1, :])
        term = gate * proj
        merged = term if merged is None else merged + term
    o_ref[...] = x_ref[...] + _dot(merged.astype(BF16), wo_ref[...])


def _merge(ya, yb, yc, big2d, gate_b, w_branch, w_out, x2d):
    m = x2d.shape[0]
    tm = MERGE_TM
    yspec = pl.BlockSpec((tm, BRANCH_W), lambda i: (i, 0))
    gspec = lambda n: pl.BlockSpec((tm, D_MODEL), lambda i, n=n: (i, COL_GATES // D_MODEL + n))
    return pl.pallas_call(
        _merge_kernel,
        grid=(m // tm,),
        in_specs=[
            yspec, yspec, yspec, gspec(0), gspec(1), gspec(2),
            pl.BlockSpec((N_BRANCH, D_MODEL), lambda i: (0, 0)),
            pl.BlockSpec((N_BRANCH, BRANCH_W, D_MODEL), lambda i: (0, 0, 0)),
            pl.BlockSpec((D_MODEL, D_MODEL), lambda i: (0, 0)),
            pl.BlockSpec((tm, D_MODEL), lambda i: (i, 0)),
        ],
        out_specs=pl.BlockSpec((tm, D_MODEL), lambda i: (i, 0)),
        out_shape=jax.ShapeDtypeStruct((m, D_MODEL), F32),
        compiler_params=_cparams(1),
        name="merge",
    )(ya, yb, yc, big2d, big2d, big2d, gate_b, w_branch, w_out, x2d)


_IN_SPLIT = (("dn_qkvz", 4 * BRANCH_W), ("beta_decay", 2 * HEADS), ("pool_uz", 2 * BRANCH_W),
             ("att_qkvz", 4 * BRANCH_W), ("idx_q", HEADS * IDX_DIM), ("idx_k", IDX_DIM), ("idx_w", HEADS),
             ("gates", N_BRANCH * D_MODEL))
_IN_OFFSET = {name: sum(w for _, w in _IN_SPLIT[:i]) for i, (name, _) in enumerate(_IN_SPLIT)}

PERM_COLS = 256
SIDE_COLS = 128


def _main_source(blk):
    shift = jnp.where(blk < COL_UB // PERM_COLS, _IN_OFFSET["dn_qkvz"] - COL_QA,
                      jnp.where(blk < COL_GATES // PERM_COLS, _IN_OFFSET["pool_uz"] - COL_UB,
                                _IN_OFFSET["gates"] - COL_GATES))
    return blk * PERM_COLS + shift


def _permute_main_kernel(w_ref, o_ref):
    for l in range(o_ref.shape[0]):
        o_ref[l] = w_ref[:, l, :].T.astype(o_ref.dtype)


def _permute_side_kernel(w_ref, ik_ref, bd_ref, iw_ref, o_ref):
    blk = pl.program_id(0)
    depth = o_ref.shape[0]
    n_idx_q = HEADS * IDX_DIM // SIDE_COLS

    @pl.when(blk < n_idx_q)
    def _():
        for l in range(depth):
            o_ref[l] = w_ref[:, l, :].T.astype(o_ref.dtype)

    @pl.when(blk == n_idx_q)
    def _():
        row = lax.broadcasted_iota(jnp.int32, (8, o_ref.shape[1]), 0)
        for l in range(depth):
            idx_w = jnp.where(row < HEADS, iw_ref[:, l, :], 0.0)
            pad = jnp.zeros((SIDE_COLS - IDX_DIM - 16, o_ref.shape[1]), F32)
            rows = jnp.concatenate([ik_ref[:, l, :], bd_ref[:, l, :], idx_w, pad], axis=0)
            o_ref[l] = rows.T.astype(o_ref.dtype)


def _permute_w_in(w_in):
    depth, d, _ = w_in.shape
    w_t = jnp.transpose(w_in, (2, 0, 1))
    el = pl.Element
    rows = lambda n, start: pl.BlockSpec((el(n), el(depth), el(d)), start)
    w_main = pl.pallas_call(
        _permute_main_kernel,
        grid=(BIG_W // PERM_COLS,),
        in_specs=[rows(PERM_COLS, lambda i: (_main_source(i), 0, 0))],
        out_specs=pl.BlockSpec((depth, d, PERM_COLS), lambda i: (0, 0, i)),
        out_shape=jax.ShapeDtypeStruct((depth, d, BIG_W), BF16),
        compiler_params=_cparams(1),
        name="permute_w_main",
    )(w_t)
    n_idx_q = HEADS * IDX_DIM // SIDE_COLS
    w_side = pl.pallas_call(
        _permute_side_kernel,
        grid=(SMALL_W // SIDE_COLS,),
        in_specs=[
            rows(SIDE_COLS, lambda i: (_IN_OFFSET["idx_q"] + jnp.minimum(i, n_idx_q - 1) * SIDE_COLS, 0, 0)),
            rows(IDX_DIM, lambda i: (_IN_OFFSET["idx_k"], 0, 0)),
            rows(8, lambda i: (_IN_OFFSET["beta_decay"], 0, 0)),
            rows(8, lambda i: (_IN_OFFSET["idx_w"], 0, 0)),
        ],
        out_specs=pl.BlockSpec((depth, d, SIDE_COLS), lambda i: (0, 0, i)),
        out_shape=jax.ShapeDtypeStruct((depth, d, SMALL_W), BF16),
        compiler_params=_cparams(1),
        name="permute_w_side",
    )(w_t, w_t, w_t, w_t)
    return w_main, w_side


def _rope_tables(t):
    def base(dim):
        inv_freq = ROPE_THETA ** (-jnp.arange(0, dim, 2, dtype=F32) / dim)
        ang = jnp.arange(t, dtype=F32)[:, None] * inv_freq[None, :]
        return jnp.cos(ang), jnp.sin(ang)

    cos_a, sin_a = base(HEAD_DIM)
    cos = jnp.concatenate([cos_a, cos_a], axis=-1)
    sin = jnp.concatenate([-sin_a, sin_a], axis=-1)
    cos_i, sin_i = base(IDX_DIM)
    zero = jnp.zeros_like(sin_i)
    icos = jnp.tile(jnp.concatenate([cos_i, cos_i], axis=-1), (1, HEADS))
    isin_up = jnp.tile(jnp.concatenate([-sin_i, zero], axis=-1), (1, HEADS))
    isin_dn = jnp.tile(jnp.concatenate([zero, sin_i], axis=-1), (1, HEADS))
    return cos, sin, icos, isin_up, isin_dn


def _lane_row(vals, offset):
    d, n = vals.shape
    return jnp.zeros((d, 1, HEAD_DIM), F32).at[:, 0, offset:offset + n].set(vals.astype(F32))


def kernel(x, norm_g, w_in, gate_b, conv_w, a_log, dt_bias, dn_onorm, pool_w, pool_scale, q_norm, k_norm,
           w_branch, w_out):
    b, t, d = x.shape
    depth = norm_g.shape[0]
    w_main, w_side = _permute_w_in(w_in)
    wb16 = w_branch.astype(BF16)
    wo16 = w_out.astype(BF16)
    pw16 = pool_w.astype(BF16)
    alog_rows = _lane_row(a_log, SM_DECAY)
    dtb_rows = _lane_row(dt_bias, SM_DECAY)
    cos, sin, *idx_tabs = _rope_tables(t)
    idx_tabs = tuple(idx_tabs)
    x2d = x.reshape(b * t, d)
    for layer in range(depth):
        big2d, small2d = _in_projection(x2d, norm_g[layer][None, :], w_main, w_side, conv_w[layer], layer,
                                        t // PROJ_TM)
        big3d = big2d.reshape(b, t, BIG_W)
        small3d = small2d.reshape(b, t, SMALL_W)
        ya = _deltanet_branch(big3d, small3d, alog_rows[layer], dtb_rows[layer], dn_onorm[layer][None, :])
        yb = _pool_branch(big3d, pw16[layer], pool_scale[layer][None, :])
        half = HEAD_DIM // 2
        gain_tabs = (cos * q_norm[layer], sin * jnp.roll(q_norm[layer], half),
                     cos * k_norm[layer], sin * jnp.roll(k_norm[layer], half))
        prep = _attn_prep(big3d, small3d, gain_tabs + idx_tabs)
        yc = _dsa_branch(big3d, *prep)
        x2d = _merge(ya.reshape(b * t, BRANCH_W), yb.reshape(b * t, BRANCH_W), yc.reshape(b * t, BRANCH_W),
                     big2d, gate_b[layer], wb16[layer], wo16[layer], x2d)
    return x2d.reshape(b, t, d)
```

```python
import functools

import jax
import jax.numpy as jnp
from jax import lax
from jax.experimental import pallas as pl
from jax.experimental.pallas import tpu as pltpu

F32 = jnp.float32
BF16 = jnp.bfloat16

D_MODEL = 1024
HEADS = 4
HEAD_DIM = 128
BRANCH_W = HEADS * HEAD_DIM
DN_CONV = 4
POOL_WINDOWS = (2, 4, 8, 16)
POOL_GROUP = 128
IDX_DIM = 64
TOPK_MAX = 256
ROPE_THETA = 10000.0
NORM_EPS = 1e-6
N_BRANCH = 3

COL_QA, COL_KA, COL_VA, COL_ZA = 0, 512, 1024, 1536
COL_UB, COL_ZB = 2048, 2560
COL_QC, COL_KC, COL_VC, COL_ZC = 3072, 3584, 4096, 4608
COL_GATES = 5120
BIG_W = 8192
SMALL_W = 384
SM_BETA, SM_DECAY, SM_IDXW = 64, 68, 72

V7X_VMEM_LIMIT = 56 * 1024 * 1024

PROJ_TM = 512
PROJ_TN = 512
MERGE_TM = 1024
PREP_TT = 1024
POOL_TT = 256
POOL_HALO = 16
DN_CHUNK = 128
CONV_HALO = 8
CONV_ROWS = 128
DN_BASE = 8
DN_PAR = 2
ATT_QB = 256
ATT_KC = 256
NUM_BISECT = 36
BISECT_UNROLL = 4
COUNT_LANES = 4
TRIP_CHUNKS = (4, 2, 1)
MASK_NEG = -1e30
LOG2_E = 1.4426950408889634


def _nt_dot(a, b):
    return lax.dot_general(a, b, (((1,), (1,)), ((), ())), preferred_element_type=F32)


def _dot(a, b):
    return jnp.dot(a, b, preferred_element_type=F32)


def _silu(x):
    return x * jax.nn.sigmoid(x)


def _cparams(n_axes):
    return pltpu.CompilerParams(dimension_semantics=("arbitrary",) * n_axes,
                                vmem_limit_bytes=V7X_VMEM_LIMIT)


def _inproj_kernel(x_ref, g_ref, w_ref, cw_ref, big_ref, small_ref, conv_buf, *, tiles_per_seq):
    step = pl.program_id(0)

    @pl.when(step == 0)
    def _():
        conv_buf[...] = jnp.zeros(conv_buf.shape, F32)

    x = x_ref[...]
    ms = jnp.mean(x * x, axis=-1, keepdims=True)
    h = ((x * lax.rsqrt(ms + NORM_EPS)) * g_ref[...]).astype(BF16)
    seq_start = (step % tiles_per_seq) == 0
    hd = HEAD_DIM
    halo = CONV_HALO

    def stage(seg):
        c = seg * PROJ_TN
        prev = jnp.where(seq_start, 0.0, conv_buf[seg, PROJ_TM:PROJ_TM + halo, :])
        conv_buf[seg, 0:halo, :] = prev
        conv_buf[seg, halo:PROJ_TM + halo, :] = _dot(h, w_ref[:, c:c + PROJ_TN])

    def conv_piece(seg, r, k, anchor):
        c = seg * PROJ_TN + k * hd
        cw = cw_ref[:, c:c + hd] + anchor
        rows = conv_buf[seg, r:r + CONV_ROWS + halo, k * hd:(k + 1) * hd]
        acc = rows * cw[DN_CONV - 1:DN_CONV, :]
        for s in range(1, DN_CONV):
            acc = acc + pltpu.roll(rows, s, axis=0) * cw[DN_CONV - 1 - s:DN_CONV - s, :]
        z = _silu(acc[halo:, :])
        if c < COL_VA:
            scale = hd ** -0.5 if c < COL_KA else 1.0
            z = z * (lax.rsqrt(jnp.sum(z * z, axis=-1, keepdims=True) + NORM_EPS) * scale)
        big_ref[r:r + CONV_ROWS, c:c + hd] = z.astype(big_ref.dtype)

    plain = list(range(COL_ZA, BIG_W, PROJ_TN))
    n_slot = 4
    for seg in range(COL_ZA // PROJ_TN):
        stage(seg)
        pieces = [(seg, r, k) for r in range(0, PROJ_TM, CONV_ROWS) for k in range(HEADS)]
        per_slot = -(-len(pieces) // n_slot)
        for _ in range(n_slot):
            c = plain.pop(0)
            y = _dot(h, w_ref[:, c:c + PROJ_TN])
            big_ref[:, c:c + PROJ_TN] = y.astype(big_ref.dtype)
            for j, piece in enumerate(pieces[:per_slot]):
                row = (j + 1) * PROJ_TM // per_slot - 1
                conv_piece(*piece, y[row:row + 1, 0:hd] * 0.0)
            pieces = pieces[per_slot:]
    for c in plain:
        big_ref[:, c:c + PROJ_TN] = _dot(h, w_ref[:, c:c + PROJ_TN]).astype(big_ref.dtype)
    small_ref[...] = _dot(h, w_ref[:, BIG_W:])


def _in_projection(x2d, g_row, w_p, conv_w, layer, tiles_per_seq):
    m = x2d.shape[0]
    return pl.pallas_call(
        functools.partial(_inproj_kernel, tiles_per_seq=tiles_per_seq),
        grid=(m // PROJ_TM,),
        in_specs=[
            pl.BlockSpec((PROJ_TM, D_MODEL), lambda i: (i, 0)),
            pl.BlockSpec((1, D_MODEL), lambda i: (0, 0)),
            pl.BlockSpec((None, D_MODEL, BIG_W + SMALL_W), lambda i: (layer, 0, 0), pipeline_mode=pl.Buffered(1)),
            pl.BlockSpec((DN_CONV, 3 * BRANCH_W), lambda i: (0, 0)),
        ],
        out_specs=[
            pl.BlockSpec((PROJ_TM, BIG_W), lambda i: (i, 0)),
            pl.BlockSpec((PROJ_TM, SMALL_W), lambda i: (i, 0)),
        ],
        out_shape=[
            jax.ShapeDtypeStruct((m, BIG_W), BF16),
            jax.ShapeDtypeStruct((m, SMALL_W), F32),
        ],
        scratch_shapes=[pltpu.VMEM((3, PROJ_TM + CONV_HALO, BRANCH_W), F32)],
        compiler_params=_cparams(1),
        name="in_projection",
    )(x2d, g_row, w_p, conv_w)


def _pool_kernel(u_ref, z_ref, pw_ref, ps_ref, y_ref):
    t_len = u_ref.shape[0]
    for t in range(t_len // POOL_TT):
        r0 = t * POOL_TT
        cur = u_ref[r0:r0 + POOL_TT, :].astype(F32)
        if t == 0:
            prev = jnp.zeros((POOL_HALO, cur.shape[1]), F32)
        else:
            prev = u_ref[r0 - POOL_HALO:r0, :].astype(F32)
        win_rows = jnp.concatenate([prev, cur], axis=0)
        pos = r0 + lax.broadcasted_iota(jnp.int32, (POOL_TT, 1), 0)
        for gi, win in enumerate(POOL_WINDOWS):
            sl = slice(gi * POOL_GROUP, (gi + 1) * POOL_GROUP)
            s = win_rows[:, sl]
            shift = 1
            while shift < win:
                s = s + pltpu.roll(s, shift, axis=0)
                shift *= 2
            count = jnp.minimum(pos + 1, win).astype(F32)
            pooled = s[POOL_HALO:, :] / count - cur[:, sl]
            mixed = _dot(pooled.astype(BF16), pw_ref[gi])
            zg = z_ref[r0:r0 + POOL_TT, sl].astype(F32)
            y_ref[r0:r0 + POOL_TT, sl] = (mixed * ps_ref[:, sl] * _silu(zg)).astype(y_ref.dtype)


def _pool_branch(big3d, pool_w, pool_scale_row):
    b, t, _ = big3d.shape
    width = POOL_GROUP * len(POOL_WINDOWS)
    return pl.pallas_call(
        _pool_kernel,
        grid=(b,),
        in_specs=[
            pl.BlockSpec((None, t, width), lambda i: (i, 0, COL_UB // width)),
            pl.BlockSpec((None, t, width), lambda i: (i, 0, COL_ZB // width)),
            pl.BlockSpec((len(POOL_WINDOWS), POOL_GROUP, POOL_GROUP), lambda i: (0, 0, 0)),
            pl.BlockSpec((1, width), lambda i: (0, 0)),
        ],
        out_specs=pl.BlockSpec((None, t, width), lambda i: (i, 0, 0)),
        out_shape=jax.ShapeDtypeStruct((b, t, width), BF16),
        compiler_params=_cparams(1),
        name="pool_branch",
    )(big3d, big3d, pool_w, pool_scale_row)


def _softplus(x):
    return jnp.maximum(x, 0.0) + jnp.log1p(jnp.exp(-jnp.abs(x)))


def _split_bf16(a):
    hi = a.astype(BF16)
    lo = (a - hi.astype(F32)).astype(BF16)
    return hi, lo


def _dot_split(a, b):
    a_hi, a_lo = a
    b_hi, b_lo = b
    return _dot(jnp.concatenate([a_hi, a_lo, a_hi], axis=1), jnp.concatenate([b_hi, b_hi, b_lo], axis=0))


def _dn_kernel(q_ref, k_ref, v_ref, z_ref, sm_ref, alog_ref, dtb_ref, on_ref,
               y_ref, u_s, w_s, qg_s, a_s, kdt_s, el_s, st_s):
    t_len = q_ref.shape[0]
    n_chunks = t_len // DN_CHUNK
    c = DN_CHUNK
    hd = HEAD_DIM
    row = lax.broadcasted_iota(jnp.int32, (c, c), 0)
    col = lax.broadcasted_iota(jnp.int32, (c, c), 1)
    tril = row >= col
    strict = row > col
    tril16 = tril.astype(BF16)
    tril16x3 = jnp.concatenate([tril16, tril16, tril16], axis=1)
    eye_f = (row == col).astype(F32)
    base_blk = (row // DN_BASE) == (col // DN_BASE)
    pair_blks = []
    size = DN_BASE
    while size < c:
        pair_blks.append(((row // (2 * size)) == (col // (2 * size))) & ((row // size) != (col // size)))
        size *= 2

    def prepare(gi, carry):
        cis = [gi * DN_PAR + j for j in range(DN_PAR)]
        starts = [pl.multiple_of(ci * c, c) for ci in cis]
        beta_all, gc_all = [], []
        for start in starts:
            sm = sm_ref[pl.ds(start, c), :]
            beta_all.append(jax.nn.sigmoid(sm))
            g_all = -jnp.exp(alog_ref[...]) * _softplus(sm + dtb_ref[...])
            g_hi = g_all.astype(BF16)
            g_r = g_all - g_hi.astype(F32)
            g_mid = g_r.astype(BF16)
            g_lo = (g_r - g_mid.astype(F32)).astype(BF16)
            gc_all.append(_dot(tril16x3, jnp.concatenate([g_hi, g_mid, g_lo], axis=0)))
        items = [(j, h) for j in range(DN_PAR) for h in range(HEADS)]
        ids = range(len(items))
        sls = [slice(h * hd, (h + 1) * hd) for _, h in items]
        qn = [q_ref[pl.ds(starts[j], c), sls[i]].astype(F32) for i, (j, _) in enumerate(items)]
        kn = [k_ref[pl.ds(starts[j], c), sls[i]].astype(F32) for i, (j, _) in enumerate(items)]
        beta_b = [jnp.broadcast_to(beta_all[j][:, SM_BETA + h:SM_BETA + h + 1], (c, hd)) for j, h in items]
        gc = [jnp.broadcast_to(gc_all[j][:, SM_DECAY + h:SM_DECAY + h + 1], (c, hd)) for j, h in items]
        decay = [jnp.exp(jnp.where(tril, gc[i] - gc[i].T, -jnp.inf)) for i in ids]
        kb = [kn[i] * beta_b[i] for i in ids]
        kn16 = [kn[i].astype(BF16) for i in ids]
        kq = [_nt_dot(jnp.concatenate([kb[i].astype(BF16), qn[i].astype(BF16)], axis=0), kn16[i]) for i in ids]
        lmat = [jnp.where(strict, kq[i][:c, :] * decay[i], 0.0) for i in ids]
        l_hi = [lmat[i].astype(BF16) for i in ids]
        l_lo = [(lmat[i] - l_hi[i].astype(F32)).astype(BF16) for i in ids]
        zero16 = jnp.zeros((c, c), BF16)
        d_parts = [(jnp.where(base_blk, l_hi[i], zero16), jnp.where(base_blk, l_lo[i], zero16)) for i in ids]
        tmat = [eye_f - jnp.where(base_blk, lmat[i], 0.0) for i in ids]
        power = [_dot_split(d_parts[i], d_parts[i]) for i in ids]
        span = 2
        while span < DN_BASE:
            parts = [_split_bf16(power[i]) for i in ids]
            t_parts = [_split_bf16(tmat[i]) for i in ids]
            span *= 2
            if span < DN_BASE:
                both = [_dot_split((jnp.concatenate([t_parts[i][0], parts[i][0]], axis=0),
                                    jnp.concatenate([t_parts[i][1], parts[i][1]], axis=0)), parts[i]) for i in ids]
                tmat = [tmat[i] + both[i][:c, :] for i in ids]
                power = [both[i][c:, :] for i in ids]
            else:
                tmat = [tmat[i] + _dot_split(t_parts[i], parts[i]) for i in ids]
        for pair_blk in pair_blks:
            t_parts = [_split_bf16(tmat[i]) for i in ids]
            off = [(jnp.where(pair_blk, l_hi[i], zero16), jnp.where(pair_blk, l_lo[i], zero16)) for i in ids]
            cx = [_dot_split(off[i], t_parts[i]) for i in ids]
            tmat = [tmat[i] - _dot_split(t_parts[i], _split_bf16(cx[i])) for i in ids]
        for i, (j, h) in enumerate(items):
            sl = sls[i]
            start = starts[j]
            t16 = tmat[i].astype(BF16)
            egc = jnp.exp(gc[i])
            vb = v_ref[pl.ds(start, c), sl].astype(F32) * beta_b[i]
            uw = _dot(t16, jnp.concatenate([vb.astype(BF16), (kb[i] * egc).astype(BF16)], axis=1))
            u_s[pl.ds(start, c), sl] = uw[:, :hd]
            w_s[pl.ds(start, c), sl] = uw[:, hd:].astype(BF16)
            qg_s[pl.ds(start, c), sl] = (qn[i] * egc).astype(BF16)
            a_s[pl.ds(start, c), sl] = jnp.where(tril, kq[i][c:, :] * decay[i], 0.0).astype(BF16)
            g_last = gc[i][c - 1:c, :]
            kd = kn[i] * jnp.exp(g_last - gc[i])
            kdt_s[pl.ds(start, c), sl] = kd.T.astype(BF16)
            el_s[pl.ds(pl.multiple_of(cis[j] * 8, 8), 8), sl] = jnp.broadcast_to(jnp.exp(g_last), (8, hd))
        return carry

    def scan_chunk(ci):
        start = pl.multiple_of(ci * c, c)
        hs = range(HEADS)
        sls = [slice(h * hd, (h + 1) * hd) for h in hs]
        state = [st_s[h] for h in hs]
        s16 = [state[h].astype(BF16) for h in hs]
        v_new = [u_s[pl.ds(start, c), sls[h]] - _dot(w_s[pl.ds(start, c), sls[h]], s16[h]) for h in hs]
        v16 = [v_new[h].astype(BF16) for h in hs]
        for h in hs:
            e_last = el_s[pl.ds(pl.multiple_of(ci * 8, 8), 8), sls[h]][0:1, :]
            st_s[h] = state[h] * e_last + _dot(kdt_s[pl.ds(start, c), sls[h]], v16[h])
        for h in hs:
            sl = sls[h]
            o = _dot(qg_s[pl.ds(start, c), sl], s16[h]) + _dot(a_s[pl.ds(start, c), sl], v16[h])
            on = o * lax.rsqrt(jnp.mean(o * o, axis=-1, keepdims=True) + NORM_EPS) * on_ref[...]
            zg = z_ref[pl.ds(start, c), sl].astype(F32)
            y_ref[pl.ds(start, c), sl] = (on * _silu(zg)).astype(y_ref.dtype)

    def scan_group(gi):
        for j in range(DN_PAR):
            scan_chunk(gi * DN_PAR + j)

    n_groups = n_chunks // DN_PAR
    st_s[...] = jnp.zeros(st_s.shape, F32)
    prepare(0, 0)

    def trip(gi, carry):
        scan_group(gi - 1)
        return prepare(gi, carry)

    lax.fori_loop(1, n_groups, trip, 0)
    scan_group(n_groups - 1)


def _deltanet_branch(big3d, small3d, alog_row, dtb_row, onorm_row):
    b, t, _ = big3d.shape
    hd = HEAD_DIM
    w = BRANCH_W
    seq_spec = lambda col: pl.BlockSpec((None, t, w), lambda i, col=col: (i, 0, col // w))
    row_spec = pl.BlockSpec((1, hd), lambda i: (0, 0))
    return pl.pallas_call(
        _dn_kernel,
        grid=(b,),
        in_specs=[
            seq_spec(COL_QA), seq_spec(COL_KA), seq_spec(COL_VA), seq_spec(COL_ZA),
            pl.BlockSpec((None, t, hd), lambda i: (i, 0, SMALL_W // hd - 1)),
            row_spec, row_spec, row_spec,
        ],
        out_specs=pl.BlockSpec((None, t, w), lambda i: (i, 0, 0)),
        out_shape=jax.ShapeDtypeStruct((b, t, w), BF16),
        scratch_shapes=[
            pltpu.VMEM((t, w), F32),
            pltpu.VMEM((t, w), BF16),
            pltpu.VMEM((t, w), BF16),
            pltpu.VMEM((t, w), BF16),
            pltpu.VMEM((t, w), BF16),
            pltpu.VMEM((8 * t // DN_CHUNK, w), F32),
            pltpu.VMEM((HEADS, hd, hd), F32),
        ],
        compiler_params=_cparams(1),
        name="deltanet_branch",
    )(big3d, big3d, big3d, big3d, small3d, alog_row, dtb_row, onorm_row)


def _attn_prep_kernel(q_ref, k_ref, v_ref, sm_ref, qc_ref, qs_ref, kc_ref, ks_ref,
                      icos_ref, isin_up_ref, isin_dn_ref,
                      qo_ref, ko_ref, vt_ref, qio_ref, kia_ref, kib_ref, wt_ref):
    hd = HEAD_DIM
    lane_r = lax.broadcasted_iota(jnp.int32, (hd, hd), 0)
    lane_c = lax.broadcasted_iota(jnp.int32, (hd, hd), 1)
    swap_halves = (lane_r == (lane_c + hd // 2) % hd).astype(BF16)

    def norm_rope(ref, cos_g, sin_g, out_ref, scale):
        for h in range(HEADS):
            sl = slice(h * hd, (h + 1) * hd)
            x16 = ref[:, sl]
            x = x16.astype(F32)
            inv = lax.rsqrt(jnp.mean(x * x, axis=-1, keepdims=True) + NORM_EPS) * scale
            out_ref[:, sl] = ((x * cos_g + _dot(x16, swap_halves) * sin_g) * inv).astype(out_ref.dtype)

    norm_rope(q_ref, qc_ref[...], qs_ref[...], qo_ref, hd ** -0.5 * LOG2_E)
    norm_rope(k_ref, kc_ref[...], ks_ref[...], ko_ref, 1.0)
    w = v_ref.shape[1]
    eye = (lax.broadcasted_iota(jnp.int32, (w, w), 0) == lax.broadcasted_iota(jnp.int32, (w, w), 1)).astype(BF16)
    vt_ref[...] = _nt_dot(eye, v_ref[...]).astype(vt_ref.dtype)

    sm = sm_ref[...]
    iq_w = HEADS * IDX_DIM
    half = IDX_DIM // 2
    iq = sm[:, :iq_w]
    iq_r = (iq * icos_ref[...] + pltpu.roll(iq, iq_w - half, axis=1) * isin_up_ref[...]
            + pltpu.roll(iq, half, axis=1) * isin_dn_ref[...])
    qio_ref[...] = iq_r.astype(qio_ref.dtype)
    last = sm[:, iq_w:]
    lw = last.shape[1]
    ik_r = (last * icos_ref[:, :lw] + pltpu.roll(last, lw - half, axis=1) * isin_up_ref[:, :lw]
            + pltpu.roll(last, half, axis=1) * isin_dn_ref[:, :lw])
    lane = lax.broadcasted_iota(jnp.int32, ik_r.shape, 1)
    ik_r = jnp.where(lane < IDX_DIM, ik_r, 0.0)
    kia_ref[...] = ik_r.astype(kia_ref.dtype)
    kib_ref[...] = pltpu.roll(ik_r, IDX_DIM, axis=1).astype(kib_ref.dtype)
    wt = last.T
    wt_ref[...] = wt[SM_IDXW:SM_IDXW + 8, :] * (HEADS ** -0.5 * IDX_DIM ** -0.5)


def _attn_prep(big3d, small3d, tabs):
    b, t, _ = big3d.shape
    w = BRANCH_W
    tt = PREP_TT
    iq_w = HEADS * IDX_DIM
    seq = lambda col: pl.BlockSpec((None, tt, w), lambda j, i, col=col: (i, j, col // w))
    tab = lambda width: pl.BlockSpec((tt, width), lambda j, i: (j, 0))
    return pl.pallas_call(
        _attn_prep_kernel,
        grid=(t // tt, b),
        in_specs=[
            seq(COL_QC), seq(COL_KC), seq(COL_VC),
            pl.BlockSpec((None, tt, SMALL_W), lambda j, i: (i, j, 0)),
            tab(HEAD_DIM), tab(HEAD_DIM), tab(HEAD_DIM), tab(HEAD_DIM), tab(iq_w), tab(iq_w), tab(iq_w),
        ],
        out_specs=[
            pl.BlockSpec((None, tt, w), lambda j, i: (i, j, 0)),
            pl.BlockSpec((None, tt, w), lambda j, i: (i, j, 0)),
            pl.BlockSpec((None, w, tt), lambda j, i: (i, 0, j)),
            pl.BlockSpec((None, tt, iq_w), lambda j, i: (i, j, 0)),
            pl.BlockSpec((None, tt, 2 * IDX_DIM), lambda j, i: (i, j, 0)),
            pl.BlockSpec((None, tt, 2 * IDX_DIM), lambda j, i: (i, j, 0)),
            pl.BlockSpec((None, 8, tt), lambda j, i: (i, 0, j)),
        ],
        out_shape=[
            jax.ShapeDtypeStruct((b, t, w), BF16),
            jax.ShapeDtypeStruct((b, t, w), BF16),
            jax.ShapeDtypeStruct((b, w, t), BF16),
            jax.ShapeDtypeStruct((b, t, iq_w), BF16),
            jax.ShapeDtypeStruct((b, t, 2 * IDX_DIM), BF16),
            jax.ShapeDtypeStruct((b, t, 2 * IDX_DIM), BF16),
            jax.ShapeDtypeStruct((b, 8, t), F32),
        ],
        compiler_params=_cparams(2),
        name="attn_prep",
    )(big3d, big3d, big3d, small3d, *tabs)


def _dsa_kernel(q_ref, k_ref, vt_ref, qi_ref, kia_ref, kib_ref, wt_ref, z_ref, y_ref, s_ref, acc_ref, *, topk):
    qb, kc = ATT_QB, ATT_KC
    blk_i = pl.program_id(1)
    n_kc = blk_i + 1
    q_pos = blk_i * qb + lax.broadcasted_iota(jnp.int32, (1, qb), 1)
    qi = qi_ref[...]
    wt = wt_ref[...]
    inf = jnp.inf

    def col_sum(x):
        return x.reshape(kc // 8, 8, qb).sum(axis=0)

    def total(x):
        return jnp.sum(x, axis=0, keepdims=True)

    def score_chunk(ci, carry, diagonal, width=kc):
        vmax, vmin, min_pos, n_pos, n_nonneg = carry
        off = pl.multiple_of(ci * width, width)
        ka = kia_ref[pl.ds(off, width), :]
        kb = kib_ref[pl.ds(off, width), :]
        s = jnp.zeros((width, qb), F32)
        for h in range(HEADS):
            kk = ka if h % 2 == 0 else kb
            qq = qi[:, (h // 2) * 2 * IDX_DIM:(h // 2 + 1) * 2 * IDX_DIM]
            s = s + jnp.maximum(_nt_dot(kk, qq), 0.0) * wt[h:h + 1, :]
        s = jnp.where(s == 0.0, 0.0, s)
        if diagonal:
            key_pos = off + lax.broadcasted_iota(jnp.int32, (width, 1), 0)
            causal = key_pos <= q_pos
            sc = jnp.where(causal, s, -inf)
            s_hi = jnp.where(causal, s, inf)
        else:
            sc = s_hi = s
        s_ref[pl.ds(off, width), :] = sc
        vmax = jnp.maximum(vmax, jnp.max(sc, axis=0, keepdims=True))
        vmin = jnp.minimum(vmin, jnp.min(s_hi, axis=0, keepdims=True))
        min_pos = jnp.minimum(min_pos, jnp.min(jnp.where(sc > 0.0, sc, inf), axis=0, keepdims=True))
        n_pos = n_pos + jnp.where(sc > 0.0, 1.0, 0.0).reshape(width // 8, 8, qb).sum(axis=0)
        n_nonneg = n_nonneg + jnp.where(sc >= 0.0, 1.0, 0.0).reshape(width // 8, 8, qb).sum(axis=0)
        return vmax, vmin, min_pos, n_pos, n_nonneg

    row_inf = jnp.full((1, qb), inf, F32)
    zeros8 = jnp.zeros((8, qb), F32)
    stats = (-row_inf, row_inf, row_inf, zeros8, zeros8)
    done_chunks = 0
    for group in TRIP_CHUNKS:
        n_trips = (blk_i - done_chunks) // group
        first = done_chunks // group
        stats = lax.fori_loop(first, first + n_trips,
                              functools.partial(score_chunk, diagonal=False, width=group * kc), stats)
        done_chunks = done_chunks + n_trips * group
    vmax, vmin, min_pos, n_pos, n_nonneg = score_chunk(blk_i, stats, diagonal=True)
    n_pos = total(n_pos)
    n_nonneg = total(n_nonneg)

    def count_ge(thr):
        def body(ci, acc, group):
            for j in range(group):
                off = pl.multiple_of((ci * group + j) * kc, kc)
                hit = jnp.where(s_ref[pl.ds(off, kc), :] >= thr, 1.0, 0.0)
                acc = acc + hit.reshape(kc // (8 * COUNT_LANES), COUNT_LANES * 8, qb).sum(axis=0)
            return acc
        acc = jnp.zeros((COUNT_LANES * 8, qb), F32)
        done_chunks = 0
        for group in TRIP_CHUNKS:
            n_trips = (n_kc - done_chunks) // group
            first = done_chunks // group
            acc = lax.fori_loop(first, first + n_trips, functools.partial(body, group=group), acc)
            done_chunks = done_chunks + n_trips * group
        return total(acc)

    k_sel = jnp.minimum(q_pos + 1, topk).astype(F32)
    n_causal = (q_pos + 1).astype(F32)
    at_zero = (n_pos < k_sel) & (k_sel <= n_nonneg)
    above = k_sel <= n_pos
    lo = jnp.where(at_zero, 0.0, jnp.where(above, min_pos, vmin))
    hi = jnp.where(at_zero, min_pos, jnp.where(above, vmax + (jnp.abs(vmax) + 1.0), 0.0))
    c_lo = jnp.where(at_zero, n_nonneg, jnp.where(above, n_pos, n_causal))
    c_hi = jnp.where(at_zero, n_pos, jnp.where(above, 0.0, n_nonneg))
    done = jnp.where(at_zero | (c_lo == k_sel), 1.0, 0.0)

    def n_open(d):
        return jnp.sum(1.0 - d)

    def bisect_cond(carry):
        it, n_left = carry[0], carry[1]
        return (it < NUM_BISECT) & (n_left > 0.0)

    def bisect_body(carry):
        it, _, lo, hi, c_lo, c_hi, done = carry
        for _ in range(BISECT_UNROLL):
            mid = 0.5 * lo + 0.5 * hi
            cnt = count_ge(mid)
            live = done < 0.5
            up = (cnt >= k_sel) & live
            dn = (cnt < k_sel) & live
            lo = jnp.where(up, mid, lo)
            c_lo = jnp.where(up, cnt, c_lo)
            hi = jnp.where(dn, mid, hi)
            c_hi = jnp.where(dn, cnt, c_hi)
            done = jnp.where(c_lo == k_sel, 1.0, done)
        return it + BISECT_UNROLL, n_open(done), lo, hi, c_lo, c_hi, done

    _, _, lo, hi, c_lo, c_hi, done = lax.while_loop(
        bisect_cond, bisect_body, (jnp.int32(0), n_open(done), lo, hi, c_lo, c_hi, done))
    need = k_sel - c_hi
    n_tied = jnp.sum(jnp.where(c_lo - c_hi > need, 1.0, 0.0))

    def tie_mask():
        tri = (lax.broadcasted_iota(jnp.int32, (kc, kc), 1)
               < lax.broadcasted_iota(jnp.int32, (kc, kc), 0)).astype(BF16)

        def mask_body(ci, seen):
            off = pl.multiple_of(ci * kc, kc)
            blk = s_ref[pl.ds(off, kc), :]
            tie = jnp.where((blk >= lo) & (blk < hi), 1.0, 0.0)
            rank = _dot(tri, tie.astype(BF16)) + seen
            sel = (blk >= hi) | ((tie > 0.5) & (rank < need))
            s_ref[pl.ds(off, kc), :] = jnp.where(sel, 1.0, 0.0)
            return seen + total(col_sum(tie))

        lax.fori_loop(0, n_kc, mask_body, jnp.zeros((1, qb), F32))

    pl.when(n_tied > 0.0)(tie_mask)
    sel_thr = jnp.where(n_tied > 0.0, 0.5, lo)

    hs = range(HEADS)
    sls = [slice(h * HEAD_DIM, (h + 1) * HEAD_DIM) for h in hs]
    qh = [q_ref[:, sl] for sl in sls]
    acc_ref[...] = jnp.zeros(acc_ref.shape, F32)

    def att_body(ci, carry, width=kc):
        ms, ls = carry
        off = pl.multiple_of(ci * width, width)
        sel = s_ref[pl.ds(off, width), :] >= sel_thr
        lm = [jnp.where(sel, _nt_dot(k_ref[pl.ds(off, width), sls[h]], qh[h]), MASK_NEG) for h in hs]
        m_new = [jnp.maximum(ms[h], jnp.max(lm[h], axis=0, keepdims=True)) for h in hs]
        p = [jnp.exp2(lm[h] - m_new[h]) for h in hs]
        alpha = [jnp.exp2(ms[h] - m_new[h]) for h in hs]
        l_new = [alpha[h] * ls[h] + jnp.sum(p[h], axis=0, keepdims=True) for h in hs]
        pv = [_dot(vt_ref[sls[h], pl.ds(off, width)], p[h].astype(BF16)) for h in hs]
        for h in hs:
            acc_ref[sls[h], :] = alpha[h] * acc_ref[sls[h], :] + pv[h]
        return tuple(m_new), tuple(l_new)

    row_neg = jnp.full((1, qb), MASK_NEG, F32)
    row_zero = jnp.zeros((1, qb), F32)
    carry = ((row_neg,) * HEADS, (row_zero,) * HEADS)
    done_chunks = 0
    for group in TRIP_CHUNKS:
        n_trips = (n_kc - done_chunks) // group
        first = done_chunks // group
        carry = lax.fori_loop(first, first + n_trips, functools.partial(att_body, width=group * kc), carry)
        done_chunks = done_chunks + n_trips * group
    _, ls = carry
    for h in hs:
        o = (acc_ref[sls[h], :] / ls[h]).T
        y_ref[:, sls[h]] = (o * _silu(z_ref[:, sls[h]].astype(F32))).astype(y_ref.dtype)


def _dsa_branch(big3d, q_r, k_r, v_t, qi_r, ki_a, ki_b, w_t):
    b, t, _ = big3d.shape
    w = BRANCH_W
    qb = ATT_QB
    topk = min(TOPK_MAX, t // 4)
    return pl.pallas_call(
        functools.partial(_dsa_kernel, topk=topk),
        grid=(b, t // qb),
        in_specs=[
            pl.BlockSpec((None, qb, w), lambda i, j: (i, j, 0)),
            pl.BlockSpec((None, t, w), lambda i, j: (i, 0, 0)),
            pl.BlockSpec((None, w, t), lambda i, j: (i, 0, 0)),
            pl.BlockSpec((None, qb, HEADS * IDX_DIM), lambda i, j: (i, j, 0)),
            pl.BlockSpec((None, t, 2 * IDX_DIM), lambda i, j: (i, 0, 0)),
            pl.BlockSpec((None, t, 2 * IDX_DIM), lambda i, j: (i, 0, 0)),
            pl.BlockSpec((None, 8, qb), lambda i, j: (i, 0, j)),
            pl.BlockSpec((None, qb, w), lambda i, j: (i, j, COL_ZC // w)),
        ],
        out_specs=pl.BlockSpec((None, qb, w), lambda i, j: (i, j, 0)),
        out_shape=jax.ShapeDtypeStruct((b, t, w), BF16),
        scratch_shapes=[pltpu.VMEM((t, qb), F32),
                        pltpu.VMEM((w, qb), F32)],
        compiler_params=_cparams(2),
        name="sparse_attention",
    )(q_r, k_r, v_t, qi_r, ki_a, ki_b, w_t, big3d)


def _merge_kernel(ya_ref, yb_ref, yc_ref, ga_ref, gb_ref, gc_ref, bias_ref, wb_ref, wo_ref, x_ref, o_ref):
    merged = None
    for n, (y_ref, g_ref) in enumerate(((ya_ref, ga_ref), (yb_ref, gb_ref), (yc_ref, gc_ref))):
        proj = _dot(y_ref[...], wb_ref[n])
        gate = jax.nn.sigmoid(g_ref[...].astype(F32) + bias_ref[n:n + 1, :])
        term = gate * proj
        merged = term if merged is None else merged + term
    o_ref[...] = x_ref[...] + _dot(merged.astype(BF16), wo_ref[...])


def _merge(ya, yb, yc, big2d, gate_b, w_branch, w_out, x2d):
    m = x2d.shape[0]
    tm = MERGE_TM
    yspec = pl.BlockSpec((tm, BRANCH_W), lambda i: (i, 0))
    gspec = lambda n: pl.BlockSpec((tm, D_MODEL), lambda i, n=n: (i, COL_GATES // D_MODEL + n))
    return pl.pallas_call(
        _merge_kernel,
        grid=(m // tm,),
        in_specs=[
            yspec, yspec, yspec, gspec(0), gspec(1), gspec(2),
            pl.BlockSpec((N_BRANCH, D_MODEL), lambda i: (0, 0)),
            pl.BlockSpec((N_BRANCH, BRANCH_W, D_MODEL), lambda i: (0, 0, 0)),
            pl.BlockSpec((D_MODEL, D_MODEL), lambda i: (0, 0)),
            pl.BlockSpec((tm, D_MODEL), lambda i: (i, 0)),
        ],
        out_specs=pl.BlockSpec((tm, D_MODEL), lambda i: (i, 0)),
        out_shape=jax.ShapeDtypeStruct((m, D_MODEL), F32),
        compiler_params=_cparams(1),
        name="merge",
    )(ya, yb, yc, big2d, big2d, big2d, gate_b, w_branch, w_out, x2d)


_IN_SPLIT = (("dn_qkvz", 4 * BRANCH_W), ("beta_decay", 2 * HEADS), ("pool_uz", 2 * BRANCH_W),
             ("att_qkvz", 4 * BRANCH_W), ("idx_q", HEADS * IDX_DIM), ("idx_k", IDX_DIM), ("idx_w", HEADS),
             ("gates", N_BRANCH * D_MODEL))
_IN_OFFSET = {name: sum(w for _, w in _IN_SPLIT[:i]) for i, (name, _) in enumerate(_IN_SPLIT)}

PERM_COLS = 128
_PERM_REGULAR = (BIG_W + HEADS * IDX_DIM) // PERM_COLS


def _perm_source(blk):
    shift = jnp.where(blk < COL_UB // PERM_COLS, _IN_OFFSET["dn_qkvz"] - COL_QA,
                      jnp.where(blk < COL_GATES // PERM_COLS, _IN_OFFSET["pool_uz"] - COL_UB,
                                jnp.where(blk < BIG_W // PERM_COLS, _IN_OFFSET["gates"] - COL_GATES,
                                          _IN_OFFSET["idx_q"] - BIG_W)))
    return jnp.where(blk < _PERM_REGULAR, blk * PERM_COLS + shift, 0)


def _permute_kernel(w_ref, ik_ref, bd_ref, iw_ref, o_ref):
    blk = pl.program_id(0)
    depth = o_ref.shape[0]

    @pl.when(blk < _PERM_REGULAR)
    def _():
        for l in range(depth):
            o_ref[l] = w_ref[:, l, :].T.astype(o_ref.dtype)

    @pl.when(blk == _PERM_REGULAR)
    def _():
        row = lax.broadcasted_iota(jnp.int32, (8, o_ref.shape[1]), 0)
        for l in range(depth):
            idx_w = jnp.where(row < HEADS, iw_ref[:, l, :], 0.0)
            pad = jnp.zeros((PERM_COLS - IDX_DIM - 16, o_ref.shape[1]), F32)
            rows = jnp.concatenate([ik_ref[:, l, :], bd_ref[:, l, :], idx_w, pad], axis=0)
            o_ref[l] = rows.T.astype(o_ref.dtype)


def _permute_w_in(w_in):
    depth, d, _ = w_in.shape
    w_t = jnp.transpose(w_in, (2, 0, 1))
    el = pl.Element
    fixed = lambda rows, start: pl.BlockSpec((el(rows), el(depth), el(d)), lambda i: (start, 0, 0))
    return pl.pallas_call(
        _permute_kernel,
        grid=(_PERM_REGULAR + 1,),
        in_specs=[
            pl.BlockSpec((el(PERM_COLS), el(depth), el(d)), lambda i: (_perm_source(i), 0, 0)),
            fixed(IDX_DIM, _IN_OFFSET["idx_k"]), fixed(8, _IN_OFFSET["beta_decay"]), fixed(8, _IN_OFFSET["idx_w"]),
        ],
        out_specs=pl.BlockSpec((depth, d, PERM_COLS), lambda i: (0, 0, i)),
        out_shape=jax.ShapeDtypeStruct((depth, d, BIG_W + SMALL_W), BF16),
        compiler_params=_cparams(1),
        name="permute_w_in",
    )(w_t, w_t, w_t, w_t)


def _rope_tables(t):
    def base(dim):
        inv_freq = ROPE_THETA ** (-jnp.arange(0, dim, 2, dtype=F32) / dim)
        ang = jnp.arange(t, dtype=F32)[:, None] * inv_freq[None, :]
        return jnp.cos(ang), jnp.sin(ang)

    cos_a, sin_a = base(HEAD_DIM)
    cos = jnp.concatenate([cos_a, cos_a], axis=-1)
    sin = jnp.concatenate([-sin_a, sin_a], axis=-1)
    cos_i, sin_i = base(IDX_DIM)
    zero = jnp.zeros_like(sin_i)
    icos = jnp.tile(jnp.concatenate([cos_i, cos_i], axis=-1), (1, HEADS))
    isin_up = jnp.tile(jnp.concatenate([-sin_i, zero], axis=-1), (1, HEADS))
    isin_dn = jnp.tile(jnp.concatenate([zero, sin_i], axis=-1), (1, HEADS))
    return cos, sin, icos, isin_up, isin_dn


def _lane_row(vals, offset):
    d, n = vals.shape
    return jnp.zeros((d, 1, HEAD_DIM), F32).at[:, 0, offset:offset + n].set(vals.astype(F32))


def kernel(x, norm_g, w_in, gate_b, conv_w, a_log, dt_bias, dn_onorm, pool_w, pool_scale, q_norm, k_norm,
           w_branch, w_out):
    b, t, d = x.shape
    depth = norm_g.shape[0]
    w_p = _permute_w_in(w_in)
    wb16 = w_branch.astype(BF16)
    wo16 = w_out.astype(BF16)
    pw16 = pool_w.astype(BF16)
    alog_rows = _lane_row(a_log, SM_DECAY)
    dtb_rows = _lane_row(dt_bias, SM_DECAY)
    cos, sin, *idx_tabs = _rope_tables(t)
    idx_tabs = tuple(idx_tabs)
    x2d = x.reshape(b * t, d)
    for layer in range(depth):
        big2d, small2d = _in_projection(x2d, norm_g[layer][None, :], w_p, conv_w[layer], layer, t // PROJ_TM)
        big3d = big2d.reshape(b, t, BIG_W)
        small3d = small2d.reshape(b, t, SMALL_W)
        ya = _deltanet_branch(big3d, small3d, alog_rows[layer], dtb_rows[layer], dn_onorm[layer][None, :])
        yb = _pool_branch(big3d, pw16[layer], pool_scale[layer][None, :])
        half = HEAD_DIM // 2
        gain_tabs = (cos * q_norm[layer], sin * jnp.roll(q_norm[layer], half),
                     cos * k_norm[layer], sin * jnp.roll(k_norm[layer], half))
        prep = _attn_prep(big3d, small3d, gain_tabs + idx_tabs)
        yc = _dsa_branch(big3d, *prep)
        x2d = _merge(ya.reshape(b * t, BRANCH_W), yb.reshape(b * t, BRANCH_W), yc.reshape(b * t, BRANCH_W),
                     big2d, gate_b[layer], wb16[layer], wo16[layer], x2d)
    return x2d.reshape(b, t, d)
```
